```python
import jax, jax.numpy as jnp
from jax import lax
import numpy as np

D_MODEL = 1024
BATCH = 16
SEQ = 2048
DEPTH = 1
DEC_BATCH = 16
DEC_SEQ = 64
PAST_LEN = 1024

CHUNK = 64
N_MEM = 256
EPS = 1e-6
ML_HEADS = 4
ML_DQK = 128
ML_DV = D_MODEL // ML_HEADS
FORGET_BIAS = 3.0
SGU_CHUNK = 128
SGU_GROUPS = 4
SGU_DIM = D_MODEL
XA_HEADS = 4
XA_DH = D_MODEL // XA_HEADS
N_BRANCH = 3
PEER_HEADS = 8
PEER_NKEYS = 128
PEER_DQ = 256
PEER_TOPK = 16
PEER_NEXP = PEER_NKEYS * PEER_NKEYS
PEER_BLOCK = 256
D_IN = 2 * ML_HEADS * ML_DQK + 2 * ML_HEADS * ML_DV + 2 * ML_HEADS + 2 * SGU_DIM + XA_HEADS * XA_DH + N_BRANCH * D_MODEL
F_START = 2 * ML_HEADS * ML_DQK + ML_HEADS * ML_DV + ML_HEADS

kernel_name = "mlstm_sgu_memxattn_peer_stream_step"


def _split_points():
    sizes = (ML_HEADS * ML_DQK, ML_HEADS * ML_DQK, ML_HEADS * ML_DV, ML_HEADS, ML_HEADS,
             ML_HEADS * ML_DV, SGU_DIM, SGU_DIM, XA_HEADS * XA_DH)
    pts, acc = [], 0
    for s in sizes:
        acc += s
        pts.append(acc)
    return pts


def _rmsnorm(x, g):
    xf = x.astype(jnp.float32)
    y = xf * lax.rsqrt(jnp.mean(xf * xf, axis=-1, keepdims=True) + EPS) * g.astype(jnp.float32)
    return y.astype(x.dtype)


def _mlstm(q, k, v, ig, lf, C0, n0, m0):
    B, S, H, _ = q.shape
    L = min(S, CHUNK)
    nc = S // L
    tril = jnp.tril(jnp.ones((L, L), dtype=bool))

    def to_chunks(a):
        return jnp.moveaxis(a.reshape((B, nc, L) + a.shape[2:]), 1, 0)

    def step(carry, inp):
        C, n, m = carry
        qc, kc, vc, igc, lfc = inp
        b = jnp.cumsum(lfc, axis=1)
        logD = b[:, :, None, :] - b[:, None, :, :] + igc[:, None, :, :]
        logD = jnp.where(tril[None, :, :, None], logD, -jnp.inf)
        log_prev = b + m[:, None, :]
        m_t = jnp.maximum(log_prev, jnp.max(logD, axis=2))
        A = jnp.einsum('bthd,bshd->btsh', qc, kc) * jnp.exp(logD - m_t[:, :, None, :])
        wp = jnp.exp(log_prev - m_t)
        num = jnp.einsum('btsh,bshv->bthv', A, vc) + wp[..., None] * jnp.einsum('bthd,bhdv->bthv', qc, C)
        den = jnp.sum(A, axis=2) + wp * jnp.einsum('bthd,bhd->bth', qc, n)
        h = num / jnp.maximum(jnp.abs(den), jnp.exp(-m_t))[..., None]
        bL = b[:, -1, :]
        log_in = bL[:, None, :] - b + igc
        m_new = jnp.maximum(bL + m, jnp.max(log_in, axis=1))
        wi = jnp.exp(log_in - m_new[:, None, :])
        wc = jnp.exp(bL + m - m_new)
        C_new = wc[..., None, None] * C + jnp.einsum('bsh,bshd,bshv->bhdv', wi, kc, vc)
        n_new = wc[..., None] * n + jnp.einsum('bsh,bshd->bhd', wi, kc)
        return (C_new, n_new, m_new), h

    xs = (to_chunks(q), to_chunks(k), to_chunks(v), to_chunks(ig), to_chunks(lf))
    (C1, n1, m1), h = lax.scan(step, (C0, n0, m0), xs)
    h = jnp.moveaxis(h, 0, 1).reshape(B, S, H, ML_DV)
    return h, C1, n1, m1


def _mem_kv(mem, g_mem, w_mk, w_mv):
    B = mem.shape[0]
    mn = _rmsnorm(mem, g_mem)
    mk = (mn @ w_mk).reshape(B, N_MEM, XA_HEADS, XA_DH)
    mv = (mn @ w_mv).reshape(B, N_MEM, XA_HEADS, XA_DH)
    return mk, mv


def _peer(xn, w_pq, k_sub1, k_sub2, u_exp, v_exp):
    B, S, D = xn.shape
    T = B * S
    nb = -(-T // PEER_BLOCK)
    xf = jnp.pad(xn.reshape(T, D), ((0, nb * PEER_BLOCK - T), (0, 0))).reshape(nb, PEER_BLOCK, D)
    half = PEER_DQ // 2

    def block(xb):
        q = (xb @ w_pq).reshape(PEER_BLOCK, PEER_HEADS, PEER_DQ)
        s1 = jnp.einsum('thd,kd->thk', q[..., :half], k_sub1).astype(jnp.float32)
        s2 = jnp.einsum('thd,kd->thk', q[..., half:], k_sub2).astype(jnp.float32)
        v1, i1 = lax.top_k(s1, PEER_TOPK)
        v2, i2 = lax.top_k(s2, PEER_TOPK)
        cand = (v1[..., :, None] + v2[..., None, :]).reshape(PEER_BLOCK, PEER_HEADS, PEER_TOPK * PEER_TOPK)
        cidx = (i1[..., :, None] * PEER_NKEYS + i2[..., None, :]).reshape(PEER_BLOCK, PEER_HEADS, PEER_TOPK * PEER_TOPK)
        sc, pos = lax.top_k(cand, PEER_TOPK)
        e = jnp.take_along_axis(cidx, pos, axis=-1).reshape(PEER_BLOCK, PEER_HEADS * PEER_TOPK)
        g = jax.nn.softmax(sc, axis=-1).reshape(PEER_BLOCK, PEER_HEADS * PEER_TOPK)
        a = jax.nn.gelu(jnp.einsum('tkd,td->tk', u_exp[e], xb).astype(jnp.float32))
        return jnp.einsum('tk,tkd->td', (g * a).astype(xb.dtype), v_exp[e])

    out = lax.map(block, xf).reshape(nb * PEER_BLOCK, D)[:T]
    return out.reshape(B, S, D)


def _layer(x, mem_k, mem_v, C0, n0, m0, g_mix, w_in, b_in, g_mlh, g_sgu, w_s, b_s, w_out,
           g_ffn, w_pq, k_sub1, k_sub2, u_exp, v_exp):
    B, S, D = x.shape
    xn = _rmsnorm(x, g_mix)
    p = xn @ w_in + b_in
    q, k, v, ig, fg, og, su, sv, xq, gates = jnp.split(p, _split_points(), axis=-1)

    f32 = jnp.float32
    qm = q.reshape(B, S, ML_HEADS, ML_DQK).astype(f32)
    km = k.reshape(B, S, ML_HEADS, ML_DQK).astype(f32) * (ML_DQK ** -0.5)
    vm = v.reshape(B, S, ML_HEADS, ML_DV).astype(f32)
    lf = jax.nn.log_sigmoid(fg.astype(f32))
    h, C1, n1, m1 = _mlstm(qm, km, vm, ig.astype(f32), lf, C0, n0, m0)
    h = h * lax.rsqrt(jnp.mean(h * h, axis=-1, keepdims=True) + EPS)
    h_a = (jax.nn.sigmoid(og.astype(f32)) * h.reshape(B, S, D) * g_mlh.astype(f32)).astype(x.dtype)

    u = jax.nn.gelu(su)
    vn = _rmsnorm(jax.nn.gelu(sv), g_sgu)
    L = min(S, SGU_CHUNK)
    nc = S // L
    vg = vn.reshape(B, nc, L, SGU_GROUPS, SGU_DIM // SGU_GROUPS)
    ws = w_s[:, :L, :L] * jnp.tril(jnp.ones((L, L), dtype=w_s.dtype))
    mix = jnp.einsum('gts,bnsgc->bntgc', ws, vg) + b_s[:, :L].T[None, None, :, :, None]
    h_b = u * mix.reshape(B, S, SGU_DIM)

    xq = xq.reshape(B, S, XA_HEADS, XA_DH)
    sc = jnp.einsum('bshd,bmhd->bhsm', xq, mem_k).astype(f32) * (XA_DH ** -0.5)
    att = jax.nn.softmax(sc, axis=-1).astype(x.dtype)
    h_c = jnp.einsum('bhsm,bmhd->bshd', att, mem_v).reshape(B, S, D)

    g = jax.nn.sigmoid(gates.astype(f32)).reshape(B, S, N_BRANCH, D)
    merged = (g[:, :, 0] * h_a + g[:, :, 1] * h_b + g[:, :, 2] * h_c).astype(x.dtype)
    x = x + merged @ w_out

    x = x + _peer(_rmsnorm(x, g_ffn), w_pq, k_sub1, k_sub2, u_exp, v_exp)
    return x, C1, n1, m1, vn


def setup_inputs(seed: int = 0) -> dict:
    key = jax.random.key(seed)
    ks = jax.random.split(key, 28)
    f32 = jnp.float32

    def nrm(k, shape, scale):
        return scale * jax.random.normal(k, shape, f32)

    def gain(k, shape):
        return 1.0 + 0.02 * jax.random.normal(k, shape, f32)

    D = D_MODEL
    b_in = nrm(ks[11], (DEPTH, D_IN), 0.02).at[:, F_START:F_START + ML_HEADS].add(FORGET_BIAS)
    return {
        "x_prompt": nrm(ks[0], (BATCH, SEQ, D), 1.0),
        "x_sample": nrm(ks[1], (DEC_BATCH, DEC_SEQ, D), 1.0),
        "mem_prompt": nrm(ks[2], (BATCH, N_MEM, D), 1.0),
        "cache_mem_k": nrm(ks[3], (DEPTH, DEC_BATCH, N_MEM, XA_HEADS, XA_DH), 1.0),
        "cache_mem_v": nrm(ks[4], (DEPTH, DEC_BATCH, N_MEM, XA_HEADS, XA_DH), 1.0),
        "state_mlstm_C": nrm(ks[5], (DEPTH, DEC_BATCH, ML_HEADS, ML_DQK, ML_DV), 0.05),
        "state_mlstm_n": nrm(ks[6], (DEPTH, DEC_BATCH, ML_HEADS, ML_DQK), 0.1),
        "state_mlstm_m": nrm(ks[7], (DEPTH, DEC_BATCH, ML_HEADS), 1.0),
        "g_mix": gain(ks[8], (DEPTH, D)),
        "w_in": nrm(ks[9], (DEPTH, D, D_IN), D ** -0.5),
        "b_in": b_in,
        "g_mlh": gain(ks[10], (DEPTH, D)),
        "g_sgu": gain(ks[12], (DEPTH, SGU_DIM)),
        "w_s": nrm(ks[13], (DEPTH, SGU_GROUPS, SGU_CHUNK, SGU_CHUNK), 0.5 * SGU_CHUNK ** -0.5),
        "b_s": gain(ks[14], (DEPTH, SGU_GROUPS, SGU_CHUNK)),
        "g_mem": gain(ks[15], (DEPTH, D)),
        "w_mk": nrm(ks[16], (DEPTH, D, XA_HEADS * XA_DH), D ** -0.5),
        "w_mv": nrm(ks[17], (DEPTH, D, XA_HEADS * XA_DH), D ** -0.5),
        "w_out": nrm(ks[18], (DEPTH, D, D), 0.5 * D ** -0.5),
        "g_ffn": gain(ks[19], (DEPTH, D)),
        "w_pq": nrm(ks[20], (DEPTH, D, PEER_HEADS * PEER_DQ), D ** -0.5),
        "k_sub1": nrm(ks[21], (DEPTH, PEER_NKEYS, PEER_DQ // 2), (PEER_DQ // 2) ** -0.5),
        "k_sub2": nrm(ks[22], (DEPTH, PEER_NKEYS, PEER_DQ // 2), (PEER_DQ // 2) ** -0.5),
        "u_exp": nrm(ks[23], (DEPTH, PEER_NEXP, D), D ** -0.5),
        "v_exp": nrm(ks[24], (DEPTH, PEER_NEXP, D), PEER_HEADS ** -0.5),
        "g_final": gain(ks[25], (D,)),
    }


def reference(x_prompt, x_sample, mem_prompt, cache_mem_k, cache_mem_v, state_mlstm_C, state_mlstm_n,
              state_mlstm_m, g_mix, w_in, b_in, g_mlh, g_sgu, w_s, b_s, g_mem, w_mk, w_mv, w_out, g_ffn,
              w_pq, k_sub1, k_sub2, u_exp, v_exp, g_final):
    f32 = jnp.float32
    Bp = x_prompt.shape[0]
    xp, xs = x_prompt, x_sample
    Cp, Np, Mp, MKp, MVp, Cs, Ns, Ms, Vs = [], [], [], [], [], [], [], [], []
    for l in range(DEPTH):
        w = (g_mix[l], w_in[l], b_in[l], g_mlh[l], g_sgu[l], w_s[l], b_s[l], w_out[l], g_ffn[l],
             w_pq[l], k_sub1[l], k_sub2[l], u_exp[l], v_exp[l])
        mk, mv = _mem_kv(mem_prompt, g_mem[l], w_mk[l], w_mv[l])
        C0 = jnp.zeros((Bp, ML_HEADS, ML_DQK, ML_DV), f32)
        n0 = jnp.zeros((Bp, ML_HEADS, ML_DQK), f32)
        m0 = jnp.zeros((Bp, ML_HEADS), f32)
        xp, c1, n1, m1, _ = _layer(xp, mk, mv, C0, n0, m0, *w)
        Cp.append(c1); Np.append(n1); Mp.append(m1); MKp.append(mk); MVp.append(mv)
        xs, c2, n2, m2, vn = _layer(xs, cache_mem_k[l], cache_mem_v[l], state_mlstm_C[l].astype(f32),
                                    state_mlstm_n[l].astype(f32), state_mlstm_m[l].astype(f32), *w)
        Cs.append(c2); Ns.append(n2); Ms.append(m2); Vs.append(vn)
    y_prompt = _rmsnorm(xp, g_final)
    y_sample = _rmsnorm(xs, g_final)
    new_mlstm_C_prompt = jnp.stack(Cp)
    new_mlstm_n_prompt = jnp.stack(Np)
    new_mlstm_m_prompt = jnp.stack(Mp)
    new_mem_k_prompt = jnp.stack(MKp)
    new_mem_v_prompt = jnp.stack(MVp)
    new_mlstm_C_sample = jnp.stack(Cs)
    new_mlstm_n_sample = jnp.stack(Ns)
    new_mlstm_m_sample = jnp.stack(Ms)
    new_sgu_v_sample = jnp.stack(Vs)
    return (y_prompt, y_sample, new_mlstm_C_prompt, new_mlstm_n_prompt, new_mlstm_m_prompt,
            new_mem_k_prompt, new_mem_v_prompt, new_mlstm_C_sample, new_mlstm_n_sample,
            new_mlstm_m_sample, new_sgu_v_sample)
```

```python
import functools

import jax
import jax.numpy as jnp
from jax import lax
from jax.experimental import pallas as pl
from jax.experimental.pallas import tpu as pltpu

D_MODEL = 1024
EPS = 1e-6
CHUNK = 64
N_MEM = 256
ML_HEADS = 4
ML_DQK = 128
ML_DV = D_MODEL // ML_HEADS
SGU_CHUNK = 128
SGU_GROUPS = 4
SGU_GDIM = D_MODEL // SGU_GROUPS
XA_HEADS = 4
XA_DH = D_MODEL // XA_HEADS
PEER_HEADS = 8
PEER_NKEYS = 128
PEER_DQ = 256
PEER_TOPK = 16
PEER_NEXP = PEER_NKEYS * PEER_NKEYS

_Q0 = 0
_K0 = _Q0 + ML_HEADS * ML_DQK
_V0 = _K0 + ML_HEADS * ML_DQK
_OG0 = _V0 + ML_HEADS * ML_DV
_SU0 = _OG0 + D_MODEL
_SV0 = _SU0 + D_MODEL
_XQ0 = _SV0 + D_MODEL
_GT0 = _XQ0 + D_MODEL
_P_COLS = _GT0 + 3 * D_MODEL

_VMEM_LIMIT = 56 * 1024 * 1024

_BF16 = jnp.bfloat16
_F32 = jnp.float32
_NEG_INF = float("-inf")


def _rms(xf, g):
    return xf * lax.rsqrt(jnp.mean(xf * xf, axis=-1, keepdims=True) + EPS) * g


def _dot(a, b):
    return jnp.dot(a, b, preferred_element_type=_F32)


def _dot_nt(a, b):
    return lax.dot_general(a, b, (((1,), (1,)), ((), ())), preferred_element_type=_F32)


def _dot_tn(a, b):
    return lax.dot_general(a, b, (((0,), (0,)), ((), ())), preferred_element_type=_F32)


def _mem_kv_kernel(mem_ref, g_ref, wk_ref, wv_ref, k_ref, v_ref):
    mn = _rms(mem_ref[0], g_ref[...]).astype(_BF16)
    k_ref[0] = _dot(mn, wk_ref[...])
    v_ref[0] = _dot(mn, wv_ref[...])


def _mem_kv(mem, g_mem, w_mk, w_mv):
    B = mem.shape[0]
    full = lambda b: (0, 0)
    return pl.pallas_call(
        _mem_kv_kernel,
        out_shape=(jax.ShapeDtypeStruct((B, N_MEM, D_MODEL), _F32),) * 2,
        grid=(B,),
        in_specs=[
            pl.BlockSpec((1, N_MEM, D_MODEL), lambda b: (b, 0, 0)),
            pl.BlockSpec((1, D_MODEL), full),
            pl.BlockSpec((D_MODEL, D_MODEL), full),
            pl.BlockSpec((D_MODEL, D_MODEL), full),
        ],
        out_specs=(pl.BlockSpec((1, N_MEM, D_MODEL), lambda b: (b, 0, 0)),) * 2,
        compiler_params=pltpu.CompilerParams(
            dimension_semantics=("arbitrary",), vmem_limit_bytes=_VMEM_LIMIT),
        name="mem_kv",
    )(mem, g_mem.reshape(1, D_MODEL), w_mk.astype(_BF16), w_mv.astype(_BF16))


def _mixer_kernel(x_ref, mk_ref, mv_ref, c0_ref, n0_ref, m0_ref, gmix_ref, win_ref, bin_ref,
                  wif_ref, wift_ref, bif_ref, bift_ref, gmlh_ref, gsgu_ref, ws_ref, bst_ref,
                  wout_ref,
                  x1_ref, c_out_ref, n_out_ref, m_out_ref, vn_ref,
                  p_scr, mrg_scr, c_scr, n_scr, m_scr, *, tc, sgu_len):
    ci = pl.program_id(1)
    nsub = tc // CHUNK

    @pl.when(ci == 0)
    def _():
        c_scr[...] = c0_ref[0]
        n_scr[...] = n0_ref[0]
        m_scr[...] = m0_ref[0]

    x = x_ref[0]
    xn = _rms(x, gmix_ref[...]).astype(_BF16)
    p_scr[...] = _dot(xn, win_ref[...]) + bin_ref[...]
    gif = _dot(xn, wif_ref[...]) + bif_ref[...]
    gift = _dot_nt(wift_ref[...], xn) + bift_ref[...]

    row = lax.broadcasted_iota(jnp.int32, (CHUNK, CHUNK), 0)
    col = lax.broadcasted_iota(jnp.int32, (CHUNK, CHUNK), 1)
    tril = row >= col
    tril_f = tril.astype(_F32)
    triu_f = (row <= col).astype(_F32)
    for j in range(nsub):
        r0 = j * CHUNK
        ig_c = gif[r0:r0 + CHUNK, 0:ML_HEADS]
        lf_c = jax.nn.log_sigmoid(gif[r0:r0 + CHUNK, ML_HEADS:2 * ML_HEADS])
        ig_r = gift[0:ML_HEADS, r0:r0 + CHUNK]
        lf_r = jax.nn.log_sigmoid(gift[ML_HEADS:2 * ML_HEADS, r0:r0 + CHUNK])
        b_c = jnp.dot(tril_f, lf_c, precision=lax.Precision.HIGHEST,
                      preferred_element_type=_F32)
        b_r = jnp.dot(lf_r, triu_f, precision=lax.Precision.HIGHEST,
                      preferred_element_type=_F32)
        for h in range(ML_HEADS):
            q = p_scr[r0:r0 + CHUNK, _Q0 + h * ML_DQK:_Q0 + (h + 1) * ML_DQK].astype(_BF16)
            k = p_scr[r0:r0 + CHUNK, _K0 + h * ML_DQK:_K0 + (h + 1) * ML_DQK] * (ML_DQK ** -0.5)
            v = p_scr[r0:r0 + CHUNK, _V0 + h * ML_DV:_V0 + (h + 1) * ML_DV].astype(_BF16)
            c_st = c_scr[h]
            n_st = n_scr[h:h + 1, :]
            m_st = m_scr[h:h + 1, 0:1]
            bc = b_c[:, h:h + 1]
            br = b_r[h:h + 1, :]
            logd = jnp.where(tril, bc - br + ig_r[h:h + 1, :], _NEG_INF)
            log_prev = bc + m_st
            m_t = jnp.maximum(log_prev, jnp.max(logd, axis=1, keepdims=True))
            a = _dot_nt(q, k.astype(_BF16)) * jnp.exp(logd - m_t)
            wp = jnp.exp(log_prev - m_t)
            num = _dot(a.astype(_BF16), v) + wp * _dot(q, c_st.astype(_BF16))
            qn = jnp.sum(q.astype(_F32) * n_st, axis=1, keepdims=True)
            den = jnp.sum(a, axis=1, keepdims=True) + wp * qn
            hh = num / jnp.maximum(jnp.abs(den), jnp.exp(-m_t))
            hn = hh * lax.rsqrt(jnp.mean(hh * hh, axis=1, keepdims=True) + EPS)
            sl = slice(h * ML_DV, (h + 1) * ML_DV)
            og = p_scr[r0:r0 + CHUNK, _OG0 + h * ML_DV:_OG0 + (h + 1) * ML_DV]
            h_a = (jax.nn.sigmoid(og) * hn * gmlh_ref[:, sl]).astype(_F32)
            g0 = jax.nn.sigmoid(p_scr[r0:r0 + CHUNK, _GT0 + h * ML_DV:_GT0 + (h + 1) * ML_DV])
            mrg_scr[r0:r0 + CHUNK, sl] = g0 * h_a
            b_last = bc[CHUNK - 1:CHUNK, :]
            log_in = b_last - bc + ig_c[:, h:h + 1]
            m_new = jnp.maximum(b_last + m_st, jnp.max(log_in, axis=0, keepdims=True))
            wi = jnp.exp(log_in - m_new)
            wc = jnp.exp(b_last + m_st - m_new)
            kw = wi * k
            c_scr[h] = wc * c_st + _dot_tn(kw.astype(_BF16), v)
            n_scr[h:h + 1, :] = wc * n_st + jnp.sum(kw, axis=0, keepdims=True)
            m_scr[h:h + 1, :] = jnp.broadcast_to(m_new, (1, 128))

    sv = jax.nn.gelu(p_scr[:, _SV0:_SV0 + D_MODEL])
    vn = _rms(sv, gsgu_ref[...])
    if vn_ref is not None:
        vn_ref[0] = vn
    vnb = vn.astype(_BF16)
    rs = lax.broadcasted_iota(jnp.int32, (sgu_len, sgu_len), 0)
    cs = lax.broadcasted_iota(jnp.int32, (sgu_len, sgu_len), 1)
    for g in range(SGU_GROUPS):
        wsg = jnp.where(rs >= cs, ws_ref[g, 0:sgu_len, 0:sgu_len], 0.0).astype(_BF16)
        bsg = bst_ref[0:sgu_len, g:g + 1]
        gl = slice(g * SGU_GDIM, (g + 1) * SGU_GDIM)
        for c in range(tc // sgu_len):
            rows = slice(c * sgu_len, (c + 1) * sgu_len)
            mix = _dot(wsg, vnb[rows, gl]) + bsg
            u = jax.nn.gelu(p_scr[rows, _SU0 + g * SGU_GDIM:_SU0 + (g + 1) * SGU_GDIM])
            g1 = jax.nn.sigmoid(
                p_scr[rows, _GT0 + D_MODEL + g * SGU_GDIM:_GT0 + D_MODEL + (g + 1) * SGU_GDIM])
            mrg_scr[rows, gl] += g1 * (u * mix)

    for h in range(XA_HEADS):
        hl = slice(h * XA_DH, (h + 1) * XA_DH)
        xq = p_scr[:, _XQ0 + h * XA_DH:_XQ0 + (h + 1) * XA_DH].astype(_BF16)
        sc = _dot_nt(xq, mk_ref[0, :, hl].astype(_BF16)) * (XA_DH ** -0.5)
        sc = sc - jnp.max(sc, axis=1, keepdims=True)
        e = jnp.exp(sc)
        att = e / jnp.sum(e, axis=1, keepdims=True)
        h_c = _dot(att.astype(_BF16), mv_ref[0, :, hl].astype(_BF16))
        g2 = jax.nn.sigmoid(p_scr[:, _GT0 + 2 * D_MODEL + h * XA_DH:_GT0 + 2 * D_MODEL + (h + 1) * XA_DH])
        mrg_scr[:, hl] += g2 * h_c

    x1_ref[0] = x + _dot(mrg_scr[...].astype(_BF16), wout_ref[...])

    @pl.when(ci == pl.num_programs(1) - 1)
    def _():
        c_out_ref[0] = c_scr[...]
        n_out_ref[0] = n_scr[...]
        m_out_ref[0] = m_scr[...]


def _mixer(x, mem_k, mem_v, c0, n0, m0, wts, *, tc, want_vn):
    B, S, D = x.shape
    sgu_len = min(S, SGU_CHUNK)
    assert S % tc == 0 and tc % CHUNK == 0 and tc % sgu_len == 0
    nchunks = S // tc
    m0p = jnp.broadcast_to(m0[:, :, None], (B, ML_HEADS, 128))
    m0p = jnp.concatenate([m0p, jnp.zeros((B, 8 - ML_HEADS, 128), _F32)], axis=1)

    def body(*refs):
        ins, rest = refs[:18], refs[18:]
        if want_vn:
            outs, scr = rest[:5], rest[5:]
        else:
            outs, scr = rest[:4] + (None,), rest[4:]
        _mixer_kernel(*ins, *outs, *scr, tc=tc, sgu_len=sgu_len)

    const2 = lambda b, c: (0, 0)
    const3 = lambda b, c: (0, 0, 0)
    per_b3 = lambda b, c: (b, 0, 0)
    per_b4 = lambda b, c: (b, 0, 0, 0)
    once = dict(pipeline_mode=pl.Buffered(1))
    in_specs = [
        pl.BlockSpec((1, tc, D), lambda b, c: (b, c, 0)),
        pl.BlockSpec((1, N_MEM, D), per_b3),
        pl.BlockSpec((1, N_MEM, D), per_b3),
        pl.BlockSpec((1, ML_HEADS, ML_DQK, ML_DV), per_b4),
        pl.BlockSpec((1, ML_HEADS, ML_DQK), per_b3),
        pl.BlockSpec((1, 8, 128), per_b3),
        pl.BlockSpec((1, D), const2, **once),
        pl.BlockSpec((D, _P_COLS), const2, **once),
        pl.BlockSpec((1, _P_COLS), const2, **once),
        pl.BlockSpec((D, 8), const2, **once),
        pl.BlockSpec((8, D), const2, **once),
        pl.BlockSpec((1, 8), const2, **once),
        pl.BlockSpec((8, 1), const2, **once),
        pl.BlockSpec((1, D), const2, **once),
        pl.BlockSpec((1, D), const2, **once),
        pl.BlockSpec((SGU_GROUPS, SGU_CHUNK, SGU_CHUNK), const3, **once),
        pl.BlockSpec((SGU_CHUNK, SGU_GROUPS), const2, **once),
        pl.BlockSpec((D, D), const2, **once),
    ]
    out_shape = [
        jax.ShapeDtypeStruct((B, S, D), _F32),
        jax.ShapeDtypeStruct((B, ML_HEADS, ML_DQK, ML_DV), _F32),
        jax.ShapeDtypeStruct((B, ML_HEADS, ML_DQK), _F32),
        jax.ShapeDtypeStruct((B, 8, 128), _F32),
    ]
    out_specs = [
        pl.BlockSpec((1, tc, D), lambda b, c: (b, c, 0)),
        pl.BlockSpec((1, ML_HEADS, ML_DQK, ML_DV), per_b4),
        pl.BlockSpec((1, ML_HEADS, ML_DQK), per_b3),
        pl.BlockSpec((1, 8, 128), per_b3),
    ]
    if want_vn:
        out_shape.append(jax.ShapeDtypeStruct((B, S, D), _F32))
        out_specs.append(pl.BlockSpec((1, tc, D), lambda b, c: (b, c, 0)))
    outs = pl.pallas_call(
        body,
        out_shape=tuple(out_shape),
        grid=(B, nchunks),
        in_specs=in_specs,
        out_specs=tuple(out_specs),
        scratch_shapes=[
            pltpu.VMEM((tc, _P_COLS), _F32),
            pltpu.VMEM((tc, D), _F32),
            pltpu.VMEM((ML_HEADS, ML_DQK, ML_DV), _F32),
            pltpu.VMEM((ML_HEADS, ML_DQK), _F32),
            pltpu.VMEM((8, 128), _F32),
        ],
        compiler_params=pltpu.CompilerParams(
            dimension_semantics=("arbitrary", "arbitrary"), vmem_limit_bytes=_VMEM_LIMIT),
        name="mixer",
    )(x, mem_k, mem_v, c0, n0, m0p, *wts)
    x1, c1, n1, m1p = outs[:4]
    vn = outs[4] if want_vn else None
    return x1, c1, n1, m1p[:, :ML_HEADS, 0], vn


def _peer_select_kernel(x1_ref, gffn_ref, wpq_ref, k1_ref, k2_ref,
                        xn_ref, theta_ref, p1n_ref, c2_ref, p2_ref,
                        v1_scr, v2_scr, cand_scr):
    xn = _rms(x1_ref[...], gffn_ref[...]).astype(_BF16)
    xn_ref[...] = xn
    k1 = k1_ref[...].astype(_BF16)
    k2 = k2_ref[...].astype(_BF16)
    half = PEER_DQ // 2

    def top16(c, out_scr):
        cur = c
        for r in range(PEER_TOPK):
            mx = jnp.max(cur, axis=0, keepdims=True)
            out_scr[r:r + 1, :] = mx
            cur = jnp.where(cur == mx, _NEG_INF, cur)

    def head(h, carry):
        q = _dot(xn, wpq_ref[h])
        s1 = _dot_nt(k1, q[:, :half].astype(_BF16))
        s2 = _dot_nt(k2, q[:, half:].astype(_BF16))
        c1 = s1 - jnp.max(s1, axis=0, keepdims=True)
        c2 = s2 - jnp.max(s2, axis=0, keepdims=True)
        top16(c1, v1_scr)
        top16(c2, v2_scr)
        for a in range(PEER_TOPK):
            cand_scr[a * PEER_TOPK:(a + 1) * PEER_TOPK, :] = v1_scr[a:a + 1, :] + v2_scr[...]
        cand = cand_scr[...]
        cur = cand
        tau = None
        for r in range(PEER_TOPK):
            tau = jnp.max(cur, axis=0, keepdims=True)
            cur = jnp.where(cur == tau, _NEG_INF, cur)
        z = jnp.sum(jnp.where(cand >= tau, jnp.exp(cand), 0.0), axis=0, keepdims=True)
        theta_ref[h] = tau - c1
        p1n_ref[h] = jnp.exp(c1) / z
        c2_ref[h] = c2
        p2_ref[h] = jnp.exp(c2)
        return carry

    lax.fori_loop(0, PEER_HEADS, head, 0)


def _peer_select(x1, g_ffn, wpq_h, k_sub1, k_sub2, *, tb):
    T, D = x1.shape
    assert T % tb == 0
    sel_shape = jax.ShapeDtypeStruct((PEER_HEADS, PEER_NKEYS, T), _F32)
    sel_spec = pl.BlockSpec((PEER_HEADS, PEER_NKEYS, tb), lambda i: (0, 0, i))
    const2 = lambda i: (0, 0)
    return pl.pallas_call(
        _peer_select_kernel,
        out_shape=(jax.ShapeDtypeStruct((T, D), _BF16),) + (sel_shape,) * 4,
        grid=(T // tb,),
        in_specs=[
            pl.BlockSpec((tb, D), lambda i: (i, 0)),
            pl.BlockSpec((1, D), const2),
            pl.BlockSpec((PEER_HEADS, D, PEER_DQ), lambda i: (0, 0, 0)),
            pl.BlockSpec((PEER_NKEYS, PEER_DQ // 2), const2),
            pl.BlockSpec((PEER_NKEYS, PEER_DQ // 2), const2),
        ],
        out_specs=(pl.BlockSpec((tb, D), lambda i: (i, 0)),) + (sel_spec,) * 4,
        scratch_shapes=[
            pltpu.VMEM((PEER_TOPK, tb), _F32),
            pltpu.VMEM((PEER_TOPK, tb), _F32),
            pltpu.VMEM((PEER_TOPK * PEER_TOPK, tb), _F32),
        ],
        compiler_params=pltpu.CompilerParams(
            dimension_semantics=("arbitrary",), vmem_limit_bytes=_VMEM_LIMIT),
        name="peer_select",
    )(x1, g_ffn.reshape(1, D), wpq_h, k_sub1, k_sub2)


def _peer_dense_kernel(xn_ref, u_ref, vt_ref, theta_ref, p1n_ref, c2_ref, p2_ref, x1_ref, gfin_ref,
                       y_ref, acc_scr, st_scr, h_scr, *, tb, rows):
    j = pl.program_id(1)

    @pl.when(j == 0)
    def _():
        acc_scr[...] = jnp.zeros_like(acc_scr)

    st_scr[...] = _dot_nt(u_ref[...], xn_ref[...])

    for r in range(rows):
        es = slice(r * PEER_NKEYS, (r + 1) * PEER_NKEYS)
        for lg in range(tb // 128):
            ls = slice(lg * 128, (lg + 1) * 128)
            gate = jnp.zeros((PEER_NKEYS, 128), _F32)
            for h in range(PEER_HEADS):
                th = theta_ref[h, r:r + 1, ls]
                pn = p1n_ref[h, r:r + 1, ls]
                gate = gate + jnp.where(c2_ref[h, :, ls] >= th, pn * p2_ref[h, :, ls], 0.0)
            act = jax.nn.gelu(st_scr[es, ls])
            h_scr[es, ls] = (gate * act).astype(_BF16)
    acc_scr[...] += _dot(vt_ref[...], h_scr[...])

    @pl.when(j == pl.num_programs(1) - 1)
    def _():
        x2 = x1_ref[...] + acc_scr[...].T
        y_ref[...] = _rms(x2, gfin_ref[...])


def _peer_dense(xn, u_bf, vt_bf, theta, p1n, c2, p2, x1, g_final, *, tb, ec):
    T, D = x1.shape
    rows = ec // PEER_NKEYS
    assert T % tb == 0 and PEER_NEXP % ec == 0 and rows % 8 == 0
    sel_all = pl.BlockSpec((PEER_HEADS, PEER_NKEYS, tb), lambda i, j: (0, 0, i))
    sel_rows = pl.BlockSpec((PEER_HEADS, rows, tb), lambda i, j: (0, j, i))
    tok = pl.BlockSpec((tb, D), lambda i, j: (i, 0))
    return pl.pallas_call(
        functools.partial(_peer_dense_kernel, tb=tb, rows=rows),
        out_shape=jax.ShapeDtypeStruct((T, D), _F32),
        grid=(T // tb, PEER_NEXP // ec),
        in_specs=[
            tok,
            pl.BlockSpec((ec, D), lambda i, j: (j, 0)),
            pl.BlockSpec((D, ec), lambda i, j: (0, j)),
            sel_rows, sel_rows, sel_all, sel_all,
            tok,
            pl.BlockSpec((1, D), lambda i, j: (0, 0)),
        ],
        out_specs=tok,
        scratch_shapes=[
            pltpu.VMEM((D, tb), _F32),
            pltpu.VMEM((ec, tb), _F32),
            pltpu.VMEM((ec, tb), _BF16),
        ],
        compiler_params=pltpu.CompilerParams(
            dimension_semantics=("arbitrary", "arbitrary"), vmem_limit_bytes=_VMEM_LIMIT),
        name="peer_dense",
    )(xn, u_bf, vt_bf, theta, p1n, c2, p2, x1, g_final.reshape(1, D))


def _mixer_weights(g_mix, w_in, b_in, g_mlh, g_sgu, w_s, b_s, w_out):
    nq = ML_HEADS * ML_DQK
    nv = ML_HEADS * ML_DV
    o_q, o_k, o_v = 0, nq, 2 * nq
    o_ig = o_v + nv
    o_fg = o_ig + ML_HEADS
    o_og = o_fg + ML_HEADS
    o_su = o_og + nv
    o_sv = o_su + D_MODEL
    o_xq = o_sv + D_MODEL
    o_gt = o_xq + D_MODEL
    main = lambda a: jnp.concatenate([a[..., o_q:o_ig], a[..., o_og:o_gt + 3 * D_MODEL]], axis=-1)
    w_if = w_in[:, o_ig:o_og]
    b_if = b_in[o_ig:o_og]
    return (
        g_mix.reshape(1, D_MODEL),
        main(w_in).astype(_BF16),
        main(b_in).reshape(1, _P_COLS),
        w_if.astype(_BF16),
        w_if.T.astype(_BF16),
        b_if.reshape(1, 8),
        b_if.reshape(8, 1),
        g_mlh.reshape(1, D_MODEL),
        g_sgu.reshape(1, D_MODEL),
        w_s,
        b_s.T,
        w_out.astype(_BF16),
    )


def kernel(x_prompt, x_sample, mem_prompt, cache_mem_k, cache_mem_v, state_mlstm_C, state_mlstm_n,
           state_mlstm_m, g_mix, w_in, b_in, g_mlh, g_sgu, w_s, b_s, g_mem, w_mk, w_mv, w_out, g_ffn,
           w_pq, k_sub1, k_sub2, u_exp, v_exp, g_final):
    depth = g_mix.shape[0]
    assert depth == 1
    l = 0
    Bp, Sp, D = x_prompt.shape
    Bs, Ss, _ = x_sample.shape

    wts = _mixer_weights(g_mix[l], w_in[l], b_in[l], g_mlh[l], g_sgu[l], w_s[l], b_s[l], w_out[l])

    mk, mv = _mem_kv(mem_prompt, g_mem[l], w_mk[l], w_mv[l])
    zc = jnp.zeros((Bp, ML_HEADS, ML_DQK, ML_DV), _F32)
    zn = jnp.zeros((Bp, ML_HEADS, ML_DQK), _F32)
    zm = jnp.zeros((Bp, ML_HEADS), _F32)
    x1p, cp, np_, mp, _ = _mixer(x_prompt, mk, mv, zc, zn, zm, wts, tc=256, want_vn=False)
    x1s, cs, ns, ms, vn = _mixer(
        x_sample, cache_mem_k[l].reshape(Bs, N_MEM, D), cache_mem_v[l].reshape(Bs, N_MEM, D),
        state_mlstm_C[l], state_mlstm_n[l], state_mlstm_m[l], wts, tc=Ss, want_vn=True)

    tp, ts = Bp * Sp, Bs * Ss
    x1 = jnp.concatenate([x1p.reshape(tp, D), x1s.reshape(ts, D)], axis=0)
    wpq_h = w_pq[l].reshape(D, PEER_HEADS, PEER_DQ).transpose(1, 0, 2).astype(_BF16)
    xn, theta, p1n, c2, p2 = _peer_select(x1, g_ffn[l], wpq_h, k_sub1[l], k_sub2[l], tb=256)
    y = _peer_dense(xn, u_exp[l].astype(_BF16), v_exp[l].T.astype(_BF16), theta, p1n, c2, p2,
                    x1, g_final, tb=512, ec=1024)

    y_prompt = y[:tp].reshape(Bp, Sp, D)
    y_sample = y[tp:].reshape(Bs, Ss, D)
    hs = (XA_HEADS, XA_DH)
    return (y_prompt, y_sample, cp[None], np_[None], mp[None],
            mk.reshape(1, Bp, N_MEM, *hs), mv.reshape(1, Bp, N_MEM, *hs),
            cs[None], ns[None], ms[None], vn[None])
```

```python
import functools

import jax
import jax.numpy as jnp
from jax import lax
from jax.experimental import pallas as pl
from jax.experimental.pallas import tpu as pltpu

D_MODEL = 1024
EPS = 1e-6
CHUNK = 64
N_MEM = 256
ML_HEADS = 4
ML_DQK = 128
ML_DV = D_MODEL // ML_HEADS
SGU_CHUNK = 128
SGU_GROUPS = 4
SGU_GDIM = D_MODEL // SGU_GROUPS
XA_HEADS = 4
XA_DH = D_MODEL // XA_HEADS
PEER_HEADS = 8
PEER_NKEYS = 128
PEER_DQ = 256
PEER_TOPK = 16
PEER_NEXP = PEER_NKEYS * PEER_NKEYS

_Q0 = 0
_K0 = _Q0 + ML_HEADS * ML_DQK
_V0 = _K0 + ML_HEADS * ML_DQK
_OG0 = _V0 + ML_HEADS * ML_DV
_SU0 = _OG0 + D_MODEL
_SV0 = _SU0 + D_MODEL
_XQ0 = _SV0 + D_MODEL
_GT0 = _XQ0 + D_MODEL
_P_COLS = _GT0 + 3 * D_MODEL

_VMEM_LIMIT = 56 * 1024 * 1024

_BF16 = jnp.bfloat16
_F32 = jnp.float32
_NEG_INF = float("-inf")


def _rms(xf, g):
    return xf * lax.rsqrt(jnp.mean(xf * xf, axis=-1, keepdims=True) + EPS) * g


def _dot(a, b):
    return jnp.dot(a, b, preferred_element_type=_F32)


def _dot_nt(a, b):
    return lax.dot_general(a, b, (((1,), (1,)), ((), ())), preferred_element_type=_F32)


def _dot_tn(a, b):
    return lax.dot_general(a, b, (((0,), (0,)), ((), ())), preferred_element_type=_F32)


def _mem_kv_kernel(mem_ref, g_ref, wk_ref, wv_ref, k_ref, v_ref):
    mn = _rms(mem_ref[0], g_ref[...]).astype(_BF16)
    k_ref[0] = _dot(mn, wk_ref[...])
    v_ref[0] = _dot(mn, wv_ref[...])


def _mem_kv(mem, g_mem, w_mk, w_mv):
    B = mem.shape[0]
    full = lambda b: (0, 0)
    return pl.pallas_call(
        _mem_kv_kernel,
        out_shape=(jax.ShapeDtypeStruct((B, N_MEM, D_MODEL), _F32),) * 2,
        grid=(B,),
        in_specs=[
            pl.BlockSpec((1, N_MEM, D_MODEL), lambda b: (b, 0, 0)),
            pl.BlockSpec((1, D_MODEL), full),
            pl.BlockSpec((D_MODEL, D_MODEL), full),
            pl.BlockSpec((D_MODEL, D_MODEL), full),
        ],
        out_specs=(pl.BlockSpec((1, N_MEM, D_MODEL), lambda b: (b, 0, 0)),) * 2,
        compiler_params=pltpu.CompilerParams(
            dimension_semantics=("arbitrary",), vmem_limit_bytes=_VMEM_LIMIT),
        name="mem_kv",
    )(mem, g_mem.reshape(1, D_MODEL), w_mk.astype(_BF16), w_mv.astype(_BF16))


def _mixer_kernel(x_ref, mk_ref, mv_ref, c0_ref, n0_ref, m0_ref, gmix_ref, win_ref, bin_ref,
                  wif_ref, wift_ref, bif_ref, bift_ref, gmlh_ref, gsgu_ref, ws_ref, bst_ref,
                  wout_ref,
                  x1_ref, c_out_ref, n_out_ref, m_out_ref, vn_ref,
                  p_scr, mrg_scr, c_scr, n_scr, m_scr, *, tc, sgu_len):
    ci = pl.program_id(1)
    nsub = tc // CHUNK

    @pl.when(ci == 0)
    def _():
        c_scr[...] = c0_ref[0]
        n_scr[...] = n0_ref[0]
        m_scr[...] = m0_ref[0]

    x = x_ref[0]
    xn = _rms(x, gmix_ref[...]).astype(_BF16)
    p_scr[...] = _dot(xn, win_ref[...]) + bin_ref[...]
    gif = _dot(xn, wif_ref[...]) + bif_ref[...]
    gift = _dot_nt(wift_ref[...], xn) + bift_ref[...]

    row = lax.broadcasted_iota(jnp.int32, (CHUNK, CHUNK), 0)
    col = lax.broadcasted_iota(jnp.int32, (CHUNK, CHUNK), 1)
    tril = row >= col
    tril_f = tril.astype(_F32)
    triu_f = (row <= col).astype(_F32)
    for j in range(nsub):
        r0 = j * CHUNK
        ig_c = gif[r0:r0 + CHUNK, 0:ML_HEADS]
        lf_c = jax.nn.log_sigmoid(gif[r0:r0 + CHUNK, ML_HEADS:2 * ML_HEADS])
        ig_r = gift[0:ML_HEADS, r0:r0 + CHUNK]
        lf_r = jax.nn.log_sigmoid(gift[ML_HEADS:2 * ML_HEADS, r0:r0 + CHUNK])
        b_c = jnp.dot(tril_f, lf_c, precision=lax.Precision.HIGHEST,
                      preferred_element_type=_F32)
        b_r = jnp.dot(lf_r, triu_f, precision=lax.Precision.HIGHEST,
                      preferred_element_type=_F32)
        for h in range(ML_HEADS):
            q = p_scr[r0:r0 + CHUNK, _Q0 + h * ML_DQK:_Q0 + (h + 1) * ML_DQK].astype(_BF16)
            k = p_scr[r0:r0 + CHUNK, _K0 + h * ML_DQK:_K0 + (h + 1) * ML_DQK] * (ML_DQK ** -0.5)
            v = p_scr[r0:r0 + CHUNK, _V0 + h * ML_DV:_V0 + (h + 1) * ML_DV].astype(_BF16)
            c_st = c_scr[h]
            n_st = n_scr[h:h + 1, :]
            m_st = m_scr[h:h + 1, 0:1]
            bc = b_c[:, h:h + 1]
            br = b_r[h:h + 1, :]
            logd = jnp.where(tril, bc - br + ig_r[h:h + 1, :], _NEG_INF)
            log_prev = bc + m_st
            m_t = jnp.maximum(log_prev, jnp.max(logd, axis=1, keepdims=True))
            a = _dot_nt(q, k.astype(_BF16)) * jnp.exp(logd - m_t)
            wp = jnp.exp(log_prev - m_t)
            num = _dot(a.astype(_BF16), v) + wp * _dot(q, c_st.astype(_BF16))
            qn = jnp.sum(q.astype(_F32) * n_st, axis=1, keepdims=True)
            den = jnp.sum(a, axis=1, keepdims=True) + wp * qn
            hh = num / jnp.maximum(jnp.abs(den), jnp.exp(-m_t))
            hn = hh * lax.rsqrt(jnp.mean(hh * hh, axis=1, keepdims=True) + EPS)
            sl = slice(h * ML_DV, (h + 1) * ML_DV)
            og = p_scr[r0:r0 + CHUNK, _OG0 + h * ML_DV:_OG0 + (h + 1) * ML_DV]
            h_a = (jax.nn.sigmoid(og) * hn * gmlh_ref[:, sl]).astype(_F32)
            g0 = jax.nn.sigmoid(p_scr[r0:r0 + CHUNK, _GT0 + h * ML_DV:_GT0 + (h + 1) * ML_DV])
            mrg_scr[r0:r0 + CHUNK, sl] = g0 * h_a
            b_last = bc[CHUNK - 1:CHUNK, :]
            log_in = b_last - bc + ig_c[:, h:h + 1]
            m_new = jnp.maximum(b_last + m_st, jnp.max(log_in, axis=0, keepdims=True))
            wi = jnp.exp(log_in - m_new)
            wc = jnp.exp(b_last + m_st - m_new)
            kw = wi * k
            c_scr[h] = wc * c_st + _dot_tn(kw.astype(_BF16), v)
            n_scr[h:h + 1, :] = wc * n_st + jnp.sum(kw, axis=0, keepdims=True)
            m_scr[h:h + 1, :] = jnp.broadcast_to(m_new, (1, 128))

    sv = jax.nn.gelu(p_scr[:, _SV0:_SV0 + D_MODEL])
    vn = _rms(sv, gsgu_ref[...])
    if vn_ref is not None:
        vn_ref[0] = vn
    vnb = vn.astype(_BF16)
    rs = lax.broadcasted_iota(jnp.int32, (sgu_len, sgu_len), 0)
    cs = lax.broadcasted_iota(jnp.int32, (sgu_len, sgu_len), 1)
    for g in range(SGU_GROUPS):
        wsg = jnp.where(rs >= cs, ws_ref[g, 0:sgu_len, 0:sgu_len], 0.0).astype(_BF16)
        bsg = bst_ref[0:sgu_len, g:g + 1]
        gl = slice(g * SGU_GDIM, (g + 1) * SGU_GDIM)
        for c in range(tc // sgu_len):
            rows = slice(c * sgu_len, (c + 1) * sgu_len)
            mix = _dot(wsg, vnb[rows, gl]) + bsg
            u = jax.nn.gelu(p_scr[rows, _SU0 + g * SGU_GDIM:_SU0 + (g + 1) * SGU_GDIM])
            g1 = jax.nn.sigmoid(
                p_scr[rows, _GT0 + D_MODEL + g * SGU_GDIM:_GT0 + D_MODEL + (g + 1) * SGU_GDIM])
            mrg_scr[rows, gl] += g1 * (u * mix)

    for h in range(XA_HEADS):
        hl = slice(h * XA_DH, (h + 1) * XA_DH)
        xq = p_scr[:, _XQ0 + h * XA_DH:_XQ0 + (h + 1) * XA_DH].astype(_BF16)
        sc = _dot_nt(xq, mk_ref[0, :, hl].astype(_BF16)) * (XA_DH ** -0.5)
        sc = sc - jnp.max(sc, axis=1, keepdims=True)
        e = jnp.exp(sc)
        att = e / jnp.sum(e, axis=1, keepdims=True)
        h_c = _dot(att.astype(_BF16), mv_ref[0, :, hl].astype(_BF16))
        g2 = jax.nn.sigmoid(p_scr[:, _GT0 + 2 * D_MODEL + h * XA_DH:_GT0 + 2 * D_MODEL + (h + 1) * XA_DH])
        mrg_scr[:, hl] += g2 * h_c

    x1_ref[0] = x + _dot(mrg_scr[...].astype(_BF16), wout_ref[...])

    @pl.when(ci == pl.num_programs(1) - 1)
    def _():
        c_out_ref[0] = c_scr[...]
        n_out_ref[0] = n_scr[...]
        m_out_ref[0] = m_scr[...]


def _mixer(x, mem_k, mem_v, c0, n0, m0, wts, *, tc, want_vn):
    B, S, D = x.shape
    sgu_len = min(S, SGU_CHUNK)
    assert S % tc == 0 and tc % CHUNK == 0 and tc % sgu_len == 0
    nchunks = S // tc
    m0p = jnp.broadcast_to(m0[:, :, None], (B, ML_HEADS, 128))
    m0p = jnp.concatenate([m0p, jnp.zeros((B, 8 - ML_HEADS, 128), _F32)], axis=1)

    def body(*refs):
        ins, rest = refs[:18], refs[18:]
        if want_vn:
            outs, scr = rest[:5], rest[5:]
        else:
            outs, scr = rest[:4] + (None,), rest[4:]
        _mixer_kernel(*ins, *outs, *scr, tc=tc, sgu_len=sgu_len)

    const2 = lambda b, c: (0, 0)
    const3 = lambda b, c: (0, 0, 0)
    per_b3 = lambda b, c: (b, 0, 0)
    per_b4 = lambda b, c: (b, 0, 0, 0)
    once = dict(pipeline_mode=pl.Buffered(1))
    in_specs = [
        pl.BlockSpec((1, tc, D), lambda b, c: (b, c, 0)),
        pl.BlockSpec((1, N_MEM, D), per_b3),
        pl.BlockSpec((1, N_MEM, D), per_b3),
        pl.BlockSpec((1, ML_HEADS, ML_DQK, ML_DV), per_b4),
        pl.BlockSpec((1, ML_HEADS, ML_DQK), per_b3),
        pl.BlockSpec((1, 8, 128), per_b3),
        pl.BlockSpec((1, D), const2, **once),
        pl.BlockSpec((D, _P_COLS), const2, **once),
        pl.BlockSpec((1, _P_COLS), const2, **once),
        pl.BlockSpec((D, 8), const2, **once),
        pl.BlockSpec((8, D), const2, **once),
        pl.BlockSpec((1, 8), const2, **once),
        pl.BlockSpec((8, 1), const2, **once),
        pl.BlockSpec((1, D), const2, **once),
        pl.BlockSpec((1, D), const2, **once),
        pl.BlockSpec((SGU_GROUPS, SGU_CHUNK, SGU_CHUNK), const3, **once),
        pl.BlockSpec((SGU_CHUNK, SGU_GROUPS), const2, **once),
        pl.BlockSpec((D, D), const2, **once),
    ]
    out_shape = [
        jax.ShapeDtypeStruct((B, S, D), _F32),
        jax.ShapeDtypeStruct((B, ML_HEADS, ML_DQK, ML_DV), _F32),
        jax.ShapeDtypeStruct((B, ML_HEADS, ML_DQK), _F32),
        jax.ShapeDtypeStruct((B, 8, 128), _F32),
    ]
    out_specs = [
        pl.BlockSpec((1, tc, D), lambda b, c: (b, c, 0)),
        pl.BlockSpec((1, ML_HEADS, ML_DQK, ML_DV), per_b4),
        pl.BlockSpec((1, ML_HEADS, ML_DQK), per_b3),
        pl.BlockSpec((1, 8, 128), per_b3),
    ]
    if want_vn:
        out_shape.append(jax.ShapeDtypeStruct((B, S, D), _F32))
        out_specs.append(pl.BlockSpec((1, tc, D), lambda b, c: (b, c, 0)))
    outs = pl.pallas_call(
        body,
        out_shape=tuple(out_shape),
        grid=(B, nchunks),
        in_specs=in_specs,
        out_specs=tuple(out_specs),
        scratch_shapes=[
            pltpu.VMEM((tc, _P_COLS), _F32),
            pltpu.VMEM((tc, D), _F32),
            pltpu.VMEM((ML_HEADS, ML_DQK, ML_DV), _F32),
            pltpu.VMEM((ML_HEADS, ML_DQK), _F32),
            pltpu.VMEM((8, 128), _F32),
        ],
        compiler_params=pltpu.CompilerParams(
            dimension_semantics=("arbitrary", "arbitrary"), vmem_limit_bytes=_VMEM_LIMIT),
        name="mixer",
    )(x, mem_k, mem_v, c0, n0, m0p, *wts)
    x1, c1, n1, m1p = outs[:4]
    vn = outs[4] if want_vn else None
    return x1, c1, n1, m1p[:, :ML_HEADS, 0], vn


def _peer_select_kernel(x1_ref, gffn_ref, wpq_ref, k1_ref, k2_ref,
                        xn_ref, cnt_ref, p1n_ref, rank_ref, p2_ref,
                        c1_scr, c2_scr, v1_scr, v2_scr, cand_scr, *, tb):
    xn = _rms(x1_ref[...], gffn_ref[...]).astype(_BF16)
    xn_ref[...] = xn
    k1 = k1_ref[...].astype(_BF16)
    k2 = k2_ref[...].astype(_BF16)
    half = PEER_DQ // 2
    K = PEER_TOPK

    def head(h, carry):
        q = _dot(xn, wpq_ref[h])
        s1 = _dot_nt(k1, q[:, :half].astype(_BF16))
        s2 = _dot_nt(k2, q[:, half:].astype(_BF16))
        c1_scr[...] = s1 - jnp.max(s1, axis=0, keepdims=True)
        c2_scr[...] = s2 - jnp.max(s2, axis=0, keepdims=True)

        def lane_group(lg, carry):
            ls = pl.ds(pl.multiple_of(lg * 128, 128), 128)
            c1 = c1_scr[:, ls]
            c2 = c2_scr[:, ls]
            cur1, cur2 = c1, c2
            rank2 = jnp.full(c2.shape, float(K), _F32)
            for r in range(K):
                mx1 = jnp.max(cur1, axis=0, keepdims=True)
                mx2 = jnp.max(cur2, axis=0, keepdims=True)
                v1_scr[r:r + 1, :] = mx1
                v2_scr[r:r + 1, :] = mx2
                cur1 = jnp.where(cur1 == mx1, _NEG_INF, cur1)
                eq2 = cur2 == mx2
                rank2 = jnp.where(eq2, float(r), rank2)
                cur2 = jnp.where(eq2, _NEG_INF, cur2)
            for b in range(8):
                cand_scr[8 * b:8 * b + 8, :] = v1_scr[0:8, :] + v2_scr[b:b + 1, :]
            cand_scr[64:72, :] = v1_scr[8:16, :] + v2_scr[0:1, :]
            cand_scr[72:80, :] = v1_scr[0:1, :] + v2_scr[8:16, :]
            cand = cand_scr[...]
            cur = cand
            tau = None
            for r in range(K):
                tau = jnp.max(cur, axis=0, keepdims=True)
                cur = jnp.where(cur == tau, _NEG_INF, cur)
            z = jnp.sum(jnp.where(cand >= tau, jnp.exp(cand), 0.0), axis=0, keepdims=True)
            cnt = jnp.zeros(c1.shape, _F32)
            for b in range(K):
                cnt = jnp.where(c1 + v2_scr[b:b + 1, :] >= tau, float(b + 1), cnt)
            cnt_ref[h, :, ls] = cnt
            p1n_ref[h, :, ls] = jnp.exp(c1) / z
            rank_ref[h, :, ls] = rank2.astype(_BF16)
            p2_ref[h, :, ls] = jnp.exp(c2).astype(_BF16)
            return carry

        lax.fori_loop(0, tb // 128, lane_group, 0)
        return carry

    lax.fori_loop(0, PEER_HEADS, head, 0)


def _peer_select(x1, g_ffn, wpq_h, k_sub1, k_sub2, *, tb):
    T, D = x1.shape
    assert T % tb == 0
    sel = lambda dt: jax.ShapeDtypeStruct((PEER_HEADS, PEER_NKEYS, T), dt)
    sel_spec = pl.BlockSpec((PEER_HEADS, PEER_NKEYS, tb), lambda i: (0, 0, i))
    const2 = lambda i: (0, 0)
    return pl.pallas_call(
        functools.partial(_peer_select_kernel, tb=tb),
        out_shape=(jax.ShapeDtypeStruct((T, D), _BF16), sel(_F32), sel(_F32), sel(_BF16), sel(_BF16)),
        grid=(T // tb,),
        in_specs=[
            pl.BlockSpec((tb, D), lambda i: (i, 0)),
            pl.BlockSpec((1, D), const2),
            pl.BlockSpec((PEER_HEADS, D, PEER_DQ), lambda i: (0, 0, 0)),
            pl.BlockSpec((PEER_NKEYS, PEER_DQ // 2), const2),
            pl.BlockSpec((PEER_NKEYS, PEER_DQ // 2), const2),
        ],
        out_specs=(pl.BlockSpec((tb, D), lambda i: (i, 0)),) + (sel_spec,) * 4,
        scratch_shapes=[
            pltpu.VMEM((PEER_NKEYS, tb), _F32),
            pltpu.VMEM((PEER_NKEYS, tb), _F32),
            pltpu.VMEM((PEER_TOPK, 128), _F32),
            pltpu.VMEM((PEER_TOPK, 128), _F32),
            pltpu.VMEM((80, 128), _F32),
        ],
        compiler_params=pltpu.CompilerParams(
            dimension_semantics=("arbitrary",), vmem_limit_bytes=_VMEM_LIMIT),
        name="peer_select",
    )(x1, g_ffn.reshape(1, D), wpq_h, k_sub1, k_sub2)


def _peer_dense_kernel(xn_ref, u_ref, vt_ref, cnt_ref, p1n_ref, rank_ref, p2_ref, x1_ref, gfin_ref,
                       y_ref, acc_scr, st_scr, h_scr, *, tb, rows):
    j = pl.program_id(1)

    @pl.when(j == 0)
    def _():
        acc_scr[...] = jnp.zeros_like(acc_scr)

    tile = (PEER_NKEYS, 128)
    rb = 2
    for b in range(rows // rb):
        bs = slice(b * rb * PEER_NKEYS, (b + 1) * rb * PEER_NKEYS)
        st_scr[bs, :] = _dot_nt(u_ref[bs, :], xn_ref[...])
        for r in range(b * rb, (b + 1) * rb):
            es = slice(r * PEER_NKEYS, (r + 1) * PEER_NKEYS)
            for lg in range(tb // 128):
                ls = slice(lg * 128, (lg + 1) * 128)
                gate = jnp.zeros(tile, _BF16)
                for h in range(PEER_HEADS):
                    cnt = jnp.broadcast_to(cnt_ref[h, r:r + 1, ls].astype(_BF16), tile)
                    pn = jnp.broadcast_to(p1n_ref[h, r:r + 1, ls].astype(_BF16), tile)
                    sel = jnp.minimum(jnp.maximum(cnt - rank_ref[h, :, ls], 0), pn)
                    gate = gate + sel * p2_ref[h, :, ls]
                act = jax.nn.gelu(st_scr[es, ls]).astype(_BF16)
                h_scr[es, ls] = gate * act
    acc_scr[...] += _dot(vt_ref[...], h_scr[...])

    @pl.when(j == pl.num_programs(1) - 1)
    def _():
        x2 = x1_ref[...] + acc_scr[...].T
        y_ref[...] = _rms(x2, gfin_ref[...])


def _peer_dense(xn, u_bf, vt_bf, cnt, p1n, rank2, p2, x1, g_final, *, tb, ec):
    T, D = x1.shape
    rows = ec // PEER_NKEYS
    assert T % tb == 0 and PEER_NEXP % ec == 0 and rows % 8 == 0
    sel_all = pl.BlockSpec((PEER_HEADS, PEER_NKEYS, tb), lambda i, j: (0, 0, i))
    sel_rows = pl.BlockSpec((PEER_HEADS, rows, tb), lambda i, j: (0, j, i))
    tok = pl.BlockSpec((tb, D), lambda i, j: (i, 0))
    return pl.pallas_call(
        functools.partial(_peer_dense_kernel, tb=tb, rows=rows),
        out_shape=jax.ShapeDtypeStruct((T, D), _F32),
        grid=(T // tb, PEER_NEXP // ec),
        in_specs=[
            tok,
            pl.BlockSpec((ec, D), lambda i, j: (j, 0)),
            pl.BlockSpec((D, ec), lambda i, j: (0, j)),
            sel_rows, sel_rows, sel_all, sel_all,
            tok,
            pl.BlockSpec((1, D), lambda i, j: (0, 0)),
        ],
        out_specs=tok,
        scratch_shapes=[
            pltpu.VMEM((D, tb), _F32),
            pltpu.VMEM((ec, tb), _F32),
            pltpu.VMEM((ec, tb), _BF16),
        ],
        compiler_params=pltpu.CompilerParams(
            dimension_semantics=("arbitrary", "arbitrary"), vmem_limit_bytes=_VMEM_LIMIT),
        name="peer_dense",
    )(xn, u_bf, vt_bf, cnt, p1n, rank2, p2, x1, g_final.reshape(1, D))


def _mixer_weights(g_mix, w_in, b_in, g_mlh, g_sgu, w_s, b_s, w_out):
    nq = ML_HEADS * ML_DQK
    nv = ML_HEADS * ML_DV
    o_q, o_k, o_v = 0, nq, 2 * nq
    o_ig = o_v + nv
    o_fg = o_ig + ML_HEADS
    o_og = o_fg + ML_HEADS
    o_su = o_og + nv
    o_sv = o_su + D_MODEL
    o_xq = o_sv + D_MODEL
    o_gt = o_xq + D_MODEL
    main = lambda a: jnp.concatenate([a[..., o_q:o_ig], a[..., o_og:o_gt + 3 * D_MODEL]], axis=-1)
    w_if = w_in[:, o_ig:o_og]
    b_if = b_in[o_ig:o_og]
    return (
        g_mix.reshape(1, D_MODEL),
        main(w_in).astype(_BF16),
        main(b_in).reshape(1, _P_COLS),
        w_if.astype(_BF16),
        w_if.T.astype(_BF16),
        b_if.reshape(1, 8),
        b_if.reshape(8, 1),
        g_mlh.reshape(1, D_MODEL),
        g_sgu.reshape(1, D_MODEL),
        w_s,
        b_s.T,
        w_out.astype(_BF16),
    )


def kernel(x_prompt, x_sample, mem_prompt, cache_mem_k, cache_mem_v, state_mlstm_C, state_mlstm_n,
           state_mlstm_m, g_mix, w_in, b_in, g_mlh, g_sgu, w_s, b_s, g_mem, w_mk, w_mv, w_out, g_ffn,
           w_pq, k_sub1, k_sub2, u_exp, v_exp, g_final):
    depth = g_mix.shape[0]
    assert depth == 1
    l = 0
    Bp, Sp, D = x_prompt.shape
    Bs, Ss, _ = x_sample.shape

    wts = _mixer_weights(g_mix[l], w_in[l], b_in[l], g_mlh[l], g_sgu[l], w_s[l], b_s[l], w_out[l])

    mk, mv = _mem_kv(mem_prompt, g_mem[l], w_mk[l], w_mv[l])
    zc = jnp.zeros((Bp, ML_HEADS, ML_DQK, ML_DV), _F32)
    zn = jnp.zeros((Bp, ML_HEADS, ML_DQK), _F32)
    zm = jnp.zeros((Bp, ML_HEADS), _F32)
    x1p, cp, np_, mp, _ = _mixer(x_prompt, mk, mv, zc, zn, zm, wts, tc=256, want_vn=False)
    x1s, cs, ns, ms, vn = _mixer(
        x_sample, cache_mem_k[l].reshape(Bs, N_MEM, D), cache_mem_v[l].reshape(Bs, N_MEM, D),
        state_mlstm_C[l], state_mlstm_n[l], state_mlstm_m[l], wts, tc=Ss, want_vn=True)

    tp, ts = Bp * Sp, Bs * Ss
    x1 = jnp.concatenate([x1p.reshape(tp, D), x1s.reshape(ts, D)], axis=0)
    wpq_h = w_pq[l].reshape(D, PEER_HEADS, PEER_DQ).transpose(1, 0, 2).astype(_BF16)
    xn, cnt, p1n, rank2, p2 = _peer_select(x1, g_ffn[l], wpq_h, k_sub1[l], k_sub2[l], tb=256)
    y = _peer_dense(xn, u_exp[l].astype(_BF16), v_exp[l].T.astype(_BF16), cnt, p1n, rank2, p2,
                    x1, g_final, tb=512, ec=1024)

    y_prompt = y[:tp].reshape(Bp, Sp, D)
    y_sample = y[tp:].reshape(Bs, Ss, D)
    hs = (XA_HEADS, XA_DH)
    return (y_prompt, y_sample, cp[None], np_[None], mp[None],
            mk.reshape(1, Bp, N_MEM, *hs), mv.reshape(1, Bp, N_MEM, *hs),
            cs[None], ns[None], ms[None], vn[None])
```

```python
import functools

import jax
import jax.numpy as jnp
from jax import lax
from jax.experimental import pallas as pl
from jax.experimental.pallas import tpu as pltpu

D_MODEL = 1024
EPS = 1e-6
CHUNK = 64
N_MEM = 256
ML_HEADS = 4
ML_DQK = 128
ML_DV = D_MODEL // ML_HEADS
SGU_CHUNK = 128
SGU_GROUPS = 4
SGU_GDIM = D_MODEL // SGU_GROUPS
XA_HEADS = 4
XA_DH = D_MODEL // XA_HEADS
PEER_HEADS = 8
PEER_NKEYS = 128
PEER_DQ = 256
PEER_TOPK = 16
PEER_NEXP = PEER_NKEYS * PEER_NKEYS

_Q0 = 0
_K0 = _Q0 + ML_HEADS * ML_DQK
_V0 = _K0 + ML_HEADS * ML_DQK
_OG0 = _V0 + ML_HEADS * ML_DV
_SU0 = _OG0 + D_MODEL
_SV0 = _SU0 + D_MODEL
_XQ0 = _SV0 + D_MODEL
_GT0 = _XQ0 + D_MODEL
_P_COLS = _GT0 + 3 * D_MODEL

_VMEM_LIMIT = 56 * 1024 * 1024

_BF16 = jnp.bfloat16
_F32 = jnp.float32
_NEG_INF = float("-inf")


def _rms(xf, g):
    return xf * lax.rsqrt(jnp.mean(xf * xf, axis=-1, keepdims=True) + EPS) * g


def _dot(a, b):
    return jnp.dot(a, b, preferred_element_type=_F32)


def _dot_nt(a, b):
    return lax.dot_general(a, b, (((1,), (1,)), ((), ())), preferred_element_type=_F32)


def _dot_tn(a, b):
    return lax.dot_general(a, b, (((0,), (0,)), ((), ())), preferred_element_type=_F32)


def _mem_kv_kernel(mem_ref, g_ref, wk_ref, wv_ref, k_ref, v_ref):
    mn = _rms(mem_ref[0], g_ref[...]).astype(_BF16)
    k_ref[0] = _dot(mn, wk_ref[...])
    v_ref[0] = _dot(mn, wv_ref[...])


def _mem_kv(mem, g_mem, w_mk, w_mv):
    B = mem.shape[0]
    full = lambda b: (0, 0)
    return pl.pallas_call(
        _mem_kv_kernel,
        out_shape=(jax.ShapeDtypeStruct((B, N_MEM, D_MODEL), _F32),) * 2,
        grid=(B,),
        in_specs=[
            pl.BlockSpec((1, N_MEM, D_MODEL), lambda b: (b, 0, 0)),
            pl.BlockSpec((1, D_MODEL), full),
            pl.BlockSpec((D_MODEL, D_MODEL), full),
            pl.BlockSpec((D_MODEL, D_MODEL), full),
        ],
        out_specs=(pl.BlockSpec((1, N_MEM, D_MODEL), lambda b: (b, 0, 0)),) * 2,
        compiler_params=pltpu.CompilerParams(
            dimension_semantics=("arbitrary",), vmem_limit_bytes=_VMEM_LIMIT),
        name="mem_kv",
    )(mem, g_mem.reshape(1, D_MODEL), w_mk.astype(_BF16), w_mv.astype(_BF16))


def _mixer_kernel(x_ref, mk_ref, mv_ref, c0_ref, n0_ref, m0_ref, gmix_ref, win_ref, bin_ref,
                  wif_ref, wift_ref, bif_ref, bift_ref, gmlh_ref, gsgu_ref, ws_ref, bst_ref,
                  wout_ref,
                  x1_ref, c_out_ref, n_out_ref, m_out_ref, vn_ref,
                  p_scr, mrg_scr, c_scr, n_scr, m_scr, *, tc, sgu_len):
    ci = pl.program_id(1)
    nsub = tc // CHUNK

    @pl.when(ci == 0)
    def _():
        c_scr[...] = c0_ref[0]
        n_scr[...] = n0_ref[0]
        m_scr[...] = m0_ref[0]

    x = x_ref[0]
    xn = _rms(x, gmix_ref[...]).astype(_BF16)
    p_scr[...] = _dot(xn, win_ref[...]) + bin_ref[...]
    gif = _dot(xn, wif_ref[...]) + bif_ref[...]
    gift = _dot_nt(wift_ref[...], xn) + bift_ref[...]

    row = lax.broadcasted_iota(jnp.int32, (CHUNK, CHUNK), 0)
    col = lax.broadcasted_iota(jnp.int32, (CHUNK, CHUNK), 1)
    tril = row >= col
    tril_f = tril.astype(_F32)
    triu_f = (row <= col).astype(_F32)
    for j in range(nsub):
        r0 = j * CHUNK
        ig_c = gif[r0:r0 + CHUNK, 0:ML_HEADS]
        lf_c = jax.nn.log_sigmoid(gif[r0:r0 + CHUNK, ML_HEADS:2 * ML_HEADS])
        ig_r = gift[0:ML_HEADS, r0:r0 + CHUNK]
        lf_r = jax.nn.log_sigmoid(gift[ML_HEADS:2 * ML_HEADS, r0:r0 + CHUNK])
        b_c = jnp.dot(tril_f, lf_c, precision=lax.Precision.HIGHEST,
                      preferred_element_type=_F32)
        b_r = jnp.dot(lf_r, triu_f, precision=lax.Precision.HIGHEST,
                      preferred_element_type=_F32)
        for h in range(ML_HEADS):
            q = p_scr[r0:r0 + CHUNK, _Q0 + h * ML_DQK:_Q0 + (h + 1) * ML_DQK].astype(_BF16)
            k = p_scr[r0:r0 + CHUNK, _K0 + h * ML_DQK:_K0 + (h + 1) * ML_DQK] * (ML_DQK ** -0.5)
            v = p_scr[r0:r0 + CHUNK, _V0 + h * ML_DV:_V0 + (h + 1) * ML_DV].astype(_BF16)
            c_st = c_scr[h]
            n_st = n_scr[h:h + 1, :]
            m_st = m_scr[h:h + 1, 0:1]
            bc = b_c[:, h:h + 1]
            br = b_r[h:h + 1, :]
            logd = jnp.where(tril, bc - br + ig_r[h:h + 1, :], _NEG_INF)
            log_prev = bc + m_st
            m_t = jnp.maximum(log_prev, jnp.max(logd, axis=1, keepdims=True))
            a = _dot_nt(q, k.astype(_BF16)) * jnp.exp(logd - m_t)
            wp = jnp.exp(log_prev - m_t)
            num = _dot(a.astype(_BF16), v) + wp * _dot(q, c_st.astype(_BF16))
            qn = jnp.sum(q.astype(_F32) * n_st, axis=1, keepdims=True)
            den = jnp.sum(a, axis=1, keepdims=True) + wp * qn
            hh = num / jnp.maximum(jnp.abs(den), jnp.exp(-m_t))
            hn = hh * lax.rsqrt(jnp.mean(hh * hh, axis=1, keepdims=True) + EPS)
            sl = slice(h * ML_DV, (h + 1) * ML_DV)
            og = p_scr[r0:r0 + CHUNK, _OG0 + h * ML_DV:_OG0 + (h + 1) * ML_DV]
            h_a = (jax.nn.sigmoid(og) * hn * gmlh_ref[:, sl]).astype(_F32)
            g0 = jax.nn.sigmoid(p_scr[r0:r0 + CHUNK, _GT0 + h * ML_DV:_GT0 + (h + 1) * ML_DV])
            mrg_scr[r0:r0 + CHUNK, sl] = g0 * h_a
            b_last = bc[CHUNK - 1:CHUNK, :]
            log_in = b_last - bc + ig_c[:, h:h + 1]
            m_new = jnp.maximum(b_last + m_st, jnp.max(log_in, axis=0, keepdims=True))
            wi = jnp.exp(log_in - m_new)
            wc = jnp.exp(b_last + m_st - m_new)
            kw = wi * k
            c_scr[h] = wc * c_st + _dot_tn(kw.astype(_BF16), v)
            n_scr[h:h + 1, :] = wc * n_st + jnp.sum(kw, axis=0, keepdims=True)
            m_scr[h:h + 1, :] = jnp.broadcast_to(m_new, (1, 128))

    sv = jax.nn.gelu(p_scr[:, _SV0:_SV0 + D_MODEL])
    vn = _rms(sv, gsgu_ref[...])
    if vn_ref is not None:
        vn_ref[0] = vn
    vnb = vn.astype(_BF16)
    rs = lax.broadcasted_iota(jnp.int32, (sgu_len, sgu_len), 0)
    cs = lax.broadcasted_iota(jnp.int32, (sgu_len, sgu_len), 1)
    for g in range(SGU_GROUPS):
        wsg = jnp.where(rs >= cs, ws_ref[g, 0:sgu_len, 0:sgu_len], 0.0).astype(_BF16)
        bsg = bst_ref[0:sgu_len, g:g + 1]
        gl = slice(g * SGU_GDIM, (g + 1) * SGU_GDIM)
        for c in range(tc // sgu_len):
            rows = slice(c * sgu_len, (c + 1) * sgu_len)
            mix = _dot(wsg, vnb[rows, gl]) + bsg
            u = jax.nn.gelu(p_scr[rows, _SU0 + g * SGU_GDIM:_SU0 + (g + 1) * SGU_GDIM])
            g1 = jax.nn.sigmoid(
                p_scr[rows, _GT0 + D_MODEL + g * SGU_GDIM:_GT0 + D_MODEL + (g + 1) * SGU_GDIM])
            mrg_scr[rows, gl] += g1 * (u * mix)

    for h in range(XA_HEADS):
        hl = slice(h * XA_DH, (h + 1) * XA_DH)
        xq = p_scr[:, _XQ0 + h * XA_DH:_XQ0 + (h + 1) * XA_DH].astype(_BF16)
        sc = _dot_nt(xq, mk_ref[0, :, hl].astype(_BF16)) * (XA_DH ** -0.5)
        sc = sc - jnp.max(sc, axis=1, keepdims=True)
        e = jnp.exp(sc)
        att = e / jnp.sum(e, axis=1, keepdims=True)
        h_c = _dot(att.astype(_BF16), mv_ref[0, :, hl].astype(_BF16))
        g2 = jax.nn.sigmoid(p_scr[:, _GT0 + 2 * D_MODEL + h * XA_DH:_GT0 + 2 * D_MODEL + (h + 1) * XA_DH])
        mrg_scr[:, hl] += g2 * h_c

    x1_ref[0] = x + _dot(mrg_scr[...].astype(_BF16), wout_ref[...])

    @pl.when(ci == pl.num_programs(1) - 1)
    def _():
        c_out_ref[0] = c_scr[...]
        n_out_ref[0] = n_scr[...]
        m_out_ref[0] = m_scr[...]


def _mixer(x, mem_k, mem_v, c0, n0, m0, wts, *, tc, want_vn):
    B, S, D = x.shape
    sgu_len = min(S, SGU_CHUNK)
    assert S % tc == 0 and tc % CHUNK == 0 and tc % sgu_len == 0
    nchunks = S // tc
    m0p = jnp.broadcast_to(m0[:, :, None], (B, ML_HEADS, 128))
    m0p = jnp.concatenate([m0p, jnp.zeros((B, 8 - ML_HEADS, 128), _F32)], axis=1)

    def body(*refs):
        ins, rest = refs[:18], refs[18:]
        if want_vn:
            outs, scr = rest[:5], rest[5:]
        else:
            outs, scr = rest[:4] + (None,), rest[4:]
        _mixer_kernel(*ins, *outs, *scr, tc=tc, sgu_len=sgu_len)

    const2 = lambda b, c: (0, 0)
    const3 = lambda b, c: (0, 0, 0)
    per_b3 = lambda b, c: (b, 0, 0)
    per_b4 = lambda b, c: (b, 0, 0, 0)
    once = dict(pipeline_mode=pl.Buffered(1))
    in_specs = [
        pl.BlockSpec((1, tc, D), lambda b, c: (b, c, 0)),
        pl.BlockSpec((1, N_MEM, D), per_b3),
        pl.BlockSpec((1, N_MEM, D), per_b3),
        pl.BlockSpec((1, ML_HEADS, ML_DQK, ML_DV), per_b4),
        pl.BlockSpec((1, ML_HEADS, ML_DQK), per_b3),
        pl.BlockSpec((1, 8, 128), per_b3),
        pl.BlockSpec((1, D), const2, **once),
        pl.BlockSpec((D, _P_COLS), const2, **once),
        pl.BlockSpec((1, _P_COLS), const2, **once),
        pl.BlockSpec((D, 8), const2, **once),
        pl.BlockSpec((8, D), const2, **once),
        pl.BlockSpec((1, 8), const2, **once),
        pl.BlockSpec((8, 1), const2, **once),
        pl.BlockSpec((1, D), const2, **once),
        pl.BlockSpec((1, D), const2, **once),
        pl.BlockSpec((SGU_GROUPS, SGU_CHUNK, SGU_CHUNK), const3, **once),
        pl.BlockSpec((SGU_CHUNK, SGU_GROUPS), const2, **once),
        pl.BlockSpec((D, D), const2, **once),
    ]
    out_shape = [
        jax.ShapeDtypeStruct((B, S, D), _F32),
        jax.ShapeDtypeStruct((B, ML_HEADS, ML_DQK, ML_DV), _F32),
        jax.ShapeDtypeStruct((B, ML_HEADS, ML_DQK), _F32),
        jax.ShapeDtypeStruct((B, 8, 128), _F32),
    ]
    out_specs = [
        pl.BlockSpec((1, tc, D), lambda b, c: (b, c, 0)),
        pl.BlockSpec((1, ML_HEADS, ML_DQK, ML_DV), per_b4),
        pl.BlockSpec((1, ML_HEADS, ML_DQK), per_b3),
        pl.BlockSpec((1, 8, 128), per_b3),
    ]
    if want_vn:
        out_shape.append(jax.ShapeDtypeStruct((B, S, D), _F32))
        out_specs.append(pl.BlockSpec((1, tc, D), lambda b, c: (b, c, 0)))
    outs = pl.pallas_call(
        body,
        out_shape=tuple(out_shape),
        grid=(B, nchunks),
        in_specs=in_specs,
        out_specs=tuple(out_specs),
        scratch_shapes=[
            pltpu.VMEM((tc, _P_COLS), _F32),
            pltpu.VMEM((tc, D), _F32),
            pltpu.VMEM((ML_HEADS, ML_DQK, ML_DV), _F32),
            pltpu.VMEM((ML_HEADS, ML_DQK), _F32),
            pltpu.VMEM((8, 128), _F32),
        ],
        compiler_params=pltpu.CompilerParams(
            dimension_semantics=("arbitrary", "arbitrary"), vmem_limit_bytes=_VMEM_LIMIT),
        name="mixer",
    )(x, mem_k, mem_v, c0, n0, m0p, *wts)
    x1, c1, n1, m1p = outs[:4]
    vn = outs[4] if want_vn else None
    return x1, c1, n1, m1p[:, :ML_HEADS, 0], vn


def _batcher_pairs(n):
    pairs = []
    p = 1
    while p < n:
        k = p
        while k >= 1:
            for j in range(k % p, n - k, 2 * k):
                for i in range(min(k, n - j - k)):
                    if (i + j) // (2 * p) == (i + j + k) // (2 * p):
                        pairs.append((i + j, i + j + k))
            k //= 2
        p *= 2
    return pairs


_SORT16 = _batcher_pairs(PEER_NKEYS // 8)
_SEL_UNROLL = 2


def _sorted_columns(c):
    cols = [c[8 * k:8 * k + 8, :] for k in range(PEER_NKEYS // 8)]
    for i, j in _SORT16:
        cols[i], cols[j] = jnp.maximum(cols[i], cols[j]), jnp.minimum(cols[i], cols[j])
    return cols


def _pop_top(cols, out_scr):
    for r in range(PEER_TOPK):
        mx = jnp.max(cols[0], axis=0, keepdims=True)
        out_scr[r:r + 1, :] = mx
        left = PEER_TOPK - 1 - r
        if left:
            eq = cols[0] == mx
            for k in range(left):
                cols[k] = jnp.where(eq, cols[k + 1], cols[k])


def _peer_select_kernel(x1_ref, gffn_ref, wpq_ref, k1_ref, k2_ref,
                        xn_ref, cnt_ref, p1n_ref, rank_ref, p2_ref,
                        q_scr, c1_scr, c2_scr, v1_scr, v2_scr, n_scr, *, tb):
    xn = _rms(x1_ref[...], gffn_ref[...]).astype(_BF16)
    xn_ref[...] = xn
    q_scr[...] = _dot(xn, wpq_ref[...])
    k1 = k1_ref[...].astype(_BF16)
    k2 = k2_ref[...].astype(_BF16)
    half = PEER_DQ // 2
    K = PEER_TOPK

    def lane_group(h, ls):
        c1 = c1_scr[:, ls]
        c2 = c2_scr[:, ls]
        _pop_top(_sorted_columns(c1), v1_scr)
        _pop_top(_sorted_columns(c2), v2_scr)
        v1 = v1_scr[...]
        lists = [v1_scr[0:8, :] + v2_scr[b:b + 1, :] for b in range(8)]
        ea = v1_scr[8:16, :] + v2_scr[0:1, :]
        eb = v1_scr[0:1, :] + v2_scr[8:16, :]
        cands = lists + [ea, eb]
        tau = None
        for r in range(K):
            tau = jnp.max(jnp.maximum(jnp.maximum(lists[0], ea), eb), axis=0, keepdims=True)
            left = K - 1 - r
            if left:
                eq = lists[0] == tau
                for k in range(min(left, 7)):
                    lists[k] = jnp.where(eq, lists[k + 1], lists[k])
                if left >= 8:
                    lists[7] = jnp.where(eq, _NEG_INF, lists[7])
                ea = jnp.where(ea == tau, _NEG_INF, ea)
                eb = jnp.where(eb == tau, _NEG_INF, eb)
        z = sum(jnp.sum(jnp.where(c >= tau, jnp.exp(c), 0.0), axis=0, keepdims=True) for c in cands)
        n = jnp.zeros(v1.shape, _F32)
        for b in range(K):
            n = jnp.where(v1 + v2_scr[b:b + 1, :] >= tau, float(b + 1), n)
        n_scr[...] = n
        cnt = jnp.zeros(c1.shape, _F32)
        rank2 = jnp.full(c2.shape, float(K), _F32)
        for a in range(K - 1, -1, -1):
            cnt = jnp.where(c1 >= v1_scr[a:a + 1, :], n_scr[a:a + 1, :], cnt)
            rank2 = jnp.where(c2 >= v2_scr[a:a + 1, :], float(a), rank2)
        cnt_ref[h, :, ls] = cnt
        p1n_ref[h, :, ls] = jnp.exp(c1) / z
        rank_ref[h, :, ls] = rank2.astype(_BF16)
        p2_ref[h, :, ls] = jnp.exp(c2).astype(_BF16)

    def head(h, carry):
        q0 = pl.multiple_of(h * PEER_DQ, PEER_DQ)
        s1 = _dot_nt(k1, q_scr[:, pl.ds(q0, half)].astype(_BF16))
        s2 = _dot_nt(k2, q_scr[:, pl.ds(q0 + half, half)].astype(_BF16))
        c1_scr[...] = s1 - jnp.max(s1, axis=0, keepdims=True)
        c2_scr[...] = s2 - jnp.max(s2, axis=0, keepdims=True)

        def lane_groups(i, carry):
            for u in range(_SEL_UNROLL):
                lane_group(h, pl.ds(pl.multiple_of((i * _SEL_UNROLL + u) * 128, 128), 128))
            return carry

        lax.fori_loop(0, tb // (128 * _SEL_UNROLL), lane_groups, 0)
        return carry

    lax.fori_loop(0, PEER_HEADS, head, 0)


def _peer_select(x1, g_ffn, wpq_bf, k_sub1, k_sub2, *, tb):
    T, D = x1.shape
    assert T % tb == 0 and tb % (128 * _SEL_UNROLL) == 0
    sel = lambda dt: jax.ShapeDtypeStruct((PEER_HEADS, PEER_NKEYS, T), dt)
    sel_spec = pl.BlockSpec((PEER_HEADS, PEER_NKEYS, tb), lambda i: (0, 0, i))
    const2 = lambda i: (0, 0)
    return pl.pallas_call(
        functools.partial(_peer_select_kernel, tb=tb),
        out_shape=(jax.ShapeDtypeStruct((T, D), _BF16), sel(_F32), sel(_F32), sel(_BF16), sel(_BF16)),
        grid=(T // tb,),
        in_specs=[
            pl.BlockSpec((tb, D), lambda i: (i, 0)),
            pl.BlockSpec((1, D), const2),
            pl.BlockSpec((D, PEER_HEADS * PEER_DQ), const2),
            pl.BlockSpec((PEER_NKEYS, PEER_DQ // 2), const2),
            pl.BlockSpec((PEER_NKEYS, PEER_DQ // 2), const2),
        ],
        out_specs=(pl.BlockSpec((tb, D), lambda i: (i, 0)),) + (sel_spec,) * 4,
        scratch_shapes=[
            pltpu.VMEM((tb, PEER_HEADS * PEER_DQ), _F32),
            pltpu.VMEM((PEER_NKEYS, tb), _F32),
            pltpu.VMEM((PEER_NKEYS, tb), _F32),
            pltpu.VMEM((PEER_TOPK, 128), _F32),
            pltpu.VMEM((PEER_TOPK, 128), _F32),
            pltpu.VMEM((PEER_TOPK, 128), _F32),
        ],
        compiler_params=pltpu.CompilerParams(
            dimension_semantics=("arbitrary",), vmem_limit_bytes=_VMEM_LIMIT),
        name="peer_select",
    )(x1, g_ffn.reshape(1, D), wpq_bf, k_sub1, k_sub2)


def _peer_dense_kernel(xn_ref, u_ref, vt_ref, cnt_ref, p1n_ref, rank_ref, p2_ref, x1_ref, gfin_ref,
                       y_ref, acc_scr, st_scr, h_scr, *, tb, rows):
    j = pl.program_id(1)

    @pl.when(j == 0)
    def _():
        acc_scr[...] = jnp.zeros_like(acc_scr)

    tile = (PEER_NKEYS, 128)
    rb = 2
    for b in range(rows // rb):
        bs = slice(b * rb * PEER_NKEYS, (b + 1) * rb * PEER_NKEYS)
        st_scr[bs, :] = _dot_nt(u_ref[bs, :], xn_ref[...])
        for r in range(b * rb, (b + 1) * rb):
            es = slice(r * PEER_NKEYS, (r + 1) * PEER_NKEYS)
            for lg in range(tb // 128):
                ls = slice(lg * 128, (lg + 1) * 128)
                gate = jnp.zeros(tile, _BF16)
                for h in range(PEER_HEADS):
                    cnt = jnp.broadcast_to(cnt_ref[h, r:r + 1, ls].astype(_BF16), tile)
                    pn = jnp.broadcast_to(p1n_ref[h, r:r + 1, ls].astype(_BF16), tile)
                    sel = jnp.minimum(jnp.maximum(cnt - rank_ref[h, :, ls], 0), pn)
                    gate = gate + sel * p2_ref[h, :, ls]
                act = jax.nn.gelu(st_scr[es, ls]).astype(_BF16)
                h_scr[es, ls] = gate * act
    acc_scr[...] += _dot(vt_ref[...], h_scr[...])

    @pl.when(j == pl.num_programs(1) - 1)
    def _():
        x2 = x1_ref[...] + acc_scr[...].T
        y_ref[...] = _rms(x2, gfin_ref[...])


def _peer_dense(xn, u_bf, vt_bf, cnt, p1n, rank2, p2, x1, g_final, *, tb, ec):
    T, D = x1.shape
    rows = ec // PEER_NKEYS
    assert T % tb == 0 and PEER_NEXP % ec == 0 and rows % 8 == 0
    sel_all = pl.BlockSpec((PEER_HEADS, PEER_NKEYS, tb), lambda i, j: (0, 0, i))
    sel_rows = pl.BlockSpec((PEER_HEADS, rows, tb), lambda i, j: (0, j, i))
    tok = pl.BlockSpec((tb, D), lambda i, j: (i, 0))
    return pl.pallas_call(
        functools.partial(_peer_dense_kernel, tb=tb, rows=rows),
        out_shape=jax.ShapeDtypeStruct((T, D), _F32),
        grid=(T // tb, PEER_NEXP // ec),
        in_specs=[
            tok,
            pl.BlockSpec((ec, D), lambda i, j: (j, 0)),
            pl.BlockSpec((D, ec), lambda i, j: (0, j)),
            sel_rows, sel_rows, sel_all, sel_all,
            tok,
            pl.BlockSpec((1, D), lambda i, j: (0, 0)),
        ],
        out_specs=tok,
        scratch_shapes=[
            pltpu.VMEM((D, tb), _F32),
            pltpu.VMEM((ec, tb), _F32),
            pltpu.VMEM((ec, tb), _BF16),
        ],
        compiler_params=pltpu.CompilerParams(
            dimension_semantics=("arbitrary", "arbitrary"), vmem_limit_bytes=_VMEM_LIMIT),
        name="peer_dense",
    )(xn, u_bf, vt_bf, cnt, p1n, rank2, p2, x1, g_final.reshape(1, D))


def _mixer_weights(g_mix, w_in, b_in, g_mlh, g_sgu, w_s, b_s, w_out):
    nq = ML_HEADS * ML_DQK
    nv = ML_HEADS * ML_DV
    o_q, o_k, o_v = 0, nq, 2 * nq
    o_ig = o_v + nv
    o_fg = o_ig + ML_HEADS
    o_og = o_fg + ML_HEADS
    o_su = o_og + nv
    o_sv = o_su + D_MODEL
    o_xq = o_sv + D_MODEL
    o_gt = o_xq + D_MODEL
    main = lambda a: jnp.concatenate([a[..., o_q:o_ig], a[..., o_og:o_gt + 3 * D_MODEL]], axis=-1)
    w_if = w_in[:, o_ig:o_og]
    b_if = b_in[o_ig:o_og]
    return (
        g_mix.reshape(1, D_MODEL),
        main(w_in).astype(_BF16),
        main(b_in).reshape(1, _P_COLS),
        w_if.astype(_BF16),
        w_if.T.astype(_BF16),
        b_if.reshape(1, 8),
        b_if.reshape(8, 1),
        g_mlh.reshape(1, D_MODEL),
        g_sgu.reshape(1, D_MODEL),
        w_s,
        b_s.T,
        w_out.astype(_BF16),
    )


def kernel(x_prompt, x_sample, mem_prompt, cache_mem_k, cache_mem_v, state_mlstm_C, state_mlstm_n,
           state_mlstm_m, g_mix, w_in, b_in, g_mlh, g_sgu, w_s, b_s, g_mem, w_mk, w_mv, w_out, g_ffn,
           w_pq, k_sub1, k_sub2, u_exp, v_exp, g_final):
    depth = g_mix.shape[0]
    assert depth == 1
    l = 0
    Bp, Sp, D = x_prompt.shape
    Bs, Ss, _ = x_sample.shape

    wts = _mixer_weights(g_mix[l], w_in[l], b_in[l], g_mlh[l], g_sgu[l], w_s[l], b_s[l], w_out[l])

    mk, mv = _mem_kv(mem_prompt, g_mem[l], w_mk[l], w_mv[l])
    zc = jnp.zeros((Bp, ML_HEADS, ML_DQK, ML_DV), _F32)
    zn = jnp.zeros((Bp, ML_HEADS, ML_DQK), _F32)
    zm = jnp.zeros((Bp, ML_HEADS), _F32)
    x1p, cp, np_, mp, _ = _mixer(x_prompt, mk, mv, zc, zn, zm, wts, tc=256, want_vn=False)
    x1s, cs, ns, ms, vn = _mixer(
        x_sample, cache_mem_k[l].reshape(Bs, N_MEM, D), cache_mem_v[l].reshape(Bs, N_MEM, D),
        state_mlstm_C[l], state_mlstm_n[l], state_mlstm_m[l], wts, tc=Ss, want_vn=True)

    wpq_bf = w_pq[l].astype(_BF16)
    u_bf = u_exp[l].astype(_BF16)
    vt_bf = v_exp[l].T.astype(_BF16)

    def peer(x1):
        B, S, _ = x1.shape
        x1 = x1.reshape(B * S, D)
        xn, cnt, p1n, rank2, p2 = _peer_select(x1, g_ffn[l], wpq_bf, k_sub1[l], k_sub2[l], tb=512)
        y = _peer_dense(xn, u_bf, vt_bf, cnt, p1n, rank2, p2, x1, g_final, tb=512, ec=1024)
        return y.reshape(B, S, D)

    hs = (XA_HEADS, XA_DH)
    return (peer(x1p), peer(x1s), cp[None], np_[None], mp[None],
            mk.reshape(1, Bp, N_MEM, *hs), mv.reshape(1, Bp, N_MEM, *hs),
            cs[None], ns[None], ms[None], vn[None])
```

```python
import functools

import jax
import jax.numpy as jnp
from jax import lax
from jax.experimental import pallas as pl
from jax.experimental.pallas import tpu as pltpu

D_MODEL = 1024
EPS = 1e-6
CHUNK = 64
N_MEM = 256
ML_HEADS = 4
ML_DQK = 128
ML_DV = D_MODEL // ML_HEADS
SGU_CHUNK = 128
SGU_GROUPS = 4
SGU_GDIM = D_MODEL // SGU_GROUPS
XA_HEADS = 4
XA_DH = D_MODEL // XA_HEADS
PEER_HEADS = 8
PEER_NKEYS = 128
PEER_DQ = 256
PEER_TOPK = 16
PEER_NEXP = PEER_NKEYS * PEER_NKEYS

_Q0 = 0
_K0 = _Q0 + ML_HEADS * ML_DQK
_V0 = _K0 + ML_HEADS * ML_DQK
_OG0 = _V0 + ML_HEADS * ML_DV
_SU0 = _OG0 + D_MODEL
_SV0 = _SU0 + D_MODEL
_XQ0 = _SV0 + D_MODEL
_GT0 = _XQ0 + D_MODEL
_P_COLS = _GT0 + 3 * D_MODEL

_VMEM_LIMIT = 56 * 1024 * 1024

_PEER_TB = 512
_PEER_EC = 1024

_BF16 = jnp.bfloat16
_F32 = jnp.float32
_NEG_INF = float("-inf")


def _rms(xf, g):
    return xf * lax.rsqrt(jnp.mean(xf * xf, axis=-1, keepdims=True) + EPS) * g


def _dot(a, b):
    return jnp.dot(a, b, preferred_element_type=_F32)


def _dot_nt(a, b):
    return lax.dot_general(a, b, (((1,), (1,)), ((), ())), preferred_element_type=_F32)


def _bmm(a, b, ca, cb):
    return lax.dot_general(a, b, (((ca,), (cb,)), ((0,), (0,))), preferred_element_type=_F32)


def _mem_kv_kernel(mem_ref, g_ref, wk_ref, wv_ref, k_ref, v_ref):
    mn = _rms(mem_ref[0], g_ref[...]).astype(_BF16)
    k_ref[0] = _dot(mn, wk_ref[...])
    v_ref[0] = _dot(mn, wv_ref[...])


def _mem_kv(mem, g_mem, w_mk, w_mv):
    B = mem.shape[0]
    full = lambda b: (0, 0)
    return pl.pallas_call(
        _mem_kv_kernel,
        out_shape=(jax.ShapeDtypeStruct((B, N_MEM, D_MODEL), _F32),) * 2,
        grid=(B,),
        in_specs=[
            pl.BlockSpec((1, N_MEM, D_MODEL), lambda b: (b, 0, 0)),
            pl.BlockSpec((1, D_MODEL), full),
            pl.BlockSpec((D_MODEL, D_MODEL), full),
            pl.BlockSpec((D_MODEL, D_MODEL), full),
        ],
        out_specs=(pl.BlockSpec((1, N_MEM, D_MODEL), lambda b: (b, 0, 0)),) * 2,
        compiler_params=pltpu.CompilerParams(
            dimension_semantics=("arbitrary",), vmem_limit_bytes=_VMEM_LIMIT),
        name="mem_kv",
    )(mem, g_mem.reshape(1, D_MODEL), w_mk.astype(_BF16), w_mv.astype(_BF16))


def _mixer_kernel(x_ref, mk_ref, mv_ref, c0_ref, n0_ref, m0_ref, gmix_ref, win_ref, bin_ref,
                  wif_ref, wift_ref, bif_ref, bift_ref, gmlh_ref, gsgu_ref, ws_ref, bst_ref,
                  wout_ref,
                  x1_ref, c_out_ref, n_out_ref, m_out_ref, vn_ref,
                  p_scr, mrg_scr, c_scr, n_scr, m_scr, *, tc, sgu_len):
    ci = pl.program_id(1)
    nsub = tc // CHUNK

    @pl.when(ci == 0)
    def _():
        c_scr[...] = c0_ref[0]
        n_scr[...] = n0_ref[0]
        m_scr[...] = m0_ref[0]

    x = x_ref[0]
    xn = _rms(x, gmix_ref[...]).astype(_BF16)
    p_scr[...] = _dot(xn, win_ref[...]) + bin_ref[...]
    gif = _dot(xn, wif_ref[...]) + bif_ref[...]
    gift = _dot_nt(wift_ref[...], xn) + bift_ref[...]

    row = lax.broadcasted_iota(jnp.int32, (CHUNK, CHUNK), 0)
    col = lax.broadcasted_iota(jnp.int32, (CHUNK, CHUNK), 1)
    tril = row >= col
    tril_f = tril.astype(_F32)
    triu_f = (row <= col).astype(_F32)
    stack = lambda f: jnp.stack([f(h) for h in range(ML_HEADS)])
    for j in range(nsub):
        r0 = j * CHUNK
        rows = slice(r0, r0 + CHUNK)
        ig_c = gif[rows, 0:ML_HEADS]
        lf_c = jax.nn.log_sigmoid(gif[rows, ML_HEADS:2 * ML_HEADS])
        ig_r = gift[0:ML_HEADS, rows]
        lf_r = jax.nn.log_sigmoid(gift[ML_HEADS:2 * ML_HEADS, rows])
        b_c = jnp.dot(tril_f, lf_c, precision=lax.Precision.HIGHEST,
                      preferred_element_type=_F32)
        b_r = jnp.dot(lf_r, triu_f, precision=lax.Precision.HIGHEST,
                      preferred_element_type=_F32)
        q = stack(lambda h: p_scr[rows, _Q0 + h * ML_DQK:_Q0 + (h + 1) * ML_DQK]).astype(_BF16)
        k = stack(lambda h: p_scr[rows, _K0 + h * ML_DQK:_K0 + (h + 1) * ML_DQK]) * (ML_DQK ** -0.5)
        v = stack(lambda h: p_scr[rows, _V0 + h * ML_DV:_V0 + (h + 1) * ML_DV]).astype(_BF16)
        c_st = c_scr[...]
        n_st = stack(lambda h: n_scr[h:h + 1, :])
        m_st = stack(lambda h: m_scr[h:h + 1, 0:1])
        bc = stack(lambda h: b_c[:, h:h + 1])
        br = stack(lambda h: b_r[h:h + 1, :])
        ig_row = stack(lambda h: ig_r[h:h + 1, :])
        ig_col = stack(lambda h: ig_c[:, h:h + 1])
        logd = jnp.where(tril[None], bc - br + ig_row, _NEG_INF)
        log_prev = bc + m_st
        m_t = jnp.maximum(log_prev, jnp.max(logd, axis=2, keepdims=True))
        a = _bmm(q, k.astype(_BF16), 2, 2) * jnp.exp(logd - m_t)
        wp = jnp.exp(log_prev - m_t)
        num = _bmm(a.astype(_BF16), v, 2, 1) + wp * _bmm(q, c_st.astype(_BF16), 2, 1)
        qn = jnp.sum(q.astype(_F32) * n_st, axis=2, keepdims=True)
        den = jnp.sum(a, axis=2, keepdims=True) + wp * qn
        hh = num / jnp.maximum(jnp.abs(den), jnp.exp(-m_t))
        hn = hh * lax.rsqrt(jnp.mean(hh * hh, axis=2, keepdims=True) + EPS)
        for h in range(ML_HEADS):
            sl = slice(h * ML_DV, (h + 1) * ML_DV)
            og = p_scr[rows, _OG0 + h * ML_DV:_OG0 + (h + 1) * ML_DV]
            h_a = jax.nn.sigmoid(og) * hn[h] * gmlh_ref[:, sl]
            g0 = jax.nn.sigmoid(p_scr[rows, _GT0 + h * ML_DV:_GT0 + (h + 1) * ML_DV])
            mrg_scr[rows, sl] = g0 * h_a
        b_last = bc[:, CHUNK - 1:CHUNK, :]
        log_in = b_last - bc + ig_col
        m_new = jnp.maximum(b_last + m_st, jnp.max(log_in, axis=1, keepdims=True))
        wi = jnp.exp(log_in - m_new)
        wc = jnp.exp(b_last + m_st - m_new)
        kw = wi * k
        c_scr[...] = wc * c_st + _bmm(kw.astype(_BF16), v, 1, 1)
        n_new = wc * n_st + jnp.sum(kw, axis=1, keepdims=True)
        for h in range(ML_HEADS):
            n_scr[h:h + 1, :] = n_new[h]
            m_scr[h:h + 1, :] = jnp.broadcast_to(m_new[h], (1, 128))

    sv = jax.nn.gelu(p_scr[:, _SV0:_SV0 + D_MODEL])
    vn = _rms(sv, gsgu_ref[...])
    if vn_ref is not None:
        vn_ref[0] = vn
    vnb = vn.astype(_BF16)
    rs = lax.broadcasted_iota(jnp.int32, (sgu_len, sgu_len), 0)
    cs = lax.broadcasted_iota(jnp.int32, (sgu_len, sgu_len), 1)
    for g in range(SGU_GROUPS):
        wsg = jnp.where(rs >= cs, ws_ref[g, 0:sgu_len, 0:sgu_len], 0.0).astype(_BF16)
        bsg = bst_ref[0:sgu_len, g:g + 1]
        gl = slice(g * SGU_GDIM, (g + 1) * SGU_GDIM)
        for c in range(tc // sgu_len):
            rows = slice(c * sgu_len, (c + 1) * sgu_len)
            mix = _dot(wsg, vnb[rows, gl]) + bsg
            u = jax.nn.gelu(p_scr[rows, _SU0 + g * SGU_GDIM:_SU0 + (g + 1) * SGU_GDIM])
            g1 = jax.nn.sigmoid(
                p_scr[rows, _GT0 + D_MODEL + g * SGU_GDIM:_GT0 + D_MODEL + (g + 1) * SGU_GDIM])
            mrg_scr[rows, gl] += g1 * (u * mix)

    for h in range(XA_HEADS):
        hl = slice(h * XA_DH, (h + 1) * XA_DH)
        xq = p_scr[:, _XQ0 + h * XA_DH:_XQ0 + (h + 1) * XA_DH].astype(_BF16)
        sc = _dot_nt(xq, mk_ref[0, :, hl].astype(_BF16)) * (XA_DH ** -0.5)
        sc = sc - jnp.max(sc, axis=1, keepdims=True)
        e = jnp.exp(sc)
        att = e / jnp.sum(e, axis=1, keepdims=True)
        h_c = _dot(att.astype(_BF16), mv_ref[0, :, hl].astype(_BF16))
        g2 = jax.nn.sigmoid(p_scr[:, _GT0 + 2 * D_MODEL + h * XA_DH:_GT0 + 2 * D_MODEL + (h + 1) * XA_DH])
        mrg_scr[:, hl] += g2 * h_c

    x1_ref[0] = x + _dot(mrg_scr[...].astype(_BF16), wout_ref[...])

    @pl.when(ci == pl.num_programs(1) - 1)
    def _():
        c_out_ref[0] = c_scr[...]
        n_out_ref[0] = n_scr[...]
        m_out_ref[0] = m_scr[...]


def _mixer(x, mem_k, mem_v, c0, n0, m0, wts, *, tc, want_vn):
    B, S, D = x.shape
    sgu_len = min(S, SGU_CHUNK)
    assert S % tc == 0 and tc % CHUNK == 0 and tc % sgu_len == 0
    nchunks = S // tc
    m0p = jnp.broadcast_to(m0[:, :, None], (B, ML_HEADS, 128))
    m0p = jnp.concatenate([m0p, jnp.zeros((B, 8 - ML_HEADS, 128), _F32)], axis=1)

    def body(*refs):
        ins, rest = refs[:18], refs[18:]
        if want_vn:
            outs, scr = rest[:5], rest[5:]
        else:
            outs, scr = rest[:4] + (None,), rest[4:]
        _mixer_kernel(*ins, *outs, *scr, tc=tc, sgu_len=sgu_len)

    const2 = lambda b, c: (0, 0)
    const3 = lambda b, c: (0, 0, 0)
    per_b3 = lambda b, c: (b, 0, 0)
    per_b4 = lambda b, c: (b, 0, 0, 0)
    once = dict(pipeline_mode=pl.Buffered(1))
    in_specs = [
        pl.BlockSpec((1, tc, D), lambda b, c: (b, c, 0)),
        pl.BlockSpec((1, N_MEM, D), per_b3),
        pl.BlockSpec((1, N_MEM, D), per_b3),
        pl.BlockSpec((1, ML_HEADS, ML_DQK, ML_DV), per_b4),
        pl.BlockSpec((1, ML_HEADS, ML_DQK), per_b3),
        pl.BlockSpec((1, 8, 128), per_b3),
        pl.BlockSpec((1, D), const2, **once),
        pl.BlockSpec((D, _P_COLS), const2, **once),
        pl.BlockSpec((1, _P_COLS), const2, **once),
        pl.BlockSpec((D, 8), const2, **once),
        pl.BlockSpec((8, D), const2, **once),
        pl.BlockSpec((1, 8), const2, **once),
        pl.BlockSpec((8, 1), const2, **once),
        pl.BlockSpec((1, D), const2, **once),
        pl.BlockSpec((1, D), const2, **once),
        pl.BlockSpec((SGU_GROUPS, SGU_CHUNK, SGU_CHUNK), const3, **once),
        pl.BlockSpec((SGU_CHUNK, SGU_GROUPS), const2, **once),
        pl.BlockSpec((D, D), const2, **once),
    ]
    out_shape = [
        jax.ShapeDtypeStruct((B, S, D), _F32),
        jax.ShapeDtypeStruct((B, ML_HEADS, ML_DQK, ML_DV), _F32),
        jax.ShapeDtypeStruct((B, ML_HEADS, ML_DQK), _F32),
        jax.ShapeDtypeStruct((B, 8, 128), _F32),
    ]
    out_specs = [
        pl.BlockSpec((1, tc, D), lambda b, c: (b, c, 0)),
        pl.BlockSpec((1, ML_HEADS, ML_DQK, ML_DV), per_b4),
        pl.BlockSpec((1, ML_HEADS, ML_DQK), per_b3),
        pl.BlockSpec((1, 8, 128), per_b3),
    ]
    if want_vn:
        out_shape.append(jax.ShapeDtypeStruct((B, S, D), _F32))
        out_specs.append(pl.BlockSpec((1, tc, D), lambda b, c: (b, c, 0)))
    outs = pl.pallas_call(
        body,
        out_shape=tuple(out_shape),
        grid=(B, nchunks),
        in_specs=in_specs,
        out_specs=tuple(out_specs),
        scratch_shapes=[
            pltpu.VMEM((tc, _P_COLS), _F32),
            pltpu.VMEM((tc, D), _F32),
            pltpu.VMEM((ML_HEADS, ML_DQK, ML_DV), _F32),
            pltpu.VMEM((ML_HEADS, ML_DQK), _F32),
            pltpu.VMEM((8, 128), _F32),
        ],
        compiler_params=pltpu.CompilerParams(
            dimension_semantics=("arbitrary", "arbitrary"), vmem_limit_bytes=_VMEM_LIMIT),
        name="mixer",
    )(x, mem_k, mem_v, c0, n0, m0p, *wts)
    x1, c1, n1, m1p = outs[:4]
    vn = outs[4] if want_vn else None
    return x1, c1, n1, m1p[:, :ML_HEADS, 0], vn


def _batcher_pairs(n):
    pairs = []
    p = 1
    while p < n:
        k = p
        while k >= 1:
            for j in range(k % p, n - k, 2 * k):
                for i in range(min(k, n - j - k)):
                    if (i + j) // (2 * p) == (i + j + k) // (2 * p):
                        pairs.append((i + j, i + j + k))
            k //= 2
        p *= 2
    return pairs


_SORT16 = _batcher_pairs(PEER_NKEYS // 8)
_SEL_UNROLL = 2


def _sorted_columns(c):
    cols = [c[8 * k:8 * k + 8, :] for k in range(PEER_NKEYS // 8)]
    for i, j in _SORT16:
        cols[i], cols[j] = jnp.maximum(cols[i], cols[j]), jnp.minimum(cols[i], cols[j])
    return cols


def _pop_top(cols, out_scr):
    for r in range(PEER_TOPK):
        mx = jnp.max(cols[0], axis=0, keepdims=True)
        out_scr[r:r + 1, :] = mx
        left = PEER_TOPK - 1 - r
        if left:
            eq = cols[0] == mx
            for k in range(left):
                cols[k] = jnp.where(eq, cols[k + 1], cols[k])


def _peer_select_kernel(x1_ref, gffn_ref, wpq_ref, k1_ref, k2_ref,
                        xn_ref, cnt_ref, p1n_ref, rank_ref, p2_ref,
                        q_scr, c1_scr, c2_scr, v1_scr, v2_scr, n_scr, *, tb):
    xn = _rms(x1_ref[...], gffn_ref[...]).astype(_BF16)
    xn_ref[...] = xn
    q_scr[...] = _dot(xn, wpq_ref[...])
    k1 = k1_ref[...].astype(_BF16)
    k2 = k2_ref[...].astype(_BF16)
    half = PEER_DQ // 2
    K = PEER_TOPK

    def lane_group(h, ls):
        c1 = c1_scr[:, ls]
        c2 = c2_scr[:, ls]
        _pop_top(_sorted_columns(c1), v1_scr)
        _pop_top(_sorted_columns(c2), v2_scr)
        v1 = v1_scr[...]
        lists = [v1_scr[0:8, :] + v2_scr[b:b + 1, :] for b in range(8)]
        ea = v1_scr[8:16, :] + v2_scr[0:1, :]
        eb = v1_scr[0:1, :] + v2_scr[8:16, :]
        cands = lists + [ea, eb]
        tau = None
        for r in range(K):
            tau = jnp.max(jnp.maximum(jnp.maximum(lists[0], ea), eb), axis=0, keepdims=True)
            left = K - 1 - r
            if left:
                eq = lists[0] == tau
                for k in range(min(left, 7)):
                    lists[k] = jnp.where(eq, lists[k + 1], lists[k])
                if left >= 8:
                    lists[7] = jnp.where(eq, _NEG_INF, lists[7])
                ea = jnp.where(ea == tau, _NEG_INF, ea)
                eb = jnp.where(eb == tau, _NEG_INF, eb)
        z = sum(jnp.sum(jnp.where(c >= tau, jnp.exp(c), 0.0), axis=0, keepdims=True) for c in cands)
        n = jnp.zeros(v1.shape, _F32)
        for b in range(K):
            n = jnp.where(v1 + v2_scr[b:b + 1, :] >= tau, float(b + 1), n)
        n_scr[...] = n
        cnt = jnp.zeros(c1.shape, _F32)
        rank2 = jnp.full(c2.shape, float(K), _F32)
        for a in range(K - 1, -1, -1):
            cnt = jnp.where(c1 >= v1_scr[a:a + 1, :], n_scr[a:a + 1, :], cnt)
            rank2 = jnp.where(c2 >= v2_scr[a:a + 1, :], float(a), rank2)
        cnt_ref[h, :, ls] = cnt
        p1n_ref[h, :, ls] = jnp.exp(c1) / z
        rank_ref[h, :, ls] = rank2.astype(_BF16)
        p2_ref[h, :, ls] = jnp.exp(c2).astype(_BF16)

    def head(h, carry):
        q0 = pl.multiple_of(h * PEER_DQ, PEER_DQ)
        s1 = _dot_nt(k1, q_scr[:, pl.ds(q0, half)].astype(_BF16))
        s2 = _dot_nt(k2, q_scr[:, pl.ds(q0 + half, half)].astype(_BF16))
        c1_scr[...] = s1 - jnp.max(s1, axis=0, keepdims=True)
        c2_scr[...] = s2 - jnp.max(s2, axis=0, keepdims=True)

        def lane_groups(i, carry):
            for u in range(_SEL_UNROLL):
                lane_group(h, pl.ds(pl.multiple_of((i * _SEL_UNROLL + u) * 128, 128), 128))
            return carry

        lax.fori_loop(0, tb // (128 * _SEL_UNROLL), lane_groups, 0)
        return carry

    lax.fori_loop(0, PEER_HEADS, head, 0)


def _peer_select(x1, g_ffn, wpq_bf, k_sub1, k_sub2, *, tb):
    T, D = x1.shape
    assert T % tb == 0 and tb % (128 * _SEL_UNROLL) == 0
    sel = lambda dt: jax.ShapeDtypeStruct((PEER_HEADS, PEER_NKEYS, T), dt)
    sel_spec = pl.BlockSpec((PEER_HEADS, PEER_NKEYS, tb), lambda i: (0, 0, i))
    const2 = lambda i: (0, 0)
    return pl.pallas_call(
        functools.partial(_peer_select_kernel, tb=tb),
        out_shape=(jax.ShapeDtypeStruct((T, D), _BF16), sel(_F32), sel(_F32), sel(_BF16), sel(_BF16)),
        grid=(T // tb,),
        in_specs=[
            pl.BlockSpec((tb, D), lambda i: (i, 0)),
            pl.BlockSpec((1, D), const2),
            pl.BlockSpec((D, PEER_HEADS * PEER_DQ), const2),
            pl.BlockSpec((PEER_NKEYS, PEER_DQ // 2), const2),
            pl.BlockSpec((PEER_NKEYS, PEER_DQ // 2), const2),
        ],
        out_specs=(pl.BlockSpec((tb, D), lambda i: (i, 0)),) + (sel_spec,) * 4,
        scratch_shapes=[
            pltpu.VMEM((tb, PEER_HEADS * PEER_DQ), _F32),
            pltpu.VMEM((PEER_NKEYS, tb), _F32),
            pltpu.VMEM((PEER_NKEYS, tb), _F32),
            pltpu.VMEM((PEER_TOPK, 128), _F32),
            pltpu.VMEM((PEER_TOPK, 128), _F32),
            pltpu.VMEM((PEER_TOPK, 128), _F32),
        ],
        compiler_params=pltpu.CompilerParams(
            dimension_semantics=("arbitrary",), vmem_limit_bytes=_VMEM_LIMIT),
        name="peer_select",
    )(x1, g_ffn.reshape(1, D), wpq_bf, k_sub1, k_sub2)


def _peer_dense_kernel(xn_ref, u_ref, vt_ref, cnt_ref, p1n_ref, rank_ref, p2_ref, x1_ref, gfin_ref,
                       y_ref, acc_scr, st_scr, h_scr, *, tb, rows):
    j = pl.program_id(1)

    @pl.when(j == 0)
    def _():
        acc_scr[...] = jnp.zeros_like(acc_scr)

    tile = (PEER_NKEYS, 128)
    rb = 2
    for b in range(rows // rb):
        bs = slice(b * rb * PEER_NKEYS, (b + 1) * rb * PEER_NKEYS)
        st_scr[bs, :] = _dot_nt(u_ref[bs, :], xn_ref[...])
        for r in range(b * rb, (b + 1) * rb):
            es = slice(r * PEER_NKEYS, (r + 1) * PEER_NKEYS)
            for lg in range(tb // 128):
                ls = slice(lg * 128, (lg + 1) * 128)
                gate = jnp.zeros(tile, _BF16)
                for h in range(PEER_HEADS):
                    cnt = jnp.broadcast_to(cnt_ref[h, r:r + 1, ls].astype(_BF16), tile)
                    pn = jnp.broadcast_to(p1n_ref[h, r:r + 1, ls].astype(_BF16), tile)
                    sel = jnp.minimum(jnp.maximum(cnt - rank_ref[h, :, ls], 0), pn)
                    gate = gate + sel * p2_ref[h, :, ls]
                act = jax.nn.gelu(st_scr[es, ls]).astype(_BF16)
                h_scr[es, ls] = gate * act
    acc_scr[...] += _dot(vt_ref[0], h_scr[...])

    @pl.when(j == pl.num_programs(1) - 1)
    def _():
        x2 = x1_ref[...] + acc_scr[...].T
        y_ref[...] = _rms(x2, gfin_ref[...])


def _peer_dense(xn, u_bf, vt_bf, cnt, p1n, rank2, p2, x1, g_final, *, tb, ec):
    T, D = x1.shape
    rows = ec // PEER_NKEYS
    assert T % tb == 0 and PEER_NEXP % ec == 0 and rows % 8 == 0
    sel_all = pl.BlockSpec((PEER_HEADS, PEER_NKEYS, tb), lambda i, j: (0, 0, i))
    sel_rows = pl.BlockSpec((PEER_HEADS, rows, tb), lambda i, j: (0, j, i))
    tok = pl.BlockSpec((tb, D), lambda i, j: (i, 0))
    return pl.pallas_call(
        functools.partial(_peer_dense_kernel, tb=tb, rows=rows),
        out_shape=jax.ShapeDtypeStruct((T, D), _F32),
        grid=(T // tb, PEER_NEXP // ec),
        in_specs=[
            tok,
            pl.BlockSpec((ec, D), lambda i, j: (j, 0)),
            pl.BlockSpec((1, D, ec), lambda i, j: (j, 0, 0)),
            sel_rows, sel_rows, sel_all, sel_all,
            tok,
            pl.BlockSpec((1, D), lambda i, j: (0, 0)),
        ],
        out_specs=tok,
        scratch_shapes=[
            pltpu.VMEM((D, tb), _F32),
            pltpu.VMEM((ec, tb), _F32),
            pltpu.VMEM((ec, tb), _BF16),
        ],
        compiler_params=pltpu.CompilerParams(
            dimension_semantics=("arbitrary", "arbitrary"), vmem_limit_bytes=_VMEM_LIMIT),
        name="peer_dense",
    )(xn, u_bf, vt_bf, cnt, p1n, rank2, p2, x1, g_final.reshape(1, D))


def _mixer_weights(g_mix, w_in, b_in, g_mlh, g_sgu, w_s, b_s, w_out):
    nq = ML_HEADS * ML_DQK
    nv = ML_HEADS * ML_DV
    o_q, o_k, o_v = 0, nq, 2 * nq
    o_ig = o_v + nv
    o_fg = o_ig + ML_HEADS
    o_og = o_fg + ML_HEADS
    o_su = o_og + nv
    o_sv = o_su + D_MODEL
    o_xq = o_sv + D_MODEL
    o_gt = o_xq + D_MODEL
    main = lambda a: jnp.concatenate([a[..., o_q:o_ig], a[..., o_og:o_gt + 3 * D_MODEL]], axis=-1)
    w_if = w_in[:, o_ig:o_og]
    b_if = b_in[o_ig:o_og]
    return (
        g_mix.reshape(1, D_MODEL),
        main(w_in).astype(_BF16),
        main(b_in).reshape(1, _P_COLS),
        w_if.astype(_BF16),
        w_if.T.astype(_BF16),
        b_if.reshape(1, 8),
        b_if.reshape(8, 1),
        g_mlh.reshape(1, D_MODEL),
        g_sgu.reshape(1, D_MODEL),
        w_s,
        b_s.T,
        w_out.astype(_BF16),
    )


def kernel(x_prompt, x_sample, mem_prompt, cache_mem_k, cache_mem_v, state_mlstm_C, state_mlstm_n,
           state_mlstm_m, g_mix, w_in, b_in, g_mlh, g_sgu, w_s, b_s, g_mem, w_mk, w_mv, w_out, g_ffn,
           w_pq, k_sub1, k_sub2, u_exp, v_exp, g_final):
    depth = g_mix.shape[0]
    assert depth == 1
    l = 0
    Bp, Sp, D = x_prompt.shape
    Bs, Ss, _ = x_sample.shape

    wts = _mixer_weights(g_mix[l], w_in[l], b_in[l], g_mlh[l], g_sgu[l], w_s[l], b_s[l], w_out[l])

    mk, mv = _mem_kv(mem_prompt, g_mem[l], w_mk[l], w_mv[l])
    zc = jnp.zeros((Bp, ML_HEADS, ML_DQK, ML_DV), _F32)
    zn = jnp.zeros((Bp, ML_HEADS, ML_DQK), _F32)
    zm = jnp.zeros((Bp, ML_HEADS), _F32)
    x1p, cp, np_, mp, _ = _mixer(x_prompt, mk, mv, zc, zn, zm, wts, tc=256, want_vn=False)
    x1s, cs, ns, ms, vn = _mixer(
        x_sample, cache_mem_k[l].reshape(Bs, N_MEM, D), cache_mem_v[l].reshape(Bs, N_MEM, D),
        state_mlstm_C[l], state_mlstm_n[l], state_mlstm_m[l], wts, tc=Ss, want_vn=True)

    wpq_bf = w_pq[l].astype(_BF16)
    u_bf = u_exp[l].astype(_BF16)
    vt_bf = v_exp[l].astype(_BF16).reshape(PEER_NEXP // _PEER_EC, _PEER_EC, D).transpose(0, 2, 1)

    def peer(x1):
        B, S, _ = x1.shape
        x1 = x1.reshape(B * S, D)
        xn, cnt, p1n, rank2, p2 = _peer_select(x1, g_ffn[l], wpq_bf, k_sub1[l], k_sub2[l], tb=_PEER_TB)
        y = _peer_dense(xn, u_bf, vt_bf, cnt, p1n, rank2, p2, x1, g_final, tb=_PEER_TB, ec=_PEER_EC)
        return y.reshape(B, S, D)

    hs = (XA_HEADS, XA_DH)
    return (peer(x1p), peer(x1s), cp[None], np_[None], mp[None],
            mk.reshape(1, Bp, N_MEM, *hs), mv.reshape(1, Bp, N_MEM, *hs),
            cs[None], ns[None], ms[None], vn[None])
```

```python
import functools

import jax
import jax.numpy as jnp
from jax import lax
from jax.experimental import pallas as pl
from jax.experimental.pallas import tpu as pltpu

D_MODEL = 1024
EPS = 1e-6
CHUNK = 64
N_MEM = 256
ML_HEADS = 4
ML_DQK = 128
ML_DV = D_MODEL // ML_HEADS
SGU_CHUNK = 128
SGU_GROUPS = 4
SGU_GDIM = D_MODEL // SGU_GROUPS
XA_HEADS = 4
XA_DH = D_MODEL // XA_HEADS
PEER_HEADS = 8
PEER_NKEYS = 128
PEER_DQ = 256
PEER_TOPK = 16
PEER_NEXP = PEER_NKEYS * PEER_NKEYS

_Q0 = 0
_K0 = _Q0 + ML_HEADS * ML_DQK
_V0 = _K0 + ML_HEADS * ML_DQK
_OG0 = _V0 + ML_HEADS * ML_DV
_SU0 = _OG0 + D_MODEL
_SV0 = _SU0 + D_MODEL
_XQ0 = _SV0 + D_MODEL
_GT0 = _XQ0 + D_MODEL
_P_COLS = _GT0 + 3 * D_MODEL

_VMEM_LIMIT = 56 * 1024 * 1024

_PEER_TB = 512
_PEER_EC = 2048

_BF16 = jnp.bfloat16
_F32 = jnp.float32
_NEG_INF = float("-inf")


def _rms(xf, g):
    return xf * lax.rsqrt(jnp.mean(xf * xf, axis=-1, keepdims=True) + EPS) * g


def _dot(a, b):
    return jnp.dot(a, b, preferred_element_type=_F32)


def _dot_nt(a, b):
    return lax.dot_general(a, b, (((1,), (1,)), ((), ())), preferred_element_type=_F32)


def _bmm(a, b, ca, cb):
    return lax.dot_general(a, b, (((ca,), (cb,)), ((0,), (0,))), preferred_element_type=_F32)


def _mem_kv_kernel(mem_ref, g_ref, wk_ref, wv_ref, k_ref, v_ref):
    mn = _rms(mem_ref[0], g_ref[...]).astype(_BF16)
    k_ref[0] = _dot(mn, wk_ref[...])
    v_ref[0] = _dot(mn, wv_ref[...])


def _mem_kv(mem, g_mem, w_mk, w_mv):
    B = mem.shape[0]
    full = lambda b: (0, 0)
    return pl.pallas_call(
        _mem_kv_kernel,
        out_shape=(jax.ShapeDtypeStruct((B, N_MEM, D_MODEL), _F32),) * 2,
        grid=(B,),
        in_specs=[
            pl.BlockSpec((1, N_MEM, D_MODEL), lambda b: (b, 0, 0)),
            pl.BlockSpec((1, D_MODEL), full),
            pl.BlockSpec((D_MODEL, D_MODEL), full),
            pl.BlockSpec((D_MODEL, D_MODEL), full),
        ],
        out_specs=(pl.BlockSpec((1, N_MEM, D_MODEL), lambda b: (b, 0, 0)),) * 2,
        compiler_params=pltpu.CompilerParams(
            dimension_semantics=("arbitrary",), vmem_limit_bytes=_VMEM_LIMIT),
        name="mem_kv",
    )(mem, g_mem.reshape(1, D_MODEL), w_mk.astype(_BF16), w_mv.astype(_BF16))


def _mixer_kernel(x_ref, mk_ref, mv_ref, c0_ref, n0_ref, m0_ref, gmix_ref, win_ref, bin_ref,
                  wif_ref, wift_ref, bif_ref, bift_ref, gmlh_ref, gsgu_ref, ws_ref, bst_ref,
                  wout_ref,
                  x1_ref, c_out_ref, n_out_ref, m_out_ref, vn_ref,
                  p_scr, mrg_scr, c_scr, n_scr, m_scr, *, tc, sgu_len):
    ci = pl.program_id(1)
    nsub = tc // CHUNK

    @pl.when(ci == 0)
    def _():
        c_scr[...] = c0_ref[0]
        n_scr[...] = n0_ref[0]
        m_scr[...] = m0_ref[0]

    x = x_ref[0]
    xn = _rms(x, gmix_ref[...]).astype(_BF16)
    p_scr[...] = _dot(xn, win_ref[...]) + bin_ref[...]
    gif = _dot(xn, wif_ref[...]) + bif_ref[...]
    gift = _dot_nt(wift_ref[...], xn) + bift_ref[...]

    row = lax.broadcasted_iota(jnp.int32, (CHUNK, CHUNK), 0)
    col = lax.broadcasted_iota(jnp.int32, (CHUNK, CHUNK), 1)
    tril = row >= col
    tril_f = tril.astype(_F32)
    triu_f = (row <= col).astype(_F32)
    stack = lambda f: jnp.stack([f(h) for h in range(ML_HEADS)])
    for j in range(nsub):
        r0 = j * CHUNK
        rows = slice(r0, r0 + CHUNK)
        ig_c = gif[rows, 0:ML_HEADS]
        lf_c = jax.nn.log_sigmoid(gif[rows, ML_HEADS:2 * ML_HEADS])
        ig_r = gift[0:ML_HEADS, rows]
        lf_r = jax.nn.log_sigmoid(gift[ML_HEADS:2 * ML_HEADS, rows])
        b_c = jnp.dot(tril_f, lf_c, precision=lax.Precision.HIGHEST,
                      preferred_element_type=_F32)
        b_r = jnp.dot(lf_r, triu_f, precision=lax.Precision.HIGHEST,
                      preferred_element_type=_F32)
        q = stack(lambda h: p_scr[rows, _Q0 + h * ML_DQK:_Q0 + (h + 1) * ML_DQK]).astype(_BF16)
        k = stack(lambda h: p_scr[rows, _K0 + h * ML_DQK:_K0 + (h + 1) * ML_DQK]) * (ML_DQK ** -0.5)
        v = stack(lambda h: p_scr[rows, _V0 + h * ML_DV:_V0 + (h + 1) * ML_DV]).astype(_BF16)
        c_st = c_scr[...]
        n_st = stack(lambda h: n_scr[h:h + 1, :])
        m_st = stack(lambda h: m_scr[h:h + 1, 0:1])
        bc = stack(lambda h: b_c[:, h:h + 1])
        br = stack(lambda h: b_r[h:h + 1, :])
        ig_row = stack(lambda h: ig_r[h:h + 1, :])
        ig_col = stack(lambda h: ig_c[:, h:h + 1])
        logd = jnp.where(tril[None], bc - br + ig_row, _NEG_INF)
        log_prev = bc + m_st
        m_t = jnp.maximum(log_prev, jnp.max(logd, axis=2, keepdims=True))
        a = _bmm(q, k.astype(_BF16), 2, 2) * jnp.exp(logd - m_t)
        wp = jnp.exp(log_prev - m_t)
        num = _bmm(a.astype(_BF16), v, 2, 1) + wp * _bmm(q, c_st.astype(_BF16), 2, 1)
        qn = jnp.sum(q.astype(_F32) * n_st, axis=2, keepdims=True)
        den = jnp.sum(a, axis=2, keepdims=True) + wp * qn
        hh = num / jnp.maximum(jnp.abs(den), jnp.exp(-m_t))
        hn = hh * lax.rsqrt(jnp.mean(hh * hh, axis=2, keepdims=True) + EPS)
        for h in range(ML_HEADS):
            sl = slice(h * ML_DV, (h + 1) * ML_DV)
            og = p_scr[rows, _OG0 + h * ML_DV:_OG0 + (h + 1) * ML_DV]
            h_a = jax.nn.sigmoid(og) * hn[h] * gmlh_ref[:, sl]
            g0 = jax.nn.sigmoid(p_scr[rows, _GT0 + h * ML_DV:_GT0 + (h + 1) * ML_DV])
            mrg_scr[rows, sl] = g0 * h_a
        b_last = bc[:, CHUNK - 1:CHUNK, :]
        log_in = b_last - bc + ig_col
        m_new = jnp.maximum(b_last + m_st, jnp.max(log_in, axis=1, keepdims=True))
        wi = jnp.exp(log_in - m_new)
        wc = jnp.exp(b_last + m_st - m_new)
        kw = wi * k
        c_scr[...] = wc * c_st + _bmm(kw.astype(_BF16), v, 1, 1)
        n_new = wc * n_st + jnp.sum(kw, axis=1, keepdims=True)
        for h in range(ML_HEADS):
            n_scr[h:h + 1, :] = n_new[h]
            m_scr[h:h + 1, :] = jnp.broadcast_to(m_new[h], (1, 128))

    sv = jax.nn.gelu(p_scr[:, _SV0:_SV0 + D_MODEL])
    vn = _rms(sv, gsgu_ref[...])
    if vn_ref is not None:
        vn_ref[0] = vn
    vnb = vn.astype(_BF16)
    rs = lax.broadcasted_iota(jnp.int32, (sgu_len, sgu_len), 0)
    cs = lax.broadcasted_iota(jnp.int32, (sgu_len, sgu_len), 1)
    for g in range(SGU_GROUPS):
        wsg = jnp.where(rs >= cs, ws_ref[g, 0:sgu_len, 0:sgu_len], 0.0).astype(_BF16)
        bsg = bst_ref[0:sgu_len, g:g + 1]
        gl = slice(g * SGU_GDIM, (g + 1) * SGU_GDIM)
        for c in range(tc // sgu_len):
            rows = slice(c * sgu_len, (c + 1) * sgu_len)
            mix = _dot(wsg, vnb[rows, gl]) + bsg
            u = jax.nn.gelu(p_scr[rows, _SU0 + g * SGU_GDIM:_SU0 + (g + 1) * SGU_GDIM])
            g1 = jax.nn.sigmoid(
                p_scr[rows, _GT0 + D_MODEL + g * SGU_GDIM:_GT0 + D_MODEL + (g + 1) * SGU_GDIM])
            mrg_scr[rows, gl] += g1 * (u * mix)

    for h in range(XA_HEADS):
        hl = slice(h * XA_DH, (h + 1) * XA_DH)
        xq = p_scr[:, _XQ0 + h * XA_DH:_XQ0 + (h + 1) * XA_DH].astype(_BF16)
        sc = _dot_nt(xq, mk_ref[0, :, hl].astype(_BF16)) * (XA_DH ** -0.5)
        sc = sc - jnp.max(sc, axis=1, keepdims=True)
        e = jnp.exp(sc)
        att = e / jnp.sum(e, axis=1, keepdims=True)
        h_c = _dot(att.astype(_BF16), mv_ref[0, :, hl].astype(_BF16))
        g2 = jax.nn.sigmoid(p_scr[:, _GT0 + 2 * D_MODEL + h * XA_DH:_GT0 + 2 * D_MODEL + (h + 1) * XA_DH])
        mrg_scr[:, hl] += g2 * h_c

    x1_ref[0] = x + _dot(mrg_scr[...].astype(_BF16), wout_ref[...])

    @pl.when(ci == pl.num_programs(1) - 1)
    def _():
        c_out_ref[0] = c_scr[...]
        n_out_ref[0] = n_scr[...]
        m_out_ref[0] = m_scr[...]


def _mixer(x, mem_k, mem_v, c0, n0, m0, wts, *, tc, want_vn):
    B, S, D = x.shape
    sgu_len = min(S, SGU_CHUNK)
    assert S % tc == 0 and tc % CHUNK == 0 and tc % sgu_len == 0
    nchunks = S // tc
    m0p = jnp.broadcast_to(m0[:, :, None], (B, ML_HEADS, 128))
    m0p = jnp.concatenate([m0p, jnp.zeros((B, 8 - ML_HEADS, 128), _F32)], axis=1)

    def body(*refs):
        ins, rest = refs[:18], refs[18:]
        if want_vn:
            outs, scr = rest[:5], rest[5:]
        else:
            outs, scr = rest[:4] + (None,), rest[4:]
        _mixer_kernel(*ins, *outs, *scr, tc=tc, sgu_len=sgu_len)

    const2 = lambda b, c: (0, 0)
    const3 = lambda b, c: (0, 0, 0)
    per_b3 = lambda b, c: (b, 0, 0)
    per_b4 = lambda b, c: (b, 0, 0, 0)
    once = dict(pipeline_mode=pl.Buffered(1))
    in_specs = [
        pl.BlockSpec((1, tc, D), lambda b, c: (b, c, 0)),
        pl.BlockSpec((1, N_MEM, D), per_b3),
        pl.BlockSpec((1, N_MEM, D), per_b3),
        pl.BlockSpec((1, ML_HEADS, ML_DQK, ML_DV), per_b4),
        pl.BlockSpec((1, ML_HEADS, ML_DQK), per_b3),
        pl.BlockSpec((1, 8, 128), per_b3),
        pl.BlockSpec((1, D), const2, **once),
        pl.BlockSpec((D, _P_COLS), const2, **once),
        pl.BlockSpec((1, _P_COLS), const2, **once),
        pl.BlockSpec((D, 8), const2, **once),
        pl.BlockSpec((8, D), const2, **once),
        pl.BlockSpec((1, 8), const2, **once),
        pl.BlockSpec((8, 1), const2, **once),
        pl.BlockSpec((1, D), const2, **once),
        pl.BlockSpec((1, D), const2, **once),
        pl.BlockSpec((SGU_GROUPS, SGU_CHUNK, SGU_CHUNK), const3, **once),
        pl.BlockSpec((SGU_CHUNK, SGU_GROUPS), const2, **once),
        pl.BlockSpec((D, D), const2, **once),
    ]
    out_shape = [
        jax.ShapeDtypeStruct((B, S, D), _F32),
        jax.ShapeDtypeStruct((B, ML_HEADS, ML_DQK, ML_DV), _F32),
        jax.ShapeDtypeStruct((B, ML_HEADS, ML_DQK), _F32),
        jax.ShapeDtypeStruct((B, 8, 128), _F32),
    ]
    out_specs = [
        pl.BlockSpec((1, tc, D), lambda b, c: (b, c, 0)),
        pl.BlockSpec((1, ML_HEADS, ML_DQK, ML_DV), per_b4),
        pl.BlockSpec((1, ML_HEADS, ML_DQK), per_b3),
        pl.BlockSpec((1, 8, 128), per_b3),
    ]
    if want_vn:
        out_shape.append(jax.ShapeDtypeStruct((B, S, D), _F32))
        out_specs.append(pl.BlockSpec((1, tc, D), lambda b, c: (b, c, 0)))
    outs = pl.pallas_call(
        body,
        out_shape=tuple(out_shape),
        grid=(B, nchunks),
        in_specs=in_specs,
        out_specs=tuple(out_specs),
        scratch_shapes=[
            pltpu.VMEM((tc, _P_COLS), _F32),
            pltpu.VMEM((tc, D), _F32),
            pltpu.VMEM((ML_HEADS, ML_DQK, ML_DV), _F32),
            pltpu.VMEM((ML_HEADS, ML_DQK), _F32),
            pltpu.VMEM((8, 128), _F32),
        ],
        compiler_params=pltpu.CompilerParams(
            dimension_semantics=("arbitrary", "arbitrary"), vmem_limit_bytes=_VMEM_LIMIT),
        name="mixer",
    )(x, mem_k, mem_v, c0, n0, m0p, *wts)
    x1, c1, n1, m1p = outs[:4]
    vn = outs[4] if want_vn else None
    return x1, c1, n1, m1p[:, :ML_HEADS, 0], vn


def _batcher_pairs(n):
    pairs = []
    p = 1
    while p < n:
        k = p
        while k >= 1:
            for j in range(k % p, n - k, 2 * k):
                for i in range(min(k, n - j - k)):
                    if (i + j) // (2 * p) == (i + j + k) // (2 * p):
                        pairs.append((i + j, i + j + k))
            k //= 2
        p *= 2
    return pairs


_SORT16 = _batcher_pairs(PEER_NKEYS // 8)
_SEL_UNROLL = 2


def _sorted_columns(c):
    cols = [c[8 * k:8 * k + 8, :] for k in range(PEER_NKEYS // 8)]
    for i, j in _SORT16:
        cols[i], cols[j] = jnp.maximum(cols[i], cols[j]), jnp.minimum(cols[i], cols[j])
    return cols


def _pop_top(cols, out_scr):
    for r in range(PEER_TOPK):
        mx = jnp.max(cols[0], axis=0, keepdims=True)
        out_scr[r:r + 1, :] = mx
        left = PEER_TOPK - 1 - r
        if left:
            eq = cols[0] == mx
            for k in range(left):
                cols[k] = jnp.where(eq, cols[k + 1], cols[k])


def _peer_select_kernel(x1_ref, gffn_ref, wpq_ref, k1_ref, k2_ref,
                        xn_ref, cnt_ref, p1n_ref, rank_ref, p2_ref,
                        q_scr, c1_scr, c2_scr, v1_scr, v2_scr, n_scr, *, tb):
    xn = _rms(x1_ref[...], gffn_ref[...]).astype(_BF16)
    xn_ref[...] = xn
    q_scr[...] = _dot(xn, wpq_ref[...])
    k1 = k1_ref[...].astype(_BF16)
    k2 = k2_ref[...].astype(_BF16)
    half = PEER_DQ // 2
    K = PEER_TOPK

    def lane_group(h, ls):
        c1 = c1_scr[:, ls]
        c2 = c2_scr[:, ls]
        _pop_top(_sorted_columns(c1), v1_scr)
        _pop_top(_sorted_columns(c2), v2_scr)
        v1 = v1_scr[...]
        lists = [v1_scr[0:8, :] + v2_scr[b:b + 1, :] for b in range(8)]
        ea = v1_scr[8:16, :] + v2_scr[0:1, :]
        eb = v1_scr[0:1, :] + v2_scr[8:16, :]
        cands = lists + [ea, eb]
        tau = None
        for r in range(K):
            tau = jnp.max(jnp.maximum(jnp.maximum(lists[0], ea), eb), axis=0, keepdims=True)
            left = K - 1 - r
            if left:
                eq = lists[0] == tau
                for k in range(min(left, 7)):
                    lists[k] = jnp.where(eq, lists[k + 1], lists[k])
                if left >= 8:
                    lists[7] = jnp.where(eq, _NEG_INF, lists[7])
                ea = jnp.where(ea == tau, _NEG_INF, ea)
                eb = jnp.where(eb == tau, _NEG_INF, eb)
        z = sum(jnp.sum(jnp.where(c >= tau, jnp.exp(c), 0.0), axis=0, keepdims=True) for c in cands)
        n = jnp.zeros(v1.shape, _F32)
        for b in range(K):
            n = jnp.where(v1 + v2_scr[b:b + 1, :] >= tau, float(b + 1), n)
        n_scr[...] = n
        cnt = jnp.zeros(c1.shape, _F32)
        rank2 = jnp.full(c2.shape, float(K), _F32)
        for a in range(K - 1, -1, -1):
            cnt = jnp.where(c1 >= v1_scr[a:a + 1, :], n_scr[a:a + 1, :], cnt)
            rank2 = jnp.where(c2 >= v2_scr[a:a + 1, :], float(a), rank2)
        cnt_ref[h, :, ls] = cnt
        p1n_ref[h, :, ls] = jnp.exp(c1) / z
        rank_ref[h, :, ls] = rank2.astype(_BF16)
        p2_ref[h, :, ls] = jnp.exp(c2).astype(_BF16)

    def head(h, carry):
        q0 = pl.multiple_of(h * PEER_DQ, PEER_DQ)
        s1 = _dot_nt(k1, q_scr[:, pl.ds(q0, half)].astype(_BF16))
        s2 = _dot_nt(k2, q_scr[:, pl.ds(q0 + half, half)].astype(_BF16))
        c1_scr[...] = s1 - jnp.max(s1, axis=0, keepdims=True)
        c2_scr[...] = s2 - jnp.max(s2, axis=0, keepdims=True)

        def lane_groups(i, carry):
            for u in range(_SEL_UNROLL):
                lane_group(h, pl.ds(pl.multiple_of((i * _SEL_UNROLL + u) * 128, 128), 128))
            return carry

        lax.fori_loop(0, tb // (128 * _SEL_UNROLL), lane_groups, 0)
        return carry

    lax.fori_loop(0, PEER_HEADS, head, 0)


def _peer_select(x1, g_ffn, wpq_bf, k_sub1, k_sub2, *, tb):
    T, D = x1.shape
    assert T % tb == 0 and tb % (128 * _SEL_UNROLL) == 0
    sel = lambda dt: jax.ShapeDtypeStruct((PEER_HEADS, PEER_NKEYS, T), dt)
    sel_spec = pl.BlockSpec((PEER_HEADS, PEER_NKEYS, tb), lambda i: (0, 0, i))
    const2 = lambda i: (0, 0)
    return pl.pallas_call(
        functools.partial(_peer_select_kernel, tb=tb),
        out_shape=(jax.ShapeDtypeStruct((T, D), _BF16), sel(_F32), sel(_F32), sel(_BF16), sel(_BF16)),
        grid=(T // tb,),
        in_specs=[
            pl.BlockSpec((tb, D), lambda i: (i, 0)),
            pl.BlockSpec((1, D), const2),
            pl.BlockSpec((D, PEER_HEADS * PEER_DQ), const2),
            pl.BlockSpec((PEER_NKEYS, PEER_DQ // 2), const2),
            pl.BlockSpec((PEER_NKEYS, PEER_DQ // 2), const2),
        ],
        out_specs=(pl.BlockSpec((tb, D), lambda i: (i, 0)),) + (sel_spec,) * 4,
        scratch_shapes=[
            pltpu.VMEM((tb, PEER_HEADS * PEER_DQ), _F32),
            pltpu.VMEM((PEER_NKEYS, tb), _F32),
            pltpu.VMEM((PEER_NKEYS, tb), _F32),
            pltpu.VMEM((PEER_TOPK, 128), _F32),
            pltpu.VMEM((PEER_TOPK, 128), _F32),
            pltpu.VMEM((PEER_TOPK, 128), _F32),
        ],
        compiler_params=pltpu.CompilerParams(
            dimension_semantics=("arbitrary",), vmem_limit_bytes=_VMEM_LIMIT),
        name="peer_select",
    )(x1, g_ffn.reshape(1, D), wpq_bf, k_sub1, k_sub2)


def _peer_dense_kernel(xn_ref, u_ref, vt_ref, cnt_ref, p1n_ref, rank_ref, p2_ref, x1_ref, gfin_ref,
                       y_ref, acc_scr, st_scr, h_scr, *, tb, rows):
    j = pl.program_id(1)

    @pl.when(j == 0)
    def _():
        acc_scr[...] = jnp.zeros_like(acc_scr)

    tile = (PEER_NKEYS, 128)
    rb = 2
    for b in range(rows // rb):
        bs = slice(b * rb * PEER_NKEYS, (b + 1) * rb * PEER_NKEYS)
        st_scr[bs, :] = _dot_nt(u_ref[bs, :], xn_ref[...])
        for r in range(b * rb, (b + 1) * rb):
            es = slice(r * PEER_NKEYS, (r + 1) * PEER_NKEYS)
            for lg in range(tb // 128):
                ls = slice(lg * 128, (lg + 1) * 128)
                gate = jnp.zeros(tile, _BF16)
                for h in range(PEER_HEADS):
                    cnt = jnp.broadcast_to(cnt_ref[h, r:r + 1, ls].astype(_BF16), tile)
                    pn = jnp.broadcast_to(p1n_ref[h, r:r + 1, ls].astype(_BF16), tile)
                    sel = jnp.minimum(jnp.maximum(cnt - rank_ref[h, :, ls], 0), pn)
                    gate = gate + sel * p2_ref[h, :, ls]
                act = jax.nn.gelu(st_scr[es, ls]).astype(_BF16)
                h_scr[es, ls] = gate * act
    acc_scr[...] += _dot(vt_ref[0], h_scr[...])

    @pl.when(j == pl.num_programs(1) - 1)
    def _():
        x2 = x1_ref[...] + acc_scr[...].T
        y_ref[...] = _rms(x2, gfin_ref[...])


def _peer_dense(xn, u_bf, vt_bf, cnt, p1n, rank2, p2, x1, g_final, *, tb, ec):
    T, D = x1.shape
    rows = ec // PEER_NKEYS
    assert T % tb == 0 and PEER_NEXP % ec == 0 and rows % 8 == 0
    sel_all = pl.BlockSpec((PEER_HEADS, PEER_NKEYS, tb), lambda i, j: (0, 0, i))
    sel_rows = pl.BlockSpec((PEER_HEADS, rows, tb), lambda i, j: (0, j, i))
    tok = pl.BlockSpec((tb, D), lambda i, j: (i, 0))
    return pl.pallas_call(
        functools.partial(_peer_dense_kernel, tb=tb, rows=rows),
        out_shape=jax.ShapeDtypeStruct((T, D), _F32),
        grid=(T // tb, PEER_NEXP // ec),
        in_specs=[
            tok,
            pl.BlockSpec((ec, D), lambda i, j: (j, 0)),
            pl.BlockSpec((1, D, ec), lambda i, j: (j, 0, 0)),
            sel_rows, sel_rows, sel_all, sel_all,
            tok,
            pl.BlockSpec((1, D), lambda i, j: (0, 0)),
        ],
        out_specs=tok,
        scratch_shapes=[
            pltpu.VMEM((D, tb), _F32),
            pltpu.VMEM((ec, tb), _F32),
            pltpu.VMEM((ec, tb), _BF16),
        ],
        compiler_params=pltpu.CompilerParams(
            dimension_semantics=("arbitrary", "arbitrary"), vmem_limit_bytes=_VMEM_LIMIT),
        name="peer_dense",
    )(xn, u_bf, vt_bf, cnt, p1n, rank2, p2, x1, g_final.reshape(1, D))


def _mixer_weights(g_mix, w_in, b_in, g_mlh, g_sgu, w_s, b_s, w_out):
    nq = ML_HEADS * ML_DQK
    nv = ML_HEADS * ML_DV
    o_q, o_k, o_v = 0, nq, 2 * nq
    o_ig = o_v + nv
    o_fg = o_ig + ML_HEADS
    o_og = o_fg + ML_HEADS
    o_su = o_og + nv
    o_sv = o_su + D_MODEL
    o_xq = o_sv + D_MODEL
    o_gt = o_xq + D_MODEL
    main = lambda a: jnp.concatenate([a[..., o_q:o_ig], a[..., o_og:o_gt + 3 * D_MODEL]], axis=-1)
    w_if = w_in[:, o_ig:o_og]
    b_if = b_in[o_ig:o_og]
    return (
        g_mix.reshape(1, D_MODEL),
        main(w_in).astype(_BF16),
        main(b_in).reshape(1, _P_COLS),
        w_if.astype(_BF16),
        w_if.T.astype(_BF16),
        b_if.reshape(1, 8),
        b_if.reshape(8, 1),
        g_mlh.reshape(1, D_MODEL),
        g_sgu.reshape(1, D_MODEL),
        w_s,
        b_s.T,
        w_out.astype(_BF16),
    )


def kernel(x_prompt, x_sample, mem_prompt, cache_mem_k, cache_mem_v, state_mlstm_C, state_mlstm_n,
           state_mlstm_m, g_mix, w_in, b_in, g_mlh, g_sgu, w_s, b_s, g_mem, w_mk, w_mv, w_out, g_ffn,
           w_pq, k_sub1, k_sub2, u_exp, v_exp, g_final):
    depth = g_mix.shape[0]
    assert depth == 1
    l = 0
    Bp, Sp, D = x_prompt.shape
    Bs, Ss, _ = x_sample.shape

    wts = _mixer_weights(g_mix[l], w_in[l], b_in[l], g_mlh[l], g_sgu[l], w_s[l], b_s[l], w_out[l])

    mk, mv = _mem_kv(mem_prompt, g_mem[l], w_mk[l], w_mv[l])
    zc = jnp.zeros((Bp, ML_HEADS, ML_DQK, ML_DV), _F32)
    zn = jnp.zeros((Bp, ML_HEADS, ML_DQK), _F32)
    zm = jnp.zeros((Bp, ML_HEADS), _F32)
    x1p, cp, np_, mp, _ = _mixer(x_prompt, mk, mv, zc, zn, zm, wts, tc=256, want_vn=False)
    x1s, cs, ns, ms, vn = _mixer(
        x_sample, cache_mem_k[l].reshape(Bs, N_MEM, D), cache_mem_v[l].reshape(Bs, N_MEM, D),
        state_mlstm_C[l], state_mlstm_n[l], state_mlstm_m[l], wts, tc=Ss, want_vn=True)

    wpq_bf = w_pq[l].astype(_BF16)
    u_bf = u_exp[l].astype(_BF16)
    vt_bf = v_exp[l].astype(_BF16).reshape(PEER_NEXP // _PEER_EC, _PEER_EC, D).transpose(0, 2, 1)

    def peer(x1):
        B, S, _ = x1.shape
        x1 = x1.reshape(B * S, D)
        xn, cnt, p1n, rank2, p2 = _peer_select(x1, g_ffn[l], wpq_bf, k_sub1[l], k_sub2[l], tb=_PEER_TB)
        y = _peer_dense(xn, u_bf, vt_bf, cnt, p1n, rank2, p2, x1, g_final, tb=_PEER_TB, ec=_PEER_EC)
        return y.reshape(B, S, D)

    hs = (XA_HEADS, XA_DH)
    return (peer(x1p), peer(x1s), cp[None], np_[None], mp[None],
            mk.reshape(1, Bp, N_MEM, *hs), mv.reshape(1, Bp, N_MEM, *hs),
            cs[None], ns[None], ms[None], vn[None])
```

```python
import functools

import jax
import jax.numpy as jnp
from jax import lax
from jax.experimental import pallas as pl
from jax.experimental.pallas import tpu as pltpu

D_MODEL = 1024
EPS = 1e-6
CHUNK = 64
N_MEM = 256
ML_HEADS = 4
ML_DQK = 128
ML_DV = D_MODEL // ML_HEADS
SGU_CHUNK = 128
SGU_GROUPS = 4
SGU_GDIM = D_MODEL // SGU_GROUPS
XA_HEADS = 4
XA_DH = D_MODEL // XA_HEADS
PEER_HEADS = 8
PEER_NKEYS = 128
PEER_DQ = 256
PEER_TOPK = 16
PEER_NEXP = PEER_NKEYS * PEER_NKEYS

_Q0 = 0
_K0 = _Q0 + ML_HEADS * ML_DQK
_V0 = _K0 + ML_HEADS * ML_DQK
_OG0 = _V0 + ML_HEADS * ML_DV
_SU0 = _OG0 + D_MODEL
_SV0 = _SU0 + D_MODEL
_XQ0 = _SV0 + D_MODEL
_GT0 = _XQ0 + D_MODEL
_P_COLS = _GT0 + 3 * D_MODEL

_VMEM_LIMIT = 56 * 1024 * 1024

_PEER_TB = 512
_PEER_EC = 2048

_BF16 = jnp.bfloat16
_F32 = jnp.float32
_NEG_INF = float("-inf")


def _rms(xf, g):
    return xf * lax.rsqrt(jnp.mean(xf * xf, axis=-1, keepdims=True) + EPS) * g


def _dot(a, b):
    return jnp.dot(a, b, preferred_element_type=_F32)


def _dot_nt(a, b):
    return lax.dot_general(a, b, (((1,), (1,)), ((), ())), preferred_element_type=_F32)


def _bmm(a, b, ca, cb):
    return lax.dot_general(a, b, (((ca,), (cb,)), ((0,), (0,))), preferred_element_type=_F32)


def _mem_kv_kernel(mem_ref, g_ref, wk_ref, wv_ref, k_ref, v_ref):
    mn = _rms(mem_ref[0], g_ref[...]).astype(_BF16)
    k_ref[0] = _dot(mn, wk_ref[...])
    v_ref[0] = _dot(mn, wv_ref[...])


def _mem_kv(mem, g_mem, w_mk, w_mv):
    B = mem.shape[0]
    full = lambda b: (0, 0)
    return pl.pallas_call(
        _mem_kv_kernel,
        out_shape=(jax.ShapeDtypeStruct((B, N_MEM, D_MODEL), _F32),) * 2,
        grid=(B,),
        in_specs=[
            pl.BlockSpec((1, N_MEM, D_MODEL), lambda b: (b, 0, 0)),
            pl.BlockSpec((1, D_MODEL), full),
            pl.BlockSpec((D_MODEL, D_MODEL), full),
            pl.BlockSpec((D_MODEL, D_MODEL), full),
        ],
        out_specs=(pl.BlockSpec((1, N_MEM, D_MODEL), lambda b: (b, 0, 0)),) * 2,
        compiler_params=pltpu.CompilerParams(
            dimension_semantics=("arbitrary",), vmem_limit_bytes=_VMEM_LIMIT),
        name="mem_kv",
    )(mem, g_mem.reshape(1, D_MODEL), w_mk.astype(_BF16), w_mv.astype(_BF16))


def _mixer_kernel(x_ref, mk_ref, mv_ref, c0_ref, n0_ref, m0_ref, gmix_ref, win_ref, bin_ref,
                  wif_ref, wift_ref, bif_ref, bift_ref, gmlh_ref, gsgu_ref, ws_ref, bst_ref,
                  wout_ref,
                  x1_ref, c_out_ref, n_out_ref, m_out_ref, vn_ref,
                  p_scr, mrg_scr, c_scr, n_scr, m_scr, *, tc, sgu_len):
    ci = pl.program_id(1)
    nsub = tc // CHUNK

    @pl.when(ci == 0)
    def _():
        c_scr[...] = c0_ref[0]
        n_scr[...] = n0_ref[0]
        m_scr[...] = m0_ref[0]

    x = x_ref[0]
    xn = _rms(x, gmix_ref[...]).astype(_BF16)
    p_scr[...] = _dot(xn, win_ref[...]) + bin_ref[...]
    gif = _dot(xn, wif_ref[...]) + bif_ref[...]
    gift = _dot_nt(wift_ref[...], xn) + bift_ref[...]

    row = lax.broadcasted_iota(jnp.int32, (CHUNK, CHUNK), 0)
    col = lax.broadcasted_iota(jnp.int32, (CHUNK, CHUNK), 1)
    tril = row >= col
    tril_f = tril.astype(_F32)
    triu_f = (row <= col).astype(_F32)
    stack = lambda f: jnp.stack([f(h) for h in range(ML_HEADS)])
    for j in range(nsub):
        r0 = j * CHUNK
        rows = slice(r0, r0 + CHUNK)
        ig_c = gif[rows, 0:ML_HEADS]
        lf_c = jax.nn.log_sigmoid(gif[rows, ML_HEADS:2 * ML_HEADS])
        ig_r = gift[0:ML_HEADS, rows]
        lf_r = jax.nn.log_sigmoid(gift[ML_HEADS:2 * ML_HEADS, rows])
        b_c = jnp.dot(tril_f, lf_c, precision=lax.Precision.HIGHEST,
                      preferred_element_type=_F32)
        b_r = jnp.dot(lf_r, triu_f, precision=lax.Precision.HIGHEST,
                      preferred_element_type=_F32)
        q = stack(lambda h: p_scr[rows, _Q0 + h * ML_DQK:_Q0 + (h + 1) * ML_DQK]).astype(_BF16)
        k = stack(lambda h: p_scr[rows, _K0 + h * ML_DQK:_K0 + (h + 1) * ML_DQK]) * (ML_DQK ** -0.5)
        v = stack(lambda h: p_scr[rows, _V0 + h * ML_DV:_V0 + (h + 1) * ML_DV]).astype(_BF16)
        c_st = c_scr[...]
        n_st = stack(lambda h: n_scr[h:h + 1, :])
        m_st = stack(lambda h: m_scr[h:h + 1, 0:1])
        bc = stack(lambda h: b_c[:, h:h + 1])
        br = stack(lambda h: b_r[h:h + 1, :])
        ig_row = stack(lambda h: ig_r[h:h + 1, :])
        ig_col = stack(lambda h: ig_c[:, h:h + 1])
        logd = jnp.where(tril[None], bc - br + ig_row, _NEG_INF)
        log_prev = bc + m_st
        m_t = jnp.maximum(log_prev, jnp.max(logd, axis=2, keepdims=True))
        a = _bmm(q, k.astype(_BF16), 2, 2) * jnp.exp(logd - m_t)
        wp = jnp.exp(log_prev - m_t)
        num = _bmm(a.astype(_BF16), v, 2, 1) + wp * _bmm(q, c_st.astype(_BF16), 2, 1)
        qn = jnp.sum(q.astype(_F32) * n_st, axis=2, keepdims=True)
        den = jnp.sum(a, axis=2, keepdims=True) + wp * qn
        hh = num / jnp.maximum(jnp.abs(den), jnp.exp(-m_t))
        hn = hh * lax.rsqrt(jnp.mean(hh * hh, axis=2, keepdims=True) + EPS)
        for h in range(ML_HEADS):
            sl = slice(h * ML_DV, (h + 1) * ML_DV)
            og = p_scr[rows, _OG0 + h * ML_DV:_OG0 + (h + 1) * ML_DV]
            h_a = jax.nn.sigmoid(og) * hn[h] * gmlh_ref[:, sl]
            g0 = jax.nn.sigmoid(p_scr[rows, _GT0 + h * ML_DV:_GT0 + (h + 1) * ML_DV])
            mrg_scr[rows, sl] = g0 * h_a
        b_last = bc[:, CHUNK - 1:CHUNK, :]
        log_in = b_last - bc + ig_col
        m_new = jnp.maximum(b_last + m_st, jnp.max(log_in, axis=1, keepdims=True))
        wi = jnp.exp(log_in - m_new)
        wc = jnp.exp(b_last + m_st - m_new)
        kw = wi * k
        c_scr[...] = wc * c_st + _bmm(kw.astype(_BF16), v, 1, 1)
        n_new = wc * n_st + jnp.sum(kw, axis=1, keepdims=True)
        for h in range(ML_HEADS):
            n_scr[h:h + 1, :] = n_new[h]
            m_scr[h:h + 1, :] = jnp.broadcast_to(m_new[h], (1, 128))

    sv = jax.nn.gelu(p_scr[:, _SV0:_SV0 + D_MODEL])
    vn = _rms(sv, gsgu_ref[...])
    if vn_ref is not None:
        vn_ref[0] = vn
    vnb = vn.astype(_BF16)
    rs = lax.broadcasted_iota(jnp.int32, (sgu_len, sgu_len), 0)
    cs = lax.broadcasted_iota(jnp.int32, (sgu_len, sgu_len), 1)
    for g in range(SGU_GROUPS):
        wsg = jnp.where(rs >= cs, ws_ref[g, 0:sgu_len, 0:sgu_len], 0.0).astype(_BF16)
        bsg = bst_ref[0:sgu_len, g:g + 1]
        gl = slice(g * SGU_GDIM, (g + 1) * SGU_GDIM)
        for c in range(tc // sgu_len):
            rows = slice(c * sgu_len, (c + 1) * sgu_len)
            mix = _dot(wsg, vnb[rows, gl]) + bsg
            u = jax.nn.gelu(p_scr[rows, _SU0 + g * SGU_GDIM:_SU0 + (g + 1) * SGU_GDIM])
            g1 = jax.nn.sigmoid(
                p_scr[rows, _GT0 + D_MODEL + g * SGU_GDIM:_GT0 + D_MODEL + (g + 1) * SGU_GDIM])
            mrg_scr[rows, gl] += g1 * (u * mix)

    for h in range(XA_HEADS):
        hl = slice(h * XA_DH, (h + 1) * XA_DH)
        xq = p_scr[:, _XQ0 + h * XA_DH:_XQ0 + (h + 1) * XA_DH].astype(_BF16)
        sc = _dot_nt(xq, mk_ref[0, :, hl].astype(_BF16)) * (XA_DH ** -0.5)
        sc = sc - jnp.max(sc, axis=1, keepdims=True)
        e = jnp.exp(sc)
        att = e / jnp.sum(e, axis=1, keepdims=True)
        h_c = _dot(att.astype(_BF16), mv_ref[0, :, hl].astype(_BF16))
        g2 = jax.nn.sigmoid(p_scr[:, _GT0 + 2 * D_MODEL + h * XA_DH:_GT0 + 2 * D_MODEL + (h + 1) * XA_DH])
        mrg_scr[:, hl] += g2 * h_c

    x1_ref[0] = x + _dot(mrg_scr[...].astype(_BF16), wout_ref[...])

    @pl.when(ci == pl.num_programs(1) - 1)
    def _():
        c_out_ref[0] = c_scr[...]
        n_out_ref[0] = n_scr[...]
        m_out_ref[0] = m_scr[...]


def _mixer(x, mem_k, mem_v, c0, n0, m0, wts, *, tc, want_vn):
    B, S, D = x.shape
    sgu_len = min(S, SGU_CHUNK)
    assert S % tc == 0 and tc % CHUNK == 0 and tc % sgu_len == 0
    nchunks = S // tc
    m0p = jnp.broadcast_to(m0[:, :, None], (B, ML_HEADS, 128))
    m0p = jnp.concatenate([m0p, jnp.zeros((B, 8 - ML_HEADS, 128), _F32)], axis=1)

    def body(*refs):
        ins, rest = refs[:18], refs[18:]
        if want_vn:
            outs, scr = rest[:5], rest[5:]
        else:
            outs, scr = rest[:4] + (None,), rest[4:]
        _mixer_kernel(*ins, *outs, *scr, tc=tc, sgu_len=sgu_len)

    const2 = lambda b, c: (0, 0)
    const3 = lambda b, c: (0, 0, 0)
    per_b3 = lambda b, c: (b, 0, 0)
    per_b4 = lambda b, c: (b, 0, 0, 0)
    once = dict(pipeline_mode=pl.Buffered(1))
    in_specs = [
        pl.BlockSpec((1, tc, D), lambda b, c: (b, c, 0)),
        pl.BlockSpec((1, N_MEM, D), per_b3),
        pl.BlockSpec((1, N_MEM, D), per_b3),
        pl.BlockSpec((1, ML_HEADS, ML_DQK, ML_DV), per_b4),
        pl.BlockSpec((1, ML_HEADS, ML_DQK), per_b3),
        pl.BlockSpec((1, 8, 128), per_b3),
        pl.BlockSpec((1, D), const2, **once),
        pl.BlockSpec((D, _P_COLS), const2, **once),
        pl.BlockSpec((1, _P_COLS), const2, **once),
        pl.BlockSpec((D, 8), const2, **once),
        pl.BlockSpec((8, D), const2, **once),
        pl.BlockSpec((1, 8), const2, **once),
        pl.BlockSpec((8, 1), const2, **once),
        pl.BlockSpec((1, D), const2, **once),
        pl.BlockSpec((1, D), const2, **once),
        pl.BlockSpec((SGU_GROUPS, SGU_CHUNK, SGU_CHUNK), const3, **once),
        pl.BlockSpec((SGU_CHUNK, SGU_GROUPS), const2, **once),
        pl.BlockSpec((D, D), const2, **once),
    ]
    out_shape = [
        jax.ShapeDtypeStruct((B, S, D), _F32),
        jax.ShapeDtypeStruct((B, ML_HEADS, ML_DQK, ML_DV), _F32),
        jax.ShapeDtypeStruct((B, ML_HEADS, ML_DQK), _F32),
        jax.ShapeDtypeStruct((B, 8, 128), _F32),
    ]
    out_specs = [
        pl.BlockSpec((1, tc, D), lambda b, c: (b, c, 0)),
        pl.BlockSpec((1, ML_HEADS, ML_DQK, ML_DV), per_b4),
        pl.BlockSpec((1, ML_HEADS, ML_DQK), per_b3),
        pl.BlockSpec((1, 8, 128), per_b3),
    ]
    if want_vn:
        out_shape.append(jax.ShapeDtypeStruct((B, S, D), _F32))
        out_specs.append(pl.BlockSpec((1, tc, D), lambda b, c: (b, c, 0)))
    outs = pl.pallas_call(
        body,
        out_shape=tuple(out_shape),
        grid=(B, nchunks),
        in_specs=in_specs,
        out_specs=tuple(out_specs),
        scratch_shapes=[
            pltpu.VMEM((tc, _P_COLS), _F32),
            pltpu.VMEM((tc, D), _F32),
            pltpu.VMEM((ML_HEADS, ML_DQK, ML_DV), _F32),
            pltpu.VMEM((ML_HEADS, ML_DQK), _F32),
            pltpu.VMEM((8, 128), _F32),
        ],
        compiler_params=pltpu.CompilerParams(
            dimension_semantics=("arbitrary", "arbitrary"), vmem_limit_bytes=_VMEM_LIMIT),
        name="mixer",
    )(x, mem_k, mem_v, c0, n0, m0p, *wts)
    x1, c1, n1, m1p = outs[:4]
    vn = outs[4] if want_vn else None
    return x1, c1, n1, m1p[:, :ML_HEADS, 0], vn


def _batcher_pairs(n):
    pairs = []
    p = 1
    while p < n:
        k = p
        while k >= 1:
            for j in range(k % p, n - k, 2 * k):
                for i in range(min(k, n - j - k)):
                    if (i + j) // (2 * p) == (i + j + k) // (2 * p):
                        pairs.append((i + j, i + j + k))
            k //= 2
        p *= 2
    return pairs


_SORT16 = _batcher_pairs(PEER_NKEYS // 8)
_SEL_UNROLL = 2


def _sorted_columns(c):
    cols = [c[8 * k:8 * k + 8, :] for k in range(PEER_NKEYS // 8)]
    for i, j in _SORT16:
        cols[i], cols[j] = jnp.maximum(cols[i], cols[j]), jnp.minimum(cols[i], cols[j])
    return cols


def _pop_top(cols, out_scr):
    for r in range(PEER_TOPK):
        mx = jnp.max(cols[0], axis=0, keepdims=True)
        out_scr[r:r + 1, :] = mx
        left = PEER_TOPK - 1 - r
        if left:
            eq = cols[0] == mx
            for k in range(left):
                cols[k] = jnp.where(eq, cols[k + 1], cols[k])


def _bf16_pair(x):
    u = pltpu.bitcast(x, jnp.uint32)
    u = u + jnp.uint32(0x7FFF) + ((u >> 16) & jnp.uint32(1))
    hi = u & jnp.uint32(0xFFFF0000)
    return hi | (hi >> 16)


def _peer_select_kernel(x1_ref, gffn_ref, wpq_ref, k1_ref, k2_ref,
                        xn_ref, cnt_ref, p1n_ref, rank_ref, p2_ref,
                        q_scr, c1_scr, c2_scr, v1_scr, v2_scr, n_scr, *, tb):
    xn = _rms(x1_ref[...], gffn_ref[...]).astype(_BF16)
    xn_ref[...] = xn
    q_scr[...] = _dot(xn, wpq_ref[...])
    k1 = k1_ref[...].astype(_BF16)
    k2 = k2_ref[...].astype(_BF16)
    half = PEER_DQ // 2
    K = PEER_TOPK

    def lane_group(h, ls):
        c1 = c1_scr[:, ls]
        c2 = c2_scr[:, ls]
        _pop_top(_sorted_columns(c1), v1_scr)
        _pop_top(_sorted_columns(c2), v2_scr)
        v1 = v1_scr[...]
        lists = [v1_scr[0:8, :] + v2_scr[b:b + 1, :] for b in range(8)]
        ea = v1_scr[8:16, :] + v2_scr[0:1, :]
        eb = v1_scr[0:1, :] + v2_scr[8:16, :]
        cands = lists + [ea, eb]
        tau = None
        for r in range(K):
            tau = jnp.max(jnp.maximum(jnp.maximum(lists[0], ea), eb), axis=0, keepdims=True)
            left = K - 1 - r
            if left:
                eq = lists[0] == tau
                for k in range(min(left, 7)):
                    lists[k] = jnp.where(eq, lists[k + 1], lists[k])
                if left >= 8:
                    lists[7] = jnp.where(eq, _NEG_INF, lists[7])
                ea = jnp.where(ea == tau, _NEG_INF, ea)
                eb = jnp.where(eb == tau, _NEG_INF, eb)
        z = sum(jnp.sum(jnp.where(c >= tau, jnp.exp(c), 0.0), axis=0, keepdims=True) for c in cands)
        n = jnp.zeros(v1.shape, _F32)
        for b in range(K):
            n = jnp.where(v1 + v2_scr[b:b + 1, :] >= tau, float(b + 1), n)
        n_scr[...] = n
        cnt = jnp.zeros(c1.shape, _F32)
        rank2 = jnp.full(c2.shape, float(K), _F32)
        for a in range(K - 1, -1, -1):
            cnt = jnp.where(c1 >= v1_scr[a:a + 1, :], n_scr[a:a + 1, :], cnt)
            rank2 = jnp.where(c2 >= v2_scr[a:a + 1, :], float(a), rank2)
        cnt_ref[h, :, ls] = _bf16_pair(cnt)
        p1n_ref[h, :, ls] = _bf16_pair(jnp.exp(c1) / z)
        rank_ref[h, :, ls] = rank2.astype(_BF16)
        p2_ref[h, :, ls] = jnp.exp(c2).astype(_BF16)

    def head(h, carry):
        q0 = pl.multiple_of(h * PEER_DQ, PEER_DQ)
        s1 = _dot_nt(k1, q_scr[:, pl.ds(q0, half)].astype(_BF16))
        s2 = _dot_nt(k2, q_scr[:, pl.ds(q0 + half, half)].astype(_BF16))
        c1_scr[...] = s1 - jnp.max(s1, axis=0, keepdims=True)
        c2_scr[...] = s2 - jnp.max(s2, axis=0, keepdims=True)

        def lane_groups(i, carry):
            for u in range(_SEL_UNROLL):
                lane_group(h, pl.ds(pl.multiple_of((i * _SEL_UNROLL + u) * 128, 128), 128))
            return carry

        lax.fori_loop(0, tb // (128 * _SEL_UNROLL), lane_groups, 0)
        return carry

    lax.fori_loop(0, PEER_HEADS, head, 0)


def _peer_select(x1, g_ffn, wpq_bf, k_sub1, k_sub2, *, tb):
    T, D = x1.shape
    assert T % tb == 0 and tb % (128 * _SEL_UNROLL) == 0
    sel = lambda dt: jax.ShapeDtypeStruct((PEER_HEADS, PEER_NKEYS, T), dt)
    sel_spec = pl.BlockSpec((PEER_HEADS, PEER_NKEYS, tb), lambda i: (0, 0, i))
    const2 = lambda i: (0, 0)
    return pl.pallas_call(
        functools.partial(_peer_select_kernel, tb=tb),
        out_shape=(jax.ShapeDtypeStruct((T, D), _BF16), sel(jnp.uint32), sel(jnp.uint32), sel(_BF16), sel(_BF16)),
        grid=(T // tb,),
        in_specs=[
            pl.BlockSpec((tb, D), lambda i: (i, 0)),
            pl.BlockSpec((1, D), const2),
            pl.BlockSpec((D, PEER_HEADS * PEER_DQ), const2),
            pl.BlockSpec((PEER_NKEYS, PEER_DQ // 2), const2),
            pl.BlockSpec((PEER_NKEYS, PEER_DQ // 2), const2),
        ],
        out_specs=(pl.BlockSpec((tb, D), lambda i: (i, 0)),) + (sel_spec,) * 4,
        scratch_shapes=[
            pltpu.VMEM((tb, PEER_HEADS * PEER_DQ), _F32),
            pltpu.VMEM((PEER_NKEYS, tb), _F32),
            pltpu.VMEM((PEER_NKEYS, tb), _F32),
            pltpu.VMEM((PEER_TOPK, 128), _F32),
            pltpu.VMEM((PEER_TOPK, 128), _F32),
            pltpu.VMEM((PEER_TOPK, 128), _F32),
        ],
        compiler_params=pltpu.CompilerParams(
            dimension_semantics=("arbitrary",), vmem_limit_bytes=_VMEM_LIMIT),
        name="peer_select",
    )(x1, g_ffn.reshape(1, D), wpq_bf, k_sub1, k_sub2)


def _peer_dense_kernel(xn_ref, u_ref, vt_ref, cnt_ref, p1n_ref, rank_ref, p2_ref, x1_ref, gfin_ref,
                       y_ref, acc_scr, st_scr, h_scr, *, tb, rows):
    j = pl.program_id(1)

    @pl.when(j == 0)
    def _():
        acc_scr[...] = jnp.zeros_like(acc_scr)

    tile = (PEER_NKEYS, 128)
    words = (PEER_NKEYS // 2, 128)
    rb = 2
    for b in range(rows // rb):
        bs = slice(b * rb * PEER_NKEYS, (b + 1) * rb * PEER_NKEYS)
        st_scr[bs, :] = _dot_nt(u_ref[bs, :], xn_ref[...])
        for r in range(b * rb, (b + 1) * rb):
            es = slice(r * PEER_NKEYS, (r + 1) * PEER_NKEYS)
            for lg in range(tb // 128):
                ls = slice(lg * 128, (lg + 1) * 128)
                gate = jnp.zeros(tile, _BF16)
                for h in range(PEER_HEADS):
                    cnt = pltpu.bitcast(jnp.broadcast_to(cnt_ref[h, r:r + 1, ls], words), _BF16)
                    pn = pltpu.bitcast(jnp.broadcast_to(p1n_ref[h, r:r + 1, ls], words), _BF16)
                    sel = jnp.minimum(jnp.maximum(cnt - rank_ref[h, :, ls], 0), pn)
                    gate = gate + sel * p2_ref[h, :, ls]
                act = jax.nn.gelu(st_scr[es, ls].astype(_BF16))
                h_scr[es, ls] = gate * act
    acc_scr[...] += _dot(vt_ref[0], h_scr[...])

    @pl.when(j == pl.num_programs(1) - 1)
    def _():
        x2 = x1_ref[...] + acc_scr[...].T
        y_ref[...] = _rms(x2, gfin_ref[...])


def _peer_dense(xn, u_bf, vt_bf, cnt, p1n, rank2, p2, x1, g_final, *, tb, ec):
    T, D = x1.shape
    rows = ec // PEER_NKEYS
    assert T % tb == 0 and PEER_NEXP % ec == 0 and rows % 8 == 0
    sel_all = pl.BlockSpec((PEER_HEADS, PEER_NKEYS, tb), lambda i, j: (0, 0, i))
    sel_rows = pl.BlockSpec((PEER_HEADS, rows, tb), lambda i, j: (0, j, i))
    tok = pl.BlockSpec((tb, D), lambda i, j: (i, 0))
    return pl.pallas_call(
        functools.partial(_peer_dense_kernel, tb=tb, rows=rows),
        out_shape=jax.ShapeDtypeStruct((T, D), _F32),
        grid=(T // tb, PEER_NEXP // ec),
        in_specs=[
            tok,
            pl.BlockSpec((ec, D), lambda i, j: (j, 0)),
            pl.BlockSpec((1, D, ec), lambda i, j: (j, 0, 0)),
            sel_rows, sel_rows, sel_all, sel_all,
            tok,
            pl.BlockSpec((1, D), lambda i, j: (0, 0)),
        ],
        out_specs=tok,
        scratch_shapes=[
            pltpu.VMEM((D, tb), _F32),
            pltpu.VMEM((ec, tb), _F32),
            pltpu.VMEM((ec, tb), _BF16),
        ],
        compiler_params=pltpu.CompilerParams(
            dimension_semantics=("arbitrary", "arbitrary"), vmem_limit_bytes=_VMEM_LIMIT),
        name="peer_dense",
    )(xn, u_bf, vt_bf, cnt, p1n, rank2, p2, x1, g_final.reshape(1, D))


def _mixer_weights(g_mix, w_in, b_in, g_mlh, g_sgu, w_s, b_s, w_out):
    nq = ML_HEADS * ML_DQK
    nv = ML_HEADS * ML_DV
    o_q, o_k, o_v = 0, nq, 2 * nq
    o_ig = o_v + nv
    o_fg = o_ig + ML_HEADS
    o_og = o_fg + ML_HEADS
    o_su = o_og + nv
    o_sv = o_su + D_MODEL
    o_xq = o_sv + D_MODEL
    o_gt = o_xq + D_MODEL
    main = lambda a: jnp.concatenate([a[..., o_q:o_ig], a[..., o_og:o_gt + 3 * D_MODEL]], axis=-1)
    w_if = w_in[:, o_ig:o_og]
    b_if = b_in[o_ig:o_og]
    return (
        g_mix.reshape(1, D_MODEL),
        main(w_in).astype(_BF16),
        main(b_in).reshape(1, _P_COLS),
        w_if.astype(_BF16),
        w_if.T.astype(_BF16),
        b_if.reshape(1, 8),
        b_if.reshape(8, 1),
        g_mlh.reshape(1, D_MODEL),
        g_sgu.reshape(1, D_MODEL),
        w_s,
        b_s.T,
        w_out.astype(_BF16),
    )


def kernel(x_prompt, x_sample, mem_prompt, cache_mem_k, cache_mem_v, state_mlstm_C, state_mlstm_n,
           state_mlstm_m, g_mix, w_in, b_in, g_mlh, g_sgu, w_s, b_s, g_mem, w_mk, w_mv, w_out, g_ffn,
           w_pq, k_sub1, k_sub2, u_exp, v_exp, g_final):
    depth = g_mix.shape[0]
    assert depth == 1
    l = 0
    Bp, Sp, D = x_prompt.shape
    Bs, Ss, _ = x_sample.shape

    wts = _mixer_weights(g_mix[l], w_in[l], b_in[l], g_mlh[l], g_sgu[l], w_s[l], b_s[l], w_out[l])

    mk, mv = _mem_kv(mem_prompt, g_mem[l], w_mk[l], w_mv[l])
    zc = jnp.zeros((Bp, ML_HEADS, ML_DQK, ML_DV), _F32)
    zn = jnp.zeros((Bp, ML_HEADS, ML_DQK), _F32)
    zm = jnp.zeros((Bp, ML_HEADS), _F32)
    x1p, cp, np_, mp, _ = _mixer(x_prompt, mk, mv, zc, zn, zm, wts, tc=256, want_vn=False)
    x1s, cs, ns, ms, vn = _mixer(
        x_sample, cache_mem_k[l].reshape(Bs, N_MEM, D), cache_mem_v[l].reshape(Bs, N_MEM, D),
        state_mlstm_C[l], state_mlstm_n[l], state_mlstm_m[l], wts, tc=Ss, want_vn=True)

    wpq_bf = w_pq[l].astype(_BF16)
    u_bf = u_exp[l].astype(_BF16)
    vt_bf = v_exp[l].astype(_BF16).reshape(PEER_NEXP // _PEER_EC, _PEER_EC, D).transpose(0, 2, 1)

    def peer(x1):
        B, S, _ = x1.shape
        x1 = x1.reshape(B * S, D)
        xn, cnt, p1n, rank2, p2 = _peer_select(x1, g_ffn[l], wpq_bf, k_sub1[l], k_sub2[l], tb=_PEER_TB)
        y = _peer_dense(xn, u_bf, vt_bf, cnt, p1n, rank2, p2, x1, g_final, tb=_PEER_TB, ec=_PEER_EC)
        return y.reshape(B, S, D)

    hs = (XA_HEADS, XA_DH)
    return (peer(x1p), peer(x1s), cp[None], np_[None], mp[None],
            mk.reshape(1, Bp, N_MEM, *hs), mv.reshape(1, Bp, N_MEM, *hs),
            cs[None], ns[None], ms[None], vn[None])
```

```python
import functools

import jax
import jax.numpy as jnp
from jax import lax
from jax.experimental import pallas as pl
from jax.experimental.pallas import tpu as pltpu

D_MODEL = 1024
EPS = 1e-6
CHUNK = 64
N_MEM = 256
ML_HEADS = 4
ML_DQK = 128
ML_DV = D_MODEL // ML_HEADS
SGU_CHUNK = 128
SGU_GROUPS = 4
SGU_GDIM = D_MODEL // SGU_GROUPS
XA_HEADS = 4
XA_DH = D_MODEL // XA_HEADS
PEER_HEADS = 8
PEER_NKEYS = 128
PEER_DQ = 256
PEER_TOPK = 16
PEER_NEXP = PEER_NKEYS * PEER_NKEYS

_Q0 = 0
_K0 = _Q0 + ML_HEADS * ML_DQK
_V0 = _K0 + ML_HEADS * ML_DQK
_OG0 = _V0 + ML_HEADS * ML_DV
_SU0 = _OG0 + D_MODEL
_SV0 = _SU0 + D_MODEL
_XQ0 = _SV0 + D_MODEL
_GT0 = _XQ0 + D_MODEL
_P_COLS = _GT0 + 3 * D_MODEL

_VMEM_LIMIT = 56 * 1024 * 1024

_PEER_TB = 512
_PEER_EC = 2048

_BF16 = jnp.bfloat16
_F32 = jnp.float32
_NEG_INF = float("-inf")


def _rms(xf, g):
    return xf * lax.rsqrt(jnp.mean(xf * xf, axis=-1, keepdims=True) + EPS) * g


def _dot(a, b):
    return jnp.dot(a, b, preferred_element_type=_F32)


def _dot_nt(a, b):
    return lax.dot_general(a, b, (((1,), (1,)), ((), ())), preferred_element_type=_F32)


def _bmm(a, b, ca, cb):
    return lax.dot_general(a, b, (((ca,), (cb,)), ((0,), (0,))), preferred_element_type=_F32)


def _mem_kv_kernel(mem_ref, g_ref, wk_ref, wv_ref, k_ref, v_ref):
    mn = _rms(mem_ref[0], g_ref[...]).astype(_BF16)
    k_ref[0] = _dot(mn, wk_ref[...])
    v_ref[0] = _dot(mn, wv_ref[...])


def _mem_kv(mem, g_mem, w_mk, w_mv):
    B = mem.shape[0]
    full = lambda b: (0, 0)
    return pl.pallas_call(
        _mem_kv_kernel,
        out_shape=(jax.ShapeDtypeStruct((B, N_MEM, D_MODEL), _F32),) * 2,
        grid=(B,),
        in_specs=[
            pl.BlockSpec((1, N_MEM, D_MODEL), lambda b: (b, 0, 0)),
            pl.BlockSpec((1, D_MODEL), full),
            pl.BlockSpec((D_MODEL, D_MODEL), full),
            pl.BlockSpec((D_MODEL, D_MODEL), full),
        ],
        out_specs=(pl.BlockSpec((1, N_MEM, D_MODEL), lambda b: (b, 0, 0)),) * 2,
        compiler_params=pltpu.CompilerParams(
            dimension_semantics=("arbitrary",), vmem_limit_bytes=_VMEM_LIMIT),
        name="mem_kv",
    )(mem, g_mem.reshape(1, D_MODEL), w_mk.astype(_BF16), w_mv.astype(_BF16))


def _mixer_kernel(x_ref, mk_ref, mv_ref, c0_ref, n0_ref, m0_ref, gmix_ref, win_ref, bin_ref,
                  wif_ref, wift_ref, bif_ref, bift_ref, gmlh_ref, gsgu_ref, ws_ref, bst_ref,
                  wout_ref,
                  x1_ref, c_out_ref, n_out_ref, m_out_ref, vn_ref,
                  p_scr, mrg_scr, c_scr, n_scr, m_scr, *, tc, sgu_len):
    ci = pl.program_id(1)
    nsub = tc // CHUNK

    @pl.when(ci == 0)
    def _():
        c_scr[...] = c0_ref[0]
        n_scr[...] = n0_ref[0]
        m_scr[...] = m0_ref[0]

    x = x_ref[0]
    xn = _rms(x, gmix_ref[...]).astype(_BF16)
    p_scr[...] = _dot(xn, win_ref[...]) + bin_ref[...]
    gif = _dot(xn, wif_ref[...]) + bif_ref[...]
    gift = _dot_nt(wift_ref[...], xn) + bift_ref[...]

    row = lax.broadcasted_iota(jnp.int32, (CHUNK, CHUNK), 0)
    col = lax.broadcasted_iota(jnp.int32, (CHUNK, CHUNK), 1)
    tril = row >= col
    tril_f = tril.astype(_F32)
    triu_f = (row <= col).astype(_F32)
    stack = lambda f: jnp.stack([f(h) for h in range(ML_HEADS)])
    for j in range(nsub):
        r0 = j * CHUNK
        rows = slice(r0, r0 + CHUNK)
        ig_c = gif[rows, 0:ML_HEADS]
        lf_c = jax.nn.log_sigmoid(gif[rows, ML_HEADS:2 * ML_HEADS])
        ig_r = gift[0:ML_HEADS, rows]
        lf_r = jax.nn.log_sigmoid(gift[ML_HEADS:2 * ML_HEADS, rows])
        b_c = jnp.dot(tril_f, lf_c, precision=lax.Precision.HIGHEST,
                      preferred_element_type=_F32)
        b_r = jnp.dot(lf_r, triu_f, precision=lax.Precision.HIGHEST,
                      preferred_element_type=_F32)
        q = stack(lambda h: p_scr[rows, _Q0 + h * ML_DQK:_Q0 + (h + 1) * ML_DQK]).astype(_BF16)
        k = stack(lambda h: p_scr[rows, _K0 + h * ML_DQK:_K0 + (h + 1) * ML_DQK]) * (ML_DQK ** -0.5)
        v = stack(lambda h: p_scr[rows, _V0 + h * ML_DV:_V0 + (h + 1) * ML_DV]).astype(_BF16)
        c_st = c_scr[...]
        n_st = stack(lambda h: n_scr[h:h + 1, :])
        m_st = stack(lambda h: m_scr[h:h + 1, 0:1])
        bc = stack(lambda h: b_c[:, h:h + 1])
        br = stack(lambda h: b_r[h:h + 1, :])
        ig_row = stack(lambda h: ig_r[h:h + 1, :])
        ig_col = stack(lambda h: ig_c[:, h:h + 1])
        logd = jnp.where(tril[None], bc - br + ig_row, _NEG_INF)
        log_prev = bc + m_st
        m_t = jnp.maximum(log_prev, jnp.max(logd, axis=2, keepdims=True))
        a = _bmm(q, k.astype(_BF16), 2, 2) * jnp.exp(logd - m_t)
        wp = jnp.exp(log_prev - m_t)
        num = _bmm(a.astype(_BF16), v, 2, 1) + wp * _bmm(q, c_st.astype(_BF16), 2, 1)
        qn = jnp.sum(q.astype(_F32) * n_st, axis=2, keepdims=True)
        den = jnp.sum(a, axis=2, keepdims=True) + wp * qn
        hh = num / jnp.maximum(jnp.abs(den), jnp.exp(-m_t))
        hn = hh * lax.rsqrt(jnp.mean(hh * hh, axis=2, keepdims=True) + EPS)
        for h in range(ML_HEADS):
            sl = slice(h * ML_DV, (h + 1) * ML_DV)
            og = p_scr[rows, _OG0 + h * ML_DV:_OG0 + (h + 1) * ML_DV]
            h_a = jax.nn.sigmoid(og) * hn[h] * gmlh_ref[:, sl]
            g0 = jax.nn.sigmoid(p_scr[rows, _GT0 + h * ML_DV:_GT0 + (h + 1) * ML_DV])
            mrg_scr[rows, sl] = g0 * h_a
        b_last = bc[:, CHUNK - 1:CHUNK, :]
        log_in = b_last - bc + ig_col
        m_new = jnp.maximum(b_last + m_st, jnp.max(log_in, axis=1, keepdims=True))
        wi = jnp.exp(log_in - m_new)
        wc = jnp.exp(b_last + m_st - m_new)
        kw = wi * k
        c_scr[...] = wc * c_st + _bmm(kw.astype(_BF16), v, 1, 1)
        n_new = wc * n_st + jnp.sum(kw, axis=1, keepdims=True)
        for h in range(ML_HEADS):
            n_scr[h:h + 1, :] = n_new[h]
            m_scr[h:h + 1, :] = jnp.broadcast_to(m_new[h], (1, 128))

    sv = jax.nn.gelu(p_scr[:, _SV0:_SV0 + D_MODEL])
    vn = _rms(sv, gsgu_ref[...])
    if vn_ref is not None:
        vn_ref[0] = vn
    vnb = vn.astype(_BF16)
    rs = lax.broadcasted_iota(jnp.int32, (sgu_len, sgu_len), 0)
    cs = lax.broadcasted_iota(jnp.int32, (sgu_len, sgu_len), 1)
    for g in range(SGU_GROUPS):
        wsg = jnp.where(rs >= cs, ws_ref[g, 0:sgu_len, 0:sgu_len], 0.0).astype(_BF16)
        bsg = bst_ref[0:sgu_len, g:g + 1]
        gl = slice(g * SGU_GDIM, (g + 1) * SGU_GDIM)
        for c in range(tc // sgu_len):
            rows = slice(c * sgu_len, (c + 1) * sgu_len)
            mix = _dot(wsg, vnb[rows, gl]) + bsg
            u = jax.nn.gelu(p_scr[rows, _SU0 + g * SGU_GDIM:_SU0 + (g + 1) * SGU_GDIM])
            g1 = jax.nn.sigmoid(
                p_scr[rows, _GT0 + D_MODEL + g * SGU_GDIM:_GT0 + D_MODEL + (g + 1) * SGU_GDIM])
            mrg_scr[rows, gl] += g1 * (u * mix)

    for h in range(XA_HEADS):
        hl = slice(h * XA_DH, (h + 1) * XA_DH)
        xq = p_scr[:, _XQ0 + h * XA_DH:_XQ0 + (h + 1) * XA_DH].astype(_BF16)
        sc = _dot_nt(xq, mk_ref[0, :, hl].astype(_BF16)) * (XA_DH ** -0.5)
        sc = sc - jnp.max(sc, axis=1, keepdims=True)
        e = jnp.exp(sc)
        att = e / jnp.sum(e, axis=1, keepdims=True)
        h_c = _dot(att.astype(_BF16), mv_ref[0, :, hl].astype(_BF16))
        g2 = jax.nn.sigmoid(p_scr[:, _GT0 + 2 * D_MODEL + h * XA_DH:_GT0 + 2 * D_MODEL + (h + 1) * XA_DH])
        mrg_scr[:, hl] += g2 * h_c

    x1_ref[0] = x + _dot(mrg_scr[...].astype(_BF16), wout_ref[...])

    @pl.when(ci == pl.num_programs(1) - 1)
    def _():
        c_out_ref[0] = c_scr[...]
        n_out_ref[0] = n_scr[...]
        m_out_ref[0] = m_scr[...]


def _mixer(x, mem_k, mem_v, c0, n0, m0, wts, *, tc, want_vn):
    B, S, D = x.shape
    sgu_len = min(S, SGU_CHUNK)
    assert S % tc == 0 and tc % CHUNK == 0 and tc % sgu_len == 0
    nchunks = S // tc
    m0p = jnp.broadcast_to(m0[:, :, None], (B, ML_HEADS, 128))
    m0p = jnp.concatenate([m0p, jnp.zeros((B, 8 - ML_HEADS, 128), _F32)], axis=1)

    def body(*refs):
        ins, rest = refs[:18], refs[18:]
        if want_vn:
            outs, scr = rest[:5], rest[5:]
        else:
            outs, scr = rest[:4] + (None,), rest[4:]
        _mixer_kernel(*ins, *outs, *scr, tc=tc, sgu_len=sgu_len)

    const2 = lambda b, c: (0, 0)
    const3 = lambda b, c: (0, 0, 0)
    per_b3 = lambda b, c: (b, 0, 0)
    per_b4 = lambda b, c: (b, 0, 0, 0)
    once = dict(pipeline_mode=pl.Buffered(1))
    in_specs = [
        pl.BlockSpec((1, tc, D), lambda b, c: (b, c, 0)),
        pl.BlockSpec((1, N_MEM, D), per_b3),
        pl.BlockSpec((1, N_MEM, D), per_b3),
        pl.BlockSpec((1, ML_HEADS, ML_DQK, ML_DV), per_b4),
        pl.BlockSpec((1, ML_HEADS, ML_DQK), per_b3),
        pl.BlockSpec((1, 8, 128), per_b3),
        pl.BlockSpec((1, D), const2, **once),
        pl.BlockSpec((D, _P_COLS), const2, **once),
        pl.BlockSpec((1, _P_COLS), const2, **once),
        pl.BlockSpec((D, 8), const2, **once),
        pl.BlockSpec((8, D), const2, **once),
        pl.BlockSpec((1, 8), const2, **once),
        pl.BlockSpec((8, 1), const2, **once),
        pl.BlockSpec((1, D), const2, **once),
        pl.BlockSpec((1, D), const2, **once),
        pl.BlockSpec((SGU_GROUPS, SGU_CHUNK, SGU_CHUNK), const3, **once),
        pl.BlockSpec((SGU_CHUNK, SGU_GROUPS), const2, **once),
        pl.BlockSpec((D, D), const2, **once),
    ]
    out_shape = [
        jax.ShapeDtypeStruct((B, S, D), _F32),
        jax.ShapeDtypeStruct((B, ML_HEADS, ML_DQK, ML_DV), _F32),
        jax.ShapeDtypeStruct((B, ML_HEADS, ML_DQK), _F32),
        jax.ShapeDtypeStruct((B, 8, 128), _F32),
    ]
    out_specs = [
        pl.BlockSpec((1, tc, D), lambda b, c: (b, c, 0)),
        pl.BlockSpec((1, ML_HEADS, ML_DQK, ML_DV), per_b4),
        pl.BlockSpec((1, ML_HEADS, ML_DQK), per_b3),
        pl.BlockSpec((1, 8, 128), per_b3),
    ]
    if want_vn:
        out_shape.append(jax.ShapeDtypeStruct((B, S, D), _F32))
        out_specs.append(pl.BlockSpec((1, tc, D), lambda b, c: (b, c, 0)))
    outs = pl.pallas_call(
        body,
        out_shape=tuple(out_shape),
        grid=(B, nchunks),
        in_specs=in_specs,
        out_specs=tuple(out_specs),
        scratch_shapes=[
            pltpu.VMEM((tc, _P_COLS), _F32),
            pltpu.VMEM((tc, D), _F32),
            pltpu.VMEM((ML_HEADS, ML_DQK, ML_DV), _F32),
            pltpu.VMEM((ML_HEADS, ML_DQK), _F32),
            pltpu.VMEM((8, 128), _F32),
        ],
        compiler_params=pltpu.CompilerParams(
            dimension_semantics=("arbitrary", "arbitrary"), vmem_limit_bytes=_VMEM_LIMIT),
        name="mixer",
    )(x, mem_k, mem_v, c0, n0, m0p, *wts)
    x1, c1, n1, m1p = outs[:4]
    vn = outs[4] if want_vn else None
    return x1, c1, n1, m1p[:, :ML_HEADS, 0], vn


def _batcher_pairs(n):
    pairs = []
    p = 1
    while p < n:
        k = p
        while k >= 1:
            for j in range(k % p, n - k, 2 * k):
                for i in range(min(k, n - j - k)):
                    if (i + j) // (2 * p) == (i + j + k) // (2 * p):
                        pairs.append((i + j, i + j + k))
            k //= 2
        p *= 2
    return pairs


_SORT16 = _batcher_pairs(PEER_NKEYS // 8)
_SEL_UNROLL = 2


def _sorted_columns(c):
    cols = [c[8 * k:8 * k + 8, :] for k in range(PEER_NKEYS // 8)]
    for i, j in _SORT16:
        cols[i], cols[j] = jnp.maximum(cols[i], cols[j]), jnp.minimum(cols[i], cols[j])
    return cols


def _pop_top(cols, out_scr):
    for r in range(PEER_TOPK):
        mx = jnp.max(cols[0], axis=0, keepdims=True)
        out_scr[r:r + 1, :] = mx
        left = PEER_TOPK - 1 - r
        if left:
            eq = cols[0] == mx
            for k in range(left):
                cols[k] = jnp.where(eq, cols[k + 1], cols[k])


def _bf16_pair(x):
    u = pltpu.bitcast(x, jnp.uint32)
    u = u + jnp.uint32(0x7FFF) + ((u >> 16) & jnp.uint32(1))
    hi = u & jnp.uint32(0xFFFF0000)
    return hi | (hi >> 16)


def _peer_select_kernel(x1_ref, gffn_ref, wpq_ref, k1_ref, k2_ref,
                        xn_ref, cnt_ref, p1n_ref, rank_ref, p2_ref,
                        q_scr, c1_scr, c2_scr, v1_scr, v2_scr, n_scr, *, tb):
    xn = _rms(x1_ref[...], gffn_ref[...]).astype(_BF16)
    xn_ref[...] = xn
    q_scr[...] = _dot(xn, wpq_ref[...])
    k1 = k1_ref[...].astype(_BF16)
    k2 = k2_ref[...].astype(_BF16)
    half = PEER_DQ // 2
    K = PEER_TOPK

    def lane_group(h, ls):
        c1 = c1_scr[:, ls]
        c2 = c2_scr[:, ls]
        _pop_top(_sorted_columns(c1), v1_scr)
        _pop_top(_sorted_columns(c2), v2_scr)
        v1 = v1_scr[...]
        lists = [v1_scr[0:8, :] + v2_scr[b:b + 1, :] for b in range(8)]
        ea = v1_scr[8:16, :] + v2_scr[0:1, :]
        eb = v1_scr[0:1, :] + v2_scr[8:16, :]
        cands = lists + [ea, eb]
        tau = None
        for r in range(K):
            tau = jnp.max(jnp.maximum(jnp.maximum(lists[0], ea), eb), axis=0, keepdims=True)
            left = K - 1 - r
            if left:
                eq = lists[0] == tau
                for k in range(min(left, 7)):
                    lists[k] = jnp.where(eq, lists[k + 1], lists[k])
                if left >= 8:
                    lists[7] = jnp.where(eq, _NEG_INF, lists[7])
                ea = jnp.where(ea == tau, _NEG_INF, ea)
                eb = jnp.where(eb == tau, _NEG_INF, eb)
        z = sum(jnp.sum(jnp.where(c >= tau, jnp.exp(c), 0.0), axis=0, keepdims=True) for c in cands)
        n = jnp.zeros(v1.shape, _F32)
        for b in range(K):
            n = jnp.where(v1 + v2_scr[b:b + 1, :] >= tau, float(b + 1), n)
        n_scr[...] = n
        cnt = jnp.zeros(c1.shape, _F32)
        rank2 = jnp.full(c2.shape, float(K), _F32)
        for a in range(K - 1, -1, -1):
            cnt = jnp.where(c1 >= v1_scr[a:a + 1, :], n_scr[a:a + 1, :], cnt)
            rank2 = jnp.where(c2 >= v2_scr[a:a + 1, :], float(a), rank2)
        cnt_ref[h, :, ls] = _bf16_pair(cnt)
        p1n_ref[h, :, ls] = _bf16_pair(jnp.exp(c1) / z)
        rank_ref[h, :, ls] = rank2.astype(_BF16)
        p2_ref[h, :, ls] = jnp.exp(c2).astype(_BF16)

    def head(h, carry):
        q0 = pl.multiple_of(h * PEER_DQ, PEER_DQ)
        s1 = _dot_nt(k1, q_scr[:, pl.ds(q0, half)].astype(_BF16))
        s2 = _dot_nt(k2, q_scr[:, pl.ds(q0 + half, half)].astype(_BF16))
        c1_scr[...] = s1 - jnp.max(s1, axis=0, keepdims=True)
        c2_scr[...] = s2 - jnp.max(s2, axis=0, keepdims=True)

        def lane_groups(i, carry):
            for u in range(_SEL_UNROLL):
                lane_group(h, pl.ds(pl.multiple_of((i * _SEL_UNROLL + u) * 128, 128), 128))
            return carry

        lax.fori_loop(0, tb // (128 * _SEL_UNROLL), lane_groups, 0)
        return carry

    lax.fori_loop(0, PEER_HEADS, head, 0)


def _peer_select(x1, g_ffn, wpq_bf, k_sub1, k_sub2, *, tb):
    T, D = x1.shape
    assert T % tb == 0 and tb % (128 * _SEL_UNROLL) == 0
    sel = lambda dt: jax.ShapeDtypeStruct((PEER_HEADS, PEER_NKEYS, T), dt)
    sel_spec = pl.BlockSpec((PEER_HEADS, PEER_NKEYS, tb), lambda i: (0, 0, i))
    const2 = lambda i: (0, 0)
    return pl.pallas_call(
        functools.partial(_peer_select_kernel, tb=tb),
        out_shape=(jax.ShapeDtypeStruct((T, D), _BF16), sel(jnp.uint32), sel(jnp.uint32), sel(_BF16), sel(_BF16)),
        grid=(T // tb,),
        in_specs=[
            pl.BlockSpec((tb, D), lambda i: (i, 0)),
            pl.BlockSpec((1, D), const2),
            pl.BlockSpec((D, PEER_HEADS * PEER_DQ), const2),
            pl.BlockSpec((PEER_NKEYS, PEER_DQ // 2), const2),
            pl.BlockSpec((PEER_NKEYS, PEER_DQ // 2), const2),
        ],
        out_specs=(pl.BlockSpec((tb, D), lambda i: (i, 0)),) + (sel_spec,) * 4,
        scratch_shapes=[
            pltpu.VMEM((tb, PEER_HEADS * PEER_DQ), _F32),
            pltpu.VMEM((PEER_NKEYS, tb), _F32),
            pltpu.VMEM((PEER_NKEYS, tb), _F32),
            pltpu.VMEM((PEER_TOPK, 128), _F32),
            pltpu.VMEM((PEER_TOPK, 128), _F32),
            pltpu.VMEM((PEER_TOPK, 128), _F32),
        ],
        compiler_params=pltpu.CompilerParams(
            dimension_semantics=("arbitrary",), vmem_limit_bytes=_VMEM_LIMIT),
        name="peer_select",
    )(x1, g_ffn.reshape(1, D), wpq_bf, k_sub1, k_sub2)


def _peer_dense_kernel(xn_ref, u_ref, vtp_ref, vtl_ref, cnt_ref, p1n_ref, rank_ref, p2_ref, x1_ref,
                       gfin_ref, y_ref, acc_scr, st_scr, h_scr, *, tb, rows):
    j = pl.program_id(1)
    cur = lax.rem(j, 2)

    @pl.when(j == 0)
    def _():
        acc_scr[...] = jnp.zeros_like(acc_scr)
        h_scr[1] = jnp.zeros(h_scr.shape[1:], _BF16)

    tile = (PEER_NKEYS, 128)
    words = (PEER_NKEYS // 2, 128)
    rb = 2
    for b in range(rows // rb):
        bs = slice(b * rb * PEER_NKEYS, (b + 1) * rb * PEER_NKEYS)
        st_scr[bs, :] = _dot_nt(u_ref[bs, :], xn_ref[...])
        for r in range(b * rb, (b + 1) * rb):
            es = slice(r * PEER_NKEYS, (r + 1) * PEER_NKEYS)
            for lg in range(tb // 128):
                ls = slice(lg * 128, (lg + 1) * 128)
                gate = jnp.zeros(tile, _BF16)
                for h in range(PEER_HEADS):
                    cnt = pltpu.bitcast(jnp.broadcast_to(cnt_ref[h, r:r + 1, ls], words), _BF16)
                    pn = pltpu.bitcast(jnp.broadcast_to(p1n_ref[h, r:r + 1, ls], words), _BF16)
                    sel = jnp.minimum(jnp.maximum(cnt - rank_ref[h, :, ls], 0), pn)
                    gate = gate + sel * p2_ref[h, :, ls]
                act = jax.nn.gelu(st_scr[es, ls].astype(_BF16))
                h_scr[cur, es, ls] = gate * act
    acc_scr[...] += _dot(vtp_ref[0], h_scr[1 - cur])

    @pl.when(j == pl.num_programs(1) - 1)
    def _():
        acc = acc_scr[...] + _dot(vtl_ref[0], h_scr[cur])
        x2 = x1_ref[...] + acc.T
        y_ref[...] = _rms(x2, gfin_ref[...])


def _peer_dense(xn, u_bf, vt_bf, cnt, p1n, rank2, p2, x1, g_final, *, tb, ec):
    T, D = x1.shape
    rows = ec // PEER_NKEYS
    nchunks = PEER_NEXP // ec
    assert T % tb == 0 and PEER_NEXP % ec == 0 and rows % 8 == 0
    sel_all = pl.BlockSpec((PEER_HEADS, PEER_NKEYS, tb), lambda i, j: (0, 0, i))
    sel_rows = pl.BlockSpec((PEER_HEADS, rows, tb), lambda i, j: (0, j, i))
    tok = pl.BlockSpec((tb, D), lambda i, j: (i, 0))
    return pl.pallas_call(
        functools.partial(_peer_dense_kernel, tb=tb, rows=rows),
        out_shape=jax.ShapeDtypeStruct((T, D), _F32),
        grid=(T // tb, nchunks),
        in_specs=[
            tok,
            pl.BlockSpec((ec, D), lambda i, j: (j, 0)),
            pl.BlockSpec((1, D, ec), lambda i, j: (jnp.maximum(j - 1, 0), 0, 0)),
            pl.BlockSpec((1, D, ec), lambda i, j: (nchunks - 1, 0, 0), pipeline_mode=pl.Buffered(1)),
            sel_rows, sel_rows, sel_all, sel_all,
            tok,
            pl.BlockSpec((1, D), lambda i, j: (0, 0)),
        ],
        out_specs=tok,
        scratch_shapes=[
            pltpu.VMEM((D, tb), _F32),
            pltpu.VMEM((ec, tb), _F32),
            pltpu.VMEM((2, ec, tb), _BF16),
        ],
        compiler_params=pltpu.CompilerParams(
            dimension_semantics=("arbitrary", "arbitrary"), vmem_limit_bytes=_VMEM_LIMIT),
        name="peer_dense",
    )(xn, u_bf, vt_bf, vt_bf, cnt, p1n, rank2, p2, x1, g_final.reshape(1, D))


def _mixer_weights(g_mix, w_in, b_in, g_mlh, g_sgu, w_s, b_s, w_out):
    nq = ML_HEADS * ML_DQK
    nv = ML_HEADS * ML_DV
    o_q, o_k, o_v = 0, nq, 2 * nq
    o_ig = o_v + nv
    o_fg = o_ig + ML_HEADS
    o_og = o_fg + ML_HEADS
    o_su = o_og + nv
    o_sv = o_su + D_MODEL
    o_xq = o_sv + D_MODEL
    o_gt = o_xq + D_MODEL
    main = lambda a: jnp.concatenate([a[..., o_q:o_ig], a[..., o_og:o_gt + 3 * D_MODEL]], axis=-1)
    w_if = w_in[:, o_ig:o_og]
    b_if = b_in[o_ig:o_og]
    return (
        g_mix.reshape(1, D_MODEL),
        main(w_in).astype(_BF16),
        main(b_in).reshape(1, _P_COLS),
        w_if.astype(_BF16),
        w_if.T.astype(_BF16),
        b_if.reshape(1, 8),
        b_if.reshape(8, 1),
        g_mlh.reshape(1, D_MODEL),
        g_sgu.reshape(1, D_MODEL),
        w_s,
        b_s.T,
        w_out.astype(_BF16),
    )


def kernel(x_prompt, x_sample, mem_prompt, cache_mem_k, cache_mem_v, state_mlstm_C, state_mlstm_n,
           state_mlstm_m, g_mix, w_in, b_in, g_mlh, g_sgu, w_s, b_s, g_mem, w_mk, w_mv, w_out, g_ffn,
           w_pq, k_sub1, k_sub2, u_exp, v_exp, g_final):
    depth = g_mix.shape[0]
    assert depth == 1
    l = 0
    Bp, Sp, D = x_prompt.shape
    Bs, Ss, _ = x_sample.shape

    wts = _mixer_weights(g_mix[l], w_in[l], b_in[l], g_mlh[l], g_sgu[l], w_s[l], b_s[l], w_out[l])

    mk, mv = _mem_kv(mem_prompt, g_mem[l], w_mk[l], w_mv[l])
    zc = jnp.zeros((Bp, ML_HEADS, ML_DQK, ML_DV), _F32)
    zn = jnp.zeros((Bp, ML_HEADS, ML_DQK), _F32)
    zm = jnp.zeros((Bp, ML_HEADS), _F32)
    x1p, cp, np_, mp, _ = _mixer(x_prompt, mk, mv, zc, zn, zm, wts, tc=256, want_vn=False)
    x1s, cs, ns, ms, vn = _mixer(
        x_sample, cache_mem_k[l].reshape(Bs, N_MEM, D), cache_mem_v[l].reshape(Bs, N_MEM, D),
        state_mlstm_C[l], state_mlstm_n[l], state_mlstm_m[l], wts, tc=Ss, want_vn=True)

    wpq_bf = w_pq[l].astype(_BF16)
    u_bf = u_exp[l].astype(_BF16)
    vt_bf = v_exp[l].astype(_BF16).reshape(PEER_NEXP // _PEER_EC, _PEER_EC, D).transpose(0, 2, 1)

    def peer(x1):
        B, S, _ = x1.shape
        x1 = x1.reshape(B * S, D)
        xn, cnt, p1n, rank2, p2 = _peer_select(x1, g_ffn[l], wpq_bf, k_sub1[l], k_sub2[l], tb=_PEER_TB)
        y = _peer_dense(xn, u_bf, vt_bf, cnt, p1n, rank2, p2, x1, g_final, tb=_PEER_TB, ec=_PEER_EC)
        return y.reshape(B, S, D)

    hs = (XA_HEADS, XA_DH)
    return (peer(x1p), peer(x1s), cp[None], np_[None], mp[None],
            mk.reshape(1, Bp, N_MEM, *hs), mv.reshape(1, Bp, N_MEM, *hs),
            cs[None], ns[None], ms[None], vn[None])
```

```python
import functools

import jax
import jax.numpy as jnp
from jax import lax
from jax.experimental import pallas as pl
from jax.experimental.pallas import tpu as pltpu

D_MODEL = 1024
EPS = 1e-6
CHUNK = 64
N_MEM = 256
ML_HEADS = 4
ML_DQK = 128
ML_DV = D_MODEL // ML_HEADS
SGU_CHUNK = 128
SGU_GROUPS = 4
SGU_GDIM = D_MODEL // SGU_GROUPS
XA_HEADS = 4
XA_DH = D_MODEL // XA_HEADS
PEER_HEADS = 8
PEER_NKEYS = 128
PEER_DQ = 256
PEER_TOPK = 16
PEER_NEXP = PEER_NKEYS * PEER_NKEYS

_SEG_A = ML_HEADS * (2 * ML_DQK + ML_DV)
_SEG_B = 2 * D_MODEL
_SEG_C = 3 * D_MODEL
_SEG_D = 2 * D_MODEL
_SEG0 = (0, _SEG_A, _SEG_A + _SEG_B, _SEG_A + _SEG_B + _SEG_C)
_P_COLS = _SEG_A + _SEG_B + _SEG_C + _SEG_D
_K0 = ML_HEADS * ML_DQK
_V0 = 2 * ML_HEADS * ML_DQK

_VMEM_LIMIT = 56 * 1024 * 1024

_PEER_TB = 512
_PEER_EC = 2048

_BF16 = jnp.bfloat16
_F32 = jnp.float32
_NEG_INF = float("-inf")


def _rms(xf, g):
    return xf * lax.rsqrt(jnp.mean(xf * xf, axis=-1, keepdims=True) + EPS) * g


def _dot(a, b):
    return jnp.dot(a, b, preferred_element_type=_F32)


def _dot_nt(a, b):
    return lax.dot_general(a, b, (((1,), (1,)), ((), ())), preferred_element_type=_F32)


def _bmm(a, b, ca, cb):
    return lax.dot_general(a, b, (((ca,), (cb,)), ((0,), (0,))), preferred_element_type=_F32)


def _mem_kv_kernel(mem_ref, g_ref, wk_ref, wv_ref, k_ref, v_ref):
    mn = _rms(mem_ref[0], g_ref[...]).astype(_BF16)
    k_ref[0] = _dot(mn, wk_ref[...])
    v_ref[0] = _dot(mn, wv_ref[...])


def _mem_kv(mem, g_mem, w_mk, w_mv):
    B = mem.shape[0]
    full = lambda b: (0, 0)
    return pl.pallas_call(
        _mem_kv_kernel,
        out_shape=(jax.ShapeDtypeStruct((B, N_MEM, D_MODEL), _F32),) * 2,
        grid=(B,),
        in_specs=[
            pl.BlockSpec((1, N_MEM, D_MODEL), lambda b: (b, 0, 0)),
            pl.BlockSpec((1, D_MODEL), full),
            pl.BlockSpec((D_MODEL, D_MODEL), full),
            pl.BlockSpec((D_MODEL, D_MODEL), full),
        ],
        out_specs=(pl.BlockSpec((1, N_MEM, D_MODEL), lambda b: (b, 0, 0)),) * 2,
        compiler_params=pltpu.CompilerParams(
            dimension_semantics=("arbitrary",), vmem_limit_bytes=_VMEM_LIMIT),
        name="mem_kv",
    )(mem, g_mem.reshape(1, D_MODEL), w_mk.astype(_BF16), w_mv.astype(_BF16))


def _mixer_kernel(x_ref, mk_ref, mv_ref, c0_ref, n0_ref, m0_ref, gmix_ref, win_ref, bin_ref,
                  wif_ref, wift_ref, bif_ref, bift_ref, gmlh_ref, gsgu_ref, ws_ref, bst_ref,
                  wout_ref,
                  x1_ref, c_out_ref, n_out_ref, m_out_ref, vn_ref,
                  pa_scr, pb_scr, pc_scr, pd_scr, mrg_scr, c_scr, n_scr, m_scr, *, tc, sgu_len):
    ci = pl.program_id(1)
    nsub = tc // CHUNK

    @pl.when(ci == 0)
    def _():
        c_scr[...] = c0_ref[0]
        n_scr[...] = n0_ref[0]
        m_scr[...] = m0_ref[0]

    x = x_ref[0]
    xn = _rms(x, gmix_ref[...]).astype(_BF16)

    def project(seg, out_scr):
        cols = slice(_SEG0[seg], _SEG0[seg] + out_scr.shape[1])
        out_scr[...] = _dot(xn, win_ref[:, cols]) + bin_ref[:, cols]

    project(0, pa_scr)
    gif = _dot(xn, wif_ref[...]) + bif_ref[...]
    gift = _dot_nt(wift_ref[...], xn) + bift_ref[...]

    row = lax.broadcasted_iota(jnp.int32, (CHUNK, CHUNK), 0)
    col = lax.broadcasted_iota(jnp.int32, (CHUNK, CHUNK), 1)
    tril = row >= col
    tril_f = tril.astype(_F32)
    triu_f = (row <= col).astype(_F32)
    stack = lambda f: jnp.stack([f(h) for h in range(ML_HEADS)])
    for j in range(nsub):
        r0 = j * CHUNK
        rows = slice(r0, r0 + CHUNK)
        ig_c = gif[rows, 0:ML_HEADS]
        lf_c = jax.nn.log_sigmoid(gif[rows, ML_HEADS:2 * ML_HEADS])
        ig_r = gift[0:ML_HEADS, rows]
        lf_r = jax.nn.log_sigmoid(gift[ML_HEADS:2 * ML_HEADS, rows])
        b_c = jnp.dot(tril_f, lf_c, precision=lax.Precision.HIGHEST,
                      preferred_element_type=_F32)
        b_r = jnp.dot(lf_r, triu_f, precision=lax.Precision.HIGHEST,
                      preferred_element_type=_F32)
        q = stack(lambda h: pa_scr[rows, h * ML_DQK:(h + 1) * ML_DQK]).astype(_BF16)
        k = stack(lambda h: pa_scr[rows, _K0 + h * ML_DQK:_K0 + (h + 1) * ML_DQK]) * (ML_DQK ** -0.5)
        v = stack(lambda h: pa_scr[rows, _V0 + h * ML_DV:_V0 + (h + 1) * ML_DV]).astype(_BF16)
        c_st = c_scr[...]
        n_st = stack(lambda h: n_scr[h:h + 1, :])
        m_st = stack(lambda h: m_scr[h:h + 1, 0:1])
        bc = stack(lambda h: b_c[:, h:h + 1])
        br = stack(lambda h: b_r[h:h + 1, :])
        ig_row = stack(lambda h: ig_r[h:h + 1, :])
        ig_col = stack(lambda h: ig_c[:, h:h + 1])
        logd = jnp.where(tril[None], bc - br + ig_row, _NEG_INF)
        log_prev = bc + m_st
        m_t = jnp.maximum(log_prev, jnp.max(logd, axis=2, keepdims=True))
        a = _bmm(q, k.astype(_BF16), 2, 2) * jnp.exp(logd - m_t)
        wp = jnp.exp(log_prev - m_t)
        num = _bmm(a.astype(_BF16), v, 2, 1) + wp * _bmm(q, c_st.astype(_BF16), 2, 1)
        qn = jnp.sum(q.astype(_F32) * n_st, axis=2, keepdims=True)
        den = jnp.sum(a, axis=2, keepdims=True) + wp * qn
        hh = num / jnp.maximum(jnp.abs(den), jnp.exp(-m_t))
        hn = hh * lax.rsqrt(jnp.mean(hh * hh, axis=2, keepdims=True) + EPS)
        for h in range(ML_HEADS):
            mrg_scr[rows, h * ML_DV:(h + 1) * ML_DV] = hn[h]
        b_last = bc[:, CHUNK - 1:CHUNK, :]
        log_in = b_last - bc + ig_col
        m_new = jnp.maximum(b_last + m_st, jnp.max(log_in, axis=1, keepdims=True))
        wi = jnp.exp(log_in - m_new)
        wc = jnp.exp(b_last + m_st - m_new)
        kw = wi * k
        c_scr[...] = wc * c_st + _bmm(kw.astype(_BF16), v, 1, 1)
        n_new = wc * n_st + jnp.sum(kw, axis=1, keepdims=True)
        for h in range(ML_HEADS):
            n_scr[h:h + 1, :] = n_new[h]
            m_scr[h:h + 1, :] = jnp.broadcast_to(m_new[h], (1, 128))

    project(1, pb_scr)
    h_a = jax.nn.sigmoid(pb_scr[:, 0:D_MODEL]) * mrg_scr[...] * gmlh_ref[...]
    mrg_scr[...] = jax.nn.sigmoid(pb_scr[:, D_MODEL:2 * D_MODEL]) * h_a

    project(2, pc_scr)
    sv = jax.nn.gelu(pc_scr[:, D_MODEL:2 * D_MODEL])
    vn = _rms(sv, gsgu_ref[...])
    if vn_ref is not None:
        vn_ref[0] = vn
    vnb = vn.astype(_BF16)
    rs = lax.broadcasted_iota(jnp.int32, (sgu_len, sgu_len), 0)
    cs = lax.broadcasted_iota(jnp.int32, (sgu_len, sgu_len), 1)
    for g in range(SGU_GROUPS):
        wsg = jnp.where(rs >= cs, ws_ref[g, 0:sgu_len, 0:sgu_len], 0.0).astype(_BF16)
        bsg = bst_ref[0:sgu_len, g:g + 1]
        gl = slice(g * SGU_GDIM, (g + 1) * SGU_GDIM)
        for c in range(tc // sgu_len):
            rows = slice(c * sgu_len, (c + 1) * sgu_len)
            mix = _dot(wsg, vnb[rows, gl]) + bsg
            u = jax.nn.gelu(pc_scr[rows, g * SGU_GDIM:(g + 1) * SGU_GDIM])
            g1 = jax.nn.sigmoid(pc_scr[rows, 2 * D_MODEL + g * SGU_GDIM:2 * D_MODEL + (g + 1) * SGU_GDIM])
            mrg_scr[rows, gl] += g1 * (u * mix)

    project(3, pd_scr)
    for h in range(XA_HEADS):
        hl = slice(h * XA_DH, (h + 1) * XA_DH)
        xq = pd_scr[:, h * XA_DH:(h + 1) * XA_DH].astype(_BF16)
        sc = _dot_nt(xq, mk_ref[0, :, hl].astype(_BF16)) * (XA_DH ** -0.5)
        sc = sc - jnp.max(sc, axis=1, keepdims=True)
        e = jnp.exp(sc)
        att = e / jnp.sum(e, axis=1, keepdims=True)
        h_c = _dot(att.astype(_BF16), mv_ref[0, :, hl].astype(_BF16))
        g2 = jax.nn.sigmoid(pd_scr[:, D_MODEL + h * XA_DH:D_MODEL + (h + 1) * XA_DH])
        mrg_scr[:, hl] += g2 * h_c

    x1_ref[0] = x + _dot(mrg_scr[...].astype(_BF16), wout_ref[...])

    @pl.when(ci == pl.num_programs(1) - 1)
    def _():
        c_out_ref[0] = c_scr[...]
        n_out_ref[0] = n_scr[...]
        m_out_ref[0] = m_scr[...]


def _mixer(x, mem_k, mem_v, c0, n0, m0, wts, *, tc, want_vn):
    B, S, D = x.shape
    sgu_len = min(S, SGU_CHUNK)
    assert S % tc == 0 and tc % CHUNK == 0 and tc % sgu_len == 0
    nchunks = S // tc
    m0p = jnp.broadcast_to(m0[:, :, None], (B, ML_HEADS, 128))
    m0p = jnp.concatenate([m0p, jnp.zeros((B, 8 - ML_HEADS, 128), _F32)], axis=1)

    def body(*refs):
        ins, rest = refs[:18], refs[18:]
        if want_vn:
            outs, scr = rest[:5], rest[5:]
        else:
            outs, scr = rest[:4] + (None,), rest[4:]
        _mixer_kernel(*ins, *outs, *scr, tc=tc, sgu_len=sgu_len)

    const2 = lambda b, c: (0, 0)
    const3 = lambda b, c: (0, 0, 0)
    per_b3 = lambda b, c: (b, 0, 0)
    per_b4 = lambda b, c: (b, 0, 0, 0)
    once = dict(pipeline_mode=pl.Buffered(1))
    in_specs = [
        pl.BlockSpec((1, tc, D), lambda b, c: (b, c, 0)),
        pl.BlockSpec((1, N_MEM, D), per_b3),
        pl.BlockSpec((1, N_MEM, D), per_b3),
        pl.BlockSpec((1, ML_HEADS, ML_DQK, ML_DV), per_b4),
        pl.BlockSpec((1, ML_HEADS, ML_DQK), per_b3),
        pl.BlockSpec((1, 8, 128), per_b3),
        pl.BlockSpec((1, D), const2, **once),
        pl.BlockSpec((D, _P_COLS), const2, **once),
        pl.BlockSpec((1, _P_COLS), const2, **once),
        pl.BlockSpec((D, 8), const2, **once),
        pl.BlockSpec((8, D), const2, **once),
        pl.BlockSpec((1, 8), const2, **once),
        pl.BlockSpec((8, 1), const2, **once),
        pl.BlockSpec((1, D), const2, **once),
        pl.BlockSpec((1, D), const2, **once),
        pl.BlockSpec((SGU_GROUPS, SGU_CHUNK, SGU_CHUNK), const3, **once),
        pl.BlockSpec((SGU_CHUNK, SGU_GROUPS), const2, **once),
        pl.BlockSpec((D, D), const2, **once),
    ]
    out_shape = [
        jax.ShapeDtypeStruct((B, S, D), _F32),
        jax.ShapeDtypeStruct((B, ML_HEADS, ML_DQK, ML_DV), _F32),
        jax.ShapeDtypeStruct((B, ML_HEADS, ML_DQK), _F32),
        jax.ShapeDtypeStruct((B, 8, 128), _F32),
    ]
    out_specs = [
        pl.BlockSpec((1, tc, D), lambda b, c: (b, c, 0)),
        pl.BlockSpec((1, ML_HEADS, ML_DQK, ML_DV), per_b4),
        pl.BlockSpec((1, ML_HEADS, ML_DQK), per_b3),
        pl.BlockSpec((1, 8, 128), per_b3),
    ]
    if want_vn:
        out_shape.append(jax.ShapeDtypeStruct((B, S, D), _F32))
        out_specs.append(pl.BlockSpec((1, tc, D), lambda b, c: (b, c, 0)))
    outs = pl.pallas_call(
        body,
        out_shape=tuple(out_shape),
        grid=(B, nchunks),
        in_specs=in_specs,
        out_specs=tuple(out_specs),
        scratch_shapes=[
            pltpu.VMEM((tc, _SEG_A), _F32),
            pltpu.VMEM((tc, _SEG_B), _F32),
            pltpu.VMEM((tc, _SEG_C), _F32),
            pltpu.VMEM((tc, _SEG_D), _F32),
            pltpu.VMEM((tc, D), _F32),
            pltpu.VMEM((ML_HEADS, ML_DQK, ML_DV), _F32),
            pltpu.VMEM((ML_HEADS, ML_DQK), _F32),
            pltpu.VMEM((8, 128), _F32),
        ],
        compiler_params=pltpu.CompilerParams(
            dimension_semantics=("arbitrary", "arbitrary"), vmem_limit_bytes=_VMEM_LIMIT),
        name="mixer",
    )(x, mem_k, mem_v, c0, n0, m0p, *wts)
    x1, c1, n1, m1p = outs[:4]
    vn = outs[4] if want_vn else None
    return x1, c1, n1, m1p[:, :ML_HEADS, 0], vn


def _batcher_pairs(n):
    pairs = []
    p = 1
    while p < n:
        k = p
        while k >= 1:
            for j in range(k % p, n - k, 2 * k):
                for i in range(min(k, n - j - k)):
                    if (i + j) // (2 * p) == (i + j + k) // (2 * p):
                        pairs.append((i + j, i + j + k))
            k //= 2
        p *= 2
    return pairs


_SORT16 = _batcher_pairs(PEER_NKEYS // 8)
_SEL_UNROLL = 2


def _sorted_columns(c):
    cols = [c[8 * k:8 * k + 8, :] for k in range(PEER_NKEYS // 8)]
    for i, j in _SORT16:
        cols[i], cols[j] = jnp.maximum(cols[i], cols[j]), jnp.minimum(cols[i], cols[j])
    return cols


def _pop_top(cols, out_scr):
    for r in range(PEER_TOPK):
        mx = jnp.max(cols[0], axis=0, keepdims=True)
        out_scr[r:r + 1, :] = mx
        left = PEER_TOPK - 1 - r
        if left:
            eq = cols[0] == mx
            for k in range(left):
                cols[k] = jnp.where(eq, cols[k + 1], cols[k])


def _bf16_pair(x):
    u = pltpu.bitcast(x, jnp.uint32)
    u = u + jnp.uint32(0x7FFF) + ((u >> 16) & jnp.uint32(1))
    hi = u & jnp.uint32(0xFFFF0000)
    return hi | (hi >> 16)


def _peer_select_kernel(x1_ref, gffn_ref, wpq_ref, k1_ref, k2_ref,
                        xn_ref, cnt_ref, p1n_ref, rank_ref, p2_ref,
                        q_scr, c1_scr, c2_scr, v1_scr, v2_scr, n_scr, *, tb):
    xn = _rms(x1_ref[...], gffn_ref[...]).astype(_BF16)
    xn_ref[...] = xn
    q_scr[...] = _dot(xn, wpq_ref[...])
    k1 = k1_ref[...].astype(_BF16)
    k2 = k2_ref[...].astype(_BF16)
    half = PEER_DQ // 2
    K = PEER_TOPK

    def lane_group(h, ls):
        c1 = c1_scr[:, ls]
        c2 = c2_scr[:, ls]
        _pop_top(_sorted_columns(c1), v1_scr)
        _pop_top(_sorted_columns(c2), v2_scr)
        v1 = v1_scr[...]
        lists = [v1_scr[0:8, :] + v2_scr[b:b + 1, :] for b in range(8)]
        ea = v1_scr[8:16, :] + v2_scr[0:1, :]
        eb = v1_scr[0:1, :] + v2_scr[8:16, :]
        cands = lists + [ea, eb]
        tau = None
        for r in range(K):
            tau = jnp.max(jnp.maximum(jnp.maximum(lists[0], ea), eb), axis=0, keepdims=True)
            left = K - 1 - r
            if left:
                eq = lists[0] == tau
                for k in range(min(left, 7)):
                    lists[k] = jnp.where(eq, lists[k + 1], lists[k])
                if left >= 8:
                    lists[7] = jnp.where(eq, _NEG_INF, lists[7])
                ea = jnp.where(ea == tau, _NEG_INF, ea)
                eb = jnp.where(eb == tau, _NEG_INF, eb)
        z = sum(jnp.sum(jnp.where(c >= tau, jnp.exp(c), 0.0), axis=0, keepdims=True) for c in cands)
        n = jnp.zeros(v1.shape, _F32)
        for b in range(K):
            n = jnp.where(v1 + v2_scr[b:b + 1, :] >= tau, float(b + 1), n)
        n_scr[...] = n
        cnt = jnp.zeros(c1.shape, _F32)
        rank2 = jnp.full(c2.shape, float(K), _F32)
        for a in range(K - 1, -1, -1):
            cnt = jnp.where(c1 >= v1_scr[a:a + 1, :], n_scr[a:a + 1, :], cnt)
            rank2 = jnp.where(c2 >= v2_scr[a:a + 1, :], float(a), rank2)
        cnt_ref[h, :, ls] = _bf16_pair(cnt)
        p1n_ref[h, :, ls] = _bf16_pair(jnp.exp(c1) / z)
        rank_ref[h, :, ls] = rank2.astype(_BF16)
        p2_ref[h, :, ls] = jnp.exp(c2).astype(_BF16)

    def head(h, carry):
        q0 = pl.multiple_of(h * PEER_DQ, PEER_DQ)
        s1 = _dot_nt(k1, q_scr[:, pl.ds(q0, half)].astype(_BF16))
        s2 = _dot_nt(k2, q_scr[:, pl.ds(q0 + half, half)].astype(_BF16))
        c1_scr[...] = s1 - jnp.max(s1, axis=0, keepdims=True)
        c2_scr[...] = s2 - jnp.max(s2, axis=0, keepdims=True)

        def lane_groups(i, carry):
            for u in range(_SEL_UNROLL):
                lane_group(h, pl.ds(pl.multiple_of((i * _SEL_UNROLL + u) * 128, 128), 128))
            return carry

        lax.fori_loop(0, tb // (128 * _SEL_UNROLL), lane_groups, 0)
        return carry

    lax.fori_loop(0, PEER_HEADS, head, 0)


def _peer_select(x1, g_ffn, wpq_bf, k_sub1, k_sub2, *, tb):
    T, D = x1.shape
    assert T % tb == 0 and tb % (128 * _SEL_UNROLL) == 0
    sel = lambda dt: jax.ShapeDtypeStruct((PEER_HEADS, PEER_NKEYS, T), dt)
    sel_spec = pl.BlockSpec((PEER_HEADS, PEER_NKEYS, tb), lambda i: (0, 0, i))
    const2 = lambda i: (0, 0)
    return pl.pallas_call(
        functools.partial(_peer_select_kernel, tb=tb),
        out_shape=(jax.ShapeDtypeStruct((T, D), _BF16), sel(jnp.uint32), sel(jnp.uint32), sel(_BF16), sel(_BF16)),
        grid=(T // tb,),
        in_specs=[
            pl.BlockSpec((tb, D), lambda i: (i, 0)),
            pl.BlockSpec((1, D), const2),
            pl.BlockSpec((D, PEER_HEADS * PEER_DQ), const2),
            pl.BlockSpec((PEER_NKEYS, PEER_DQ // 2), const2),
            pl.BlockSpec((PEER_NKEYS, PEER_DQ // 2), const2),
        ],
        out_specs=(pl.BlockSpec((tb, D), lambda i: (i, 0)),) + (sel_spec,) * 4,
        scratch_shapes=[
            pltpu.VMEM((tb, PEER_HEADS * PEER_DQ), _F32),
            pltpu.VMEM((PEER_NKEYS, tb), _F32),
            pltpu.VMEM((PEER_NKEYS, tb), _F32),
            pltpu.VMEM((PEER_TOPK, 128), _F32),
            pltpu.VMEM((PEER_TOPK, 128), _F32),
            pltpu.VMEM((PEER_TOPK, 128), _F32),
        ],
        compiler_params=pltpu.CompilerParams(
            dimension_semantics=("arbitrary",), vmem_limit_bytes=_VMEM_LIMIT),
        name="peer_select",
    )(x1, g_ffn.reshape(1, D), wpq_bf, k_sub1, k_sub2)


def _peer_dense_kernel(xn_ref, u_ref, vtp_ref, vtl_ref, cnt_ref, p1n_ref, rank_ref, p2_ref, x1_ref,
                       gfin_ref, y_ref, acc_scr, st_scr, h_scr, *, tb, rows):
    j = pl.program_id(1)
    cur = lax.rem(j, 2)

    @pl.when(j == 0)
    def _():
        acc_scr[...] = jnp.zeros_like(acc_scr)
        h_scr[1] = jnp.zeros(h_scr.shape[1:], _BF16)

    tile = (PEER_NKEYS, 128)
    words = (PEER_NKEYS // 2, 128)
    rb = 2
    for b in range(rows // rb):
        bs = slice(b * rb * PEER_NKEYS, (b + 1) * rb * PEER_NKEYS)
        st_scr[bs, :] = _dot_nt(u_ref[bs, :], xn_ref[...])
        for r in range(b * rb, (b + 1) * rb):
            es = slice(r * PEER_NKEYS, (r + 1) * PEER_NKEYS)
            for lg in range(tb // 128):
                ls = slice(lg * 128, (lg + 1) * 128)
                gate = jnp.zeros(tile, _BF16)
                for h in range(PEER_HEADS):
                    cnt = pltpu.bitcast(jnp.broadcast_to(cnt_ref[h, r:r + 1, ls], words), _BF16)
                    pn = pltpu.bitcast(jnp.broadcast_to(p1n_ref[h, r:r + 1, ls], words), _BF16)
                    sel = jnp.minimum(jnp.maximum(cnt - rank_ref[h, :, ls], 0), pn)
                    gate = gate + sel * p2_ref[h, :, ls]
                act = jax.nn.gelu(st_scr[es, ls].astype(_BF16))
                h_scr[cur, es, ls] = gate * act
    acc_scr[...] += _dot(vtp_ref[0], h_scr[1 - cur])

    @pl.when(j == pl.num_programs(1) - 1)
    def _():
        acc = acc_scr[...] + _dot(vtl_ref[0], h_scr[cur])
        x2 = x1_ref[...] + acc.T
        y_ref[...] = _rms(x2, gfin_ref[...])


def _peer_dense(xn, u_bf, vt_bf, cnt, p1n, rank2, p2, x1, g_final, *, tb, ec):
    T, D = x1.shape
    rows = ec // PEER_NKEYS
    nchunks = PEER_NEXP // ec
    assert T % tb == 0 and PEER_NEXP % ec == 0 and rows % 8 == 0
    sel_all = pl.BlockSpec((PEER_HEADS, PEER_NKEYS, tb), lambda i, j: (0, 0, i))
    sel_rows = pl.BlockSpec((PEER_HEADS, rows, tb), lambda i, j: (0, j, i))
    tok = pl.BlockSpec((tb, D), lambda i, j: (i, 0))
    return pl.pallas_call(
        functools.partial(_peer_dense_kernel, tb=tb, rows=rows),
        out_shape=jax.ShapeDtypeStruct((T, D), _F32),
        grid=(T // tb, nchunks),
        in_specs=[
            tok,
            pl.BlockSpec((ec, D), lambda i, j: (j, 0)),
            pl.BlockSpec((1, D, ec), lambda i, j: (jnp.maximum(j - 1, 0), 0, 0)),
            pl.BlockSpec((1, D, ec), lambda i, j: (nchunks - 1, 0, 0), pipeline_mode=pl.Buffered(1)),
            sel_rows, sel_rows, sel_all, sel_all,
            tok,
            pl.BlockSpec((1, D), lambda i, j: (0, 0)),
        ],
        out_specs=tok,
        scratch_shapes=[
            pltpu.VMEM((D, tb), _F32),
            pltpu.VMEM((ec, tb), _F32),
            pltpu.VMEM((2, ec, tb), _BF16),
        ],
        compiler_params=pltpu.CompilerParams(
            dimension_semantics=("arbitrary", "arbitrary"), vmem_limit_bytes=_VMEM_LIMIT),
        name="peer_dense",
    )(xn, u_bf, vt_bf, vt_bf, cnt, p1n, rank2, p2, x1, g_final.reshape(1, D))


def _mixer_weights(g_mix, w_in, b_in, g_mlh, g_sgu, w_s, b_s, w_out):
    nq = ML_HEADS * ML_DQK
    nv = ML_HEADS * ML_DV
    o_q, o_k, o_v = 0, nq, 2 * nq
    o_ig = o_v + nv
    o_fg = o_ig + ML_HEADS
    o_og = o_fg + ML_HEADS
    o_su = o_og + nv
    o_sv = o_su + D_MODEL
    o_xq = o_sv + D_MODEL
    o_gt = o_xq + D_MODEL
    gate = lambda b: slice(o_gt + b * D_MODEL, o_gt + (b + 1) * D_MODEL)
    main = lambda a: jnp.concatenate(
        [a[..., o_q:o_ig],
         a[..., o_og:o_su], a[..., gate(0)],
         a[..., o_su:o_xq], a[..., gate(1)],
         a[..., o_xq:o_gt], a[..., gate(2)]], axis=-1)
    w_if = w_in[:, o_ig:o_og]
    b_if = b_in[o_ig:o_og]
    return (
        g_mix.reshape(1, D_MODEL),
        main(w_in).astype(_BF16),
        main(b_in).reshape(1, _P_COLS),
        w_if.astype(_BF16),
        w_if.T.astype(_BF16),
        b_if.reshape(1, 8),
        b_if.reshape(8, 1),
        g_mlh.reshape(1, D_MODEL),
        g_sgu.reshape(1, D_MODEL),
        w_s,
        b_s.T,
        w_out.astype(_BF16),
    )


def kernel(x_prompt, x_sample, mem_prompt, cache_mem_k, cache_mem_v, state_mlstm_C, state_mlstm_n,
           state_mlstm_m, g_mix, w_in, b_in, g_mlh, g_sgu, w_s, b_s, g_mem, w_mk, w_mv, w_out, g_ffn,
           w_pq, k_sub1, k_sub2, u_exp, v_exp, g_final):
    depth = g_mix.shape[0]
    assert depth == 1
    l = 0
    Bp, Sp, D = x_prompt.shape
    Bs, Ss, _ = x_sample.shape

    wts = _mixer_weights(g_mix[l], w_in[l], b_in[l], g_mlh[l], g_sgu[l], w_s[l], b_s[l], w_out[l])

    mk, mv = _mem_kv(mem_prompt, g_mem[l], w_mk[l], w_mv[l])
    zc = jnp.zeros((Bp, ML_HEADS, ML_DQK, ML_DV), _F32)
    zn = jnp.zeros((Bp, ML_HEADS, ML_DQK), _F32)
    zm = jnp.zeros((Bp, ML_HEADS), _F32)
    x1p, cp, np_, mp, _ = _mixer(x_prompt, mk, mv, zc, zn, zm, wts, tc=256, want_vn=False)
    x1s, cs, ns, ms, vn = _mixer(
        x_sample, cache_mem_k[l].reshape(Bs, N_MEM, D), cache_mem_v[l].reshape(Bs, N_MEM, D),
        state_mlstm_C[l], state_mlstm_n[l], state_mlstm_m[l], wts, tc=Ss, want_vn=True)

    wpq_bf = w_pq[l].astype(_BF16)
    u_bf = u_exp[l].astype(_BF16)
    vt_bf = v_exp[l].astype(_BF16).reshape(PEER_NEXP // _PEER_EC, _PEER_EC, D).transpose(0, 2, 1)

    def peer(x1):
        B, S, _ = x1.shape
        x1 = x1.reshape(B * S, D)
        xn, cnt, p1n, rank2, p2 = _peer_select(x1, g_ffn[l], wpq_bf, k_sub1[l], k_sub2[l], tb=_PEER_TB)
        y = _peer_dense(xn, u_bf, vt_bf, cnt, p1n, rank2, p2, x1, g_final, tb=_PEER_TB, ec=_PEER_EC)
        return y.reshape(B, S, D)

    hs = (XA_HEADS, XA_DH)
    return (peer(x1p), peer(x1s), cp[None], np_[None], mp[None],
            mk.reshape(1, Bp, N_MEM, *hs), mv.reshape(1, Bp, N_MEM, *hs),
            cs[None], ns[None], ms[None], vn[None])
```

```python
import functools

import jax
import jax.numpy as jnp
from jax import lax
from jax.experimental import pallas as pl
from jax.experimental.pallas import tpu as pltpu

D_MODEL = 1024
EPS = 1e-6
CHUNK = 64
N_MEM = 256
ML_HEADS = 4
ML_DQK = 128
ML_DV = D_MODEL // ML_HEADS
SGU_CHUNK = 128
SGU_GROUPS = 4
SGU_GDIM = D_MODEL // SGU_GROUPS
XA_HEADS = 4
XA_DH = D_MODEL // XA_HEADS
PEER_HEADS = 8
PEER_NKEYS = 128
PEER_DQ = 256
PEER_TOPK = 16
PEER_NEXP = PEER_NKEYS * PEER_NKEYS

_SEG_A = ML_HEADS * (2 * ML_DQK + ML_DV)
_SEG_B = 2 * D_MODEL
_SEG_C = 3 * D_MODEL
_SEG_D = 2 * D_MODEL
_SEG0 = (0, _SEG_A, _SEG_A + _SEG_B, _SEG_A + _SEG_B + _SEG_C)
_P_COLS = _SEG_A + _SEG_B + _SEG_C + _SEG_D
_K0 = ML_HEADS * ML_DQK
_V0 = 2 * ML_HEADS * ML_DQK

_VMEM_LIMIT = 56 * 1024 * 1024

_PEER_TB = 512
_PEER_EC = 1024

_BF16 = jnp.bfloat16
_F32 = jnp.float32
_NEG_INF = float("-inf")


def _rms(xf, g):
    return xf * lax.rsqrt(jnp.mean(xf * xf, axis=-1, keepdims=True) + EPS) * g


def _dot(a, b):
    return jnp.dot(a, b, preferred_element_type=_F32)


def _dot_nt(a, b):
    return lax.dot_general(a, b, (((1,), (1,)), ((), ())), preferred_element_type=_F32)


def _bmm(a, b, ca, cb):
    return lax.dot_general(a, b, (((ca,), (cb,)), ((0,), (0,))), preferred_element_type=_F32)


def _mem_kv_kernel(mem_ref, g_ref, wk_ref, wv_ref, k_ref, v_ref):
    mn = _rms(mem_ref[0], g_ref[...]).astype(_BF16)
    k_ref[0] = _dot(mn, wk_ref[...])
    v_ref[0] = _dot(mn, wv_ref[...])


def _mem_kv(mem, g_mem, w_mk, w_mv):
    B = mem.shape[0]
    full = lambda b: (0, 0)
    return pl.pallas_call(
        _mem_kv_kernel,
        out_shape=(jax.ShapeDtypeStruct((B, N_MEM, D_MODEL), _F32),) * 2,
        grid=(B,),
        in_specs=[
            pl.BlockSpec((1, N_MEM, D_MODEL), lambda b: (b, 0, 0)),
            pl.BlockSpec((1, D_MODEL), full),
            pl.BlockSpec((D_MODEL, D_MODEL), full),
            pl.BlockSpec((D_MODEL, D_MODEL), full),
        ],
        out_specs=(pl.BlockSpec((1, N_MEM, D_MODEL), lambda b: (b, 0, 0)),) * 2,
        compiler_params=pltpu.CompilerParams(
            dimension_semantics=("arbitrary",), vmem_limit_bytes=_VMEM_LIMIT),
        name="mem_kv",
    )(mem, g_mem.reshape(1, D_MODEL), w_mk.astype(_BF16), w_mv.astype(_BF16))


def _mixer_kernel(x_ref, mk_ref, mv_ref, c0_ref, n0_ref, m0_ref, gmix_ref, win_ref, bin_ref,
                  wif_ref, wift_ref, bif_ref, bift_ref, gmlh_ref, gsgu_ref, ws_ref, bst_ref,
                  wout_ref,
                  x1_ref, c_out_ref, n_out_ref, m_out_ref, vn_ref,
                  pa_scr, pb_scr, pc_scr, pd_scr, mrg_scr, c_scr, n_scr, m_scr, *, tc, sgu_len):
    ci = pl.program_id(1)
    nsub = tc // CHUNK

    @pl.when(ci == 0)
    def _():
        c_scr[...] = c0_ref[0]
        n_scr[...] = n0_ref[0]
        m_scr[...] = m0_ref[0]

    x = x_ref[0]
    xn = _rms(x, gmix_ref[...]).astype(_BF16)

    def project(seg, out_scr):
        cols = slice(_SEG0[seg], _SEG0[seg] + out_scr.shape[1])
        out_scr[...] = _dot(xn, win_ref[:, cols]) + bin_ref[:, cols]

    project(0, pa_scr)
    gif = _dot(xn, wif_ref[...]) + bif_ref[...]
    gift = _dot_nt(wift_ref[...], xn) + bift_ref[...]

    row = lax.broadcasted_iota(jnp.int32, (CHUNK, CHUNK), 0)
    col = lax.broadcasted_iota(jnp.int32, (CHUNK, CHUNK), 1)
    tril = row >= col
    tril_f = tril.astype(_F32)
    triu_f = (row <= col).astype(_F32)
    stack = lambda f: jnp.stack([f(h) for h in range(ML_HEADS)])
    for j in range(nsub):
        r0 = j * CHUNK
        rows = slice(r0, r0 + CHUNK)
        ig_c = gif[rows, 0:ML_HEADS]
        lf_c = jax.nn.log_sigmoid(gif[rows, ML_HEADS:2 * ML_HEADS])
        ig_r = gift[0:ML_HEADS, rows]
        lf_r = jax.nn.log_sigmoid(gift[ML_HEADS:2 * ML_HEADS, rows])
        b_c = jnp.dot(tril_f, lf_c, precision=lax.Precision.HIGHEST,
                      preferred_element_type=_F32)
        b_r = jnp.dot(lf_r, triu_f, precision=lax.Precision.HIGHEST,
                      preferred_element_type=_F32)
        q = stack(lambda h: pa_scr[rows, h * ML_DQK:(h + 1) * ML_DQK]).astype(_BF16)
        k = stack(lambda h: pa_scr[rows, _K0 + h * ML_DQK:_K0 + (h + 1) * ML_DQK]) * (ML_DQK ** -0.5)
        v = stack(lambda h: pa_scr[rows, _V0 + h * ML_DV:_V0 + (h + 1) * ML_DV]).astype(_BF16)
        c_st = c_scr[...]
        n_st = stack(lambda h: n_scr[h:h + 1, :])
        m_st = stack(lambda h: m_scr[h:h + 1, 0:1])
        bc = stack(lambda h: b_c[:, h:h + 1])
        br = stack(lambda h: b_r[h:h + 1, :])
        ig_row = stack(lambda h: ig_r[h:h + 1, :])
        ig_col = stack(lambda h: ig_c[:, h:h + 1])
        logd = jnp.where(tril[None], bc - br + ig_row, _NEG_INF)
        log_prev = bc + m_st
        m_t = jnp.maximum(log_prev, jnp.max(logd, axis=2, keepdims=True))
        a = _bmm(q, k.astype(_BF16), 2, 2) * jnp.exp(logd - m_t)
        wp = jnp.exp(log_prev - m_t)
        num = _bmm(a.astype(_BF16), v, 2, 1) + wp * _bmm(q, c_st.astype(_BF16), 2, 1)
        qn = jnp.sum(q.astype(_F32) * n_st, axis=2, keepdims=True)
        den = jnp.sum(a, axis=2, keepdims=True) + wp * qn
        hh = num / jnp.maximum(jnp.abs(den), jnp.exp(-m_t))
        hn = hh * lax.rsqrt(jnp.mean(hh * hh, axis=2, keepdims=True) + EPS)
        for h in range(ML_HEADS):
            mrg_scr[rows, h * ML_DV:(h + 1) * ML_DV] = hn[h]
        b_last = bc[:, CHUNK - 1:CHUNK, :]
        log_in = b_last - bc + ig_col
        m_new = jnp.maximum(b_last + m_st, jnp.max(log_in, axis=1, keepdims=True))
        wi = jnp.exp(log_in - m_new)
        wc = jnp.exp(b_last + m_st - m_new)
        kw = wi * k
        c_scr[...] = wc * c_st + _bmm(kw.astype(_BF16), v, 1, 1)
        n_new = wc * n_st + jnp.sum(kw, axis=1, keepdims=True)
        for h in range(ML_HEADS):
            n_scr[h:h + 1, :] = n_new[h]
            m_scr[h:h + 1, :] = jnp.broadcast_to(m_new[h], (1, 128))

    project(1, pb_scr)
    h_a = jax.nn.sigmoid(pb_scr[:, 0:D_MODEL]) * mrg_scr[...] * gmlh_ref[...]
    mrg_scr[...] = jax.nn.sigmoid(pb_scr[:, D_MODEL:2 * D_MODEL]) * h_a

    project(2, pc_scr)
    sv = jax.nn.gelu(pc_scr[:, D_MODEL:2 * D_MODEL])
    vn = _rms(sv, gsgu_ref[...])
    if vn_ref is not None:
        vn_ref[0] = vn
    vnb = vn.astype(_BF16)
    rs = lax.broadcasted_iota(jnp.int32, (sgu_len, sgu_len), 0)
    cs = lax.broadcasted_iota(jnp.int32, (sgu_len, sgu_len), 1)
    for g in range(SGU_GROUPS):
        wsg = jnp.where(rs >= cs, ws_ref[g, 0:sgu_len, 0:sgu_len], 0.0).astype(_BF16)
        bsg = bst_ref[0:sgu_len, g:g + 1]
        gl = slice(g * SGU_GDIM, (g + 1) * SGU_GDIM)
        for c in range(tc // sgu_len):
            rows = slice(c * sgu_len, (c + 1) * sgu_len)
            mix = _dot(wsg, vnb[rows, gl]) + bsg
            u = jax.nn.gelu(pc_scr[rows, g * SGU_GDIM:(g + 1) * SGU_GDIM])
            g1 = jax.nn.sigmoid(pc_scr[rows, 2 * D_MODEL + g * SGU_GDIM:2 * D_MODEL + (g + 1) * SGU_GDIM])
            mrg_scr[rows, gl] += g1 * (u * mix)

    project(3, pd_scr)
    for h in range(XA_HEADS):
        hl = slice(h * XA_DH, (h + 1) * XA_DH)
        xq = pd_scr[:, h * XA_DH:(h + 1) * XA_DH].astype(_BF16)
        sc = _dot_nt(xq, mk_ref[0, :, hl].astype(_BF16)) * (XA_DH ** -0.5)
        sc = sc - jnp.max(sc, axis=1, keepdims=True)
        e = jnp.exp(sc)
        att = e / jnp.sum(e, axis=1, keepdims=True)
        h_c = _dot(att.astype(_BF16), mv_ref[0, :, hl].astype(_BF16))
        g2 = jax.nn.sigmoid(pd_scr[:, D_MODEL + h * XA_DH:D_MODEL + (h + 1) * XA_DH])
        mrg_scr[:, hl] += g2 * h_c

    x1_ref[0] = x + _dot(mrg_scr[...].astype(_BF16), wout_ref[...])

    @pl.when(ci == pl.num_programs(1) - 1)
    def _():
        c_out_ref[0] = c_scr[...]
        n_out_ref[0] = n_scr[...]
        m_out_ref[0] = m_scr[...]


def _mixer(x, mem_k, mem_v, c0, n0, m0, wts, *, tc, want_vn):
    B, S, D = x.shape
    sgu_len = min(S, SGU_CHUNK)
    assert S % tc == 0 and tc % CHUNK == 0 and tc % sgu_len == 0
    nchunks = S // tc
    m0p = jnp.broadcast_to(m0[:, :, None], (B, ML_HEADS, 128))
    m0p = jnp.concatenate([m0p, jnp.zeros((B, 8 - ML_HEADS, 128), _F32)], axis=1)

    def body(*refs):
        ins, rest = refs[:18], refs[18:]
        if want_vn:
            outs, scr = rest[:5], rest[5:]
        else:
            outs, scr = rest[:4] + (None,), rest[4:]
        _mixer_kernel(*ins, *outs, *scr, tc=tc, sgu_len=sgu_len)

    const2 = lambda b, c: (0, 0)
    const3 = lambda b, c: (0, 0, 0)
    per_b3 = lambda b, c: (b, 0, 0)
    per_b4 = lambda b, c: (b, 0, 0, 0)
    once = dict(pipeline_mode=pl.Buffered(1))
    in_specs = [
        pl.BlockSpec((1, tc, D), lambda b, c: (b, c, 0)),
        pl.BlockSpec((1, N_MEM, D), per_b3),
        pl.BlockSpec((1, N_MEM, D), per_b3),
        pl.BlockSpec((1, ML_HEADS, ML_DQK, ML_DV), per_b4),
        pl.BlockSpec((1, ML_HEADS, ML_DQK), per_b3),
        pl.BlockSpec((1, 8, 128), per_b3),
        pl.BlockSpec((1, D), const2, **once),
        pl.BlockSpec((D, _P_COLS), const2, **once),
        pl.BlockSpec((1, _P_COLS), const2, **once),
        pl.BlockSpec((D, 8), const2, **once),
        pl.BlockSpec((8, D), const2, **once),
        pl.BlockSpec((1, 8), const2, **once),
        pl.BlockSpec((8, 1), const2, **once),
        pl.BlockSpec((1, D), const2, **once),
        pl.BlockSpec((1, D), const2, **once),
        pl.BlockSpec((SGU_GROUPS, SGU_CHUNK, SGU_CHUNK), const3, **once),
        pl.BlockSpec((SGU_CHUNK, SGU_GROUPS), const2, **once),
        pl.BlockSpec((D, D), const2, **once),
    ]
    out_shape = [
        jax.ShapeDtypeStruct((B, S, D), _F32),
        jax.ShapeDtypeStruct((B, ML_HEADS, ML_DQK, ML_DV), _F32),
        jax.ShapeDtypeStruct((B, ML_HEADS, ML_DQK), _F32),
        jax.ShapeDtypeStruct((B, 8, 128), _F32),
    ]
    out_specs = [
        pl.BlockSpec((1, tc, D), lambda b, c: (b, c, 0)),
        pl.BlockSpec((1, ML_HEADS, ML_DQK, ML_DV), per_b4),
        pl.BlockSpec((1, ML_HEADS, ML_DQK), per_b3),
        pl.BlockSpec((1, 8, 128), per_b3),
    ]
    if want_vn:
        out_shape.append(jax.ShapeDtypeStruct((B, S, D), _F32))
        out_specs.append(pl.BlockSpec((1, tc, D), lambda b, c: (b, c, 0)))
    outs = pl.pallas_call(
        body,
        out_shape=tuple(out_shape),
        grid=(B, nchunks),
        in_specs=in_specs,
        out_specs=tuple(out_specs),
        scratch_shapes=[
            pltpu.VMEM((tc, _SEG_A), _F32),
            pltpu.VMEM((tc, _SEG_B), _F32),
            pltpu.VMEM((tc, _SEG_C), _F32),
            pltpu.VMEM((tc, _SEG_D), _F32),
            pltpu.VMEM((tc, D), _F32),
            pltpu.VMEM((ML_HEADS, ML_DQK, ML_DV), _F32),
            pltpu.VMEM((ML_HEADS, ML_DQK), _F32),
            pltpu.VMEM((8, 128), _F32),
        ],
        compiler_params=pltpu.CompilerParams(
            dimension_semantics=("arbitrary", "arbitrary"), vmem_limit_bytes=_VMEM_LIMIT),
        name="mixer",
    )(x, mem_k, mem_v, c0, n0, m0p, *wts)
    x1, c1, n1, m1p = outs[:4]
    vn = outs[4] if want_vn else None
    return x1, c1, n1, m1p[:, :ML_HEADS, 0], vn


def _batcher_pairs(n):
    pairs = []
    p = 1
    while p < n:
        k = p
        while k >= 1:
            for j in range(k % p, n - k, 2 * k):
                for i in range(min(k, n - j - k)):
                    if (i + j) // (2 * p) == (i + j + k) // (2 * p):
                        pairs.append((i + j, i + j + k))
            k //= 2
        p *= 2
    return pairs


_SORT16 = _batcher_pairs(PEER_NKEYS // 8)
_SEL_UNROLL = 2


def _sorted_columns(c):
    cols = [c[8 * k:8 * k + 8, :] for k in range(PEER_NKEYS // 8)]
    for i, j in _SORT16:
        cols[i], cols[j] = jnp.maximum(cols[i], cols[j]), jnp.minimum(cols[i], cols[j])
    return cols


def _pop_top(cols, out_scr):
    for r in range(PEER_TOPK):
        mx = jnp.max(cols[0], axis=0, keepdims=True)
        out_scr[r:r + 1, :] = mx
        left = PEER_TOPK - 1 - r
        if left:
            eq = cols[0] == mx
            for k in range(left):
                cols[k] = jnp.where(eq, cols[k + 1], cols[k])


def _bf16_pair(x):
    u = pltpu.bitcast(x, jnp.uint32)
    u = u + jnp.uint32(0x7FFF) + ((u >> 16) & jnp.uint32(1))
    hi = u & jnp.uint32(0xFFFF0000)
    return hi | (hi >> 16)


def _peer_select_kernel(x1_ref, gffn_ref, wpq_ref, k1_ref, k2_ref,
                        xn_ref, cnt_ref, p1n_ref, rank_ref, p2_ref,
                        q_scr, c1_scr, c2_scr, v1_scr, v2_scr, n_scr, *, tb):
    xn = _rms(x1_ref[...], gffn_ref[...]).astype(_BF16)
    xn_ref[...] = xn
    q_scr[...] = _dot(xn, wpq_ref[...])
    k1 = k1_ref[...].astype(_BF16)
    k2 = k2_ref[...].astype(_BF16)
    half = PEER_DQ // 2
    K = PEER_TOPK

    def lane_group(h, ls):
        c1 = c1_scr[:, ls]
        c2 = c2_scr[:, ls]
        _pop_top(_sorted_columns(c1), v1_scr)
        _pop_top(_sorted_columns(c2), v2_scr)
        v1 = v1_scr[...]
        lists = [v1_scr[0:8, :] + v2_scr[b:b + 1, :] for b in range(8)]
        ea = v1_scr[8:16, :] + v2_scr[0:1, :]
        eb = v1_scr[0:1, :] + v2_scr[8:16, :]
        cands = lists + [ea, eb]
        tau = None
        for r in range(K):
            tau = jnp.max(jnp.maximum(jnp.maximum(lists[0], ea), eb), axis=0, keepdims=True)
            left = K - 1 - r
            if left:
                eq = lists[0] == tau
                for k in range(min(left, 7)):
                    lists[k] = jnp.where(eq, lists[k + 1], lists[k])
                if left >= 8:
                    lists[7] = jnp.where(eq, _NEG_INF, lists[7])
                ea = jnp.where(ea == tau, _NEG_INF, ea)
                eb = jnp.where(eb == tau, _NEG_INF, eb)
        z = sum(jnp.sum(jnp.where(c >= tau, jnp.exp(c), 0.0), axis=0, keepdims=True) for c in cands)
        n = jnp.zeros(v1.shape, _F32)
        for b in range(K):
            n = jnp.where(v1 + v2_scr[b:b + 1, :] >= tau, float(b + 1), n)
        n_scr[...] = n
        cnt = jnp.zeros(c1.shape, _F32)
        rank2 = jnp.full(c2.shape, float(K), _F32)
        for a in range(K - 1, -1, -1):
            cnt = jnp.where(c1 >= v1_scr[a:a + 1, :], n_scr[a:a + 1, :], cnt)
            rank2 = jnp.where(c2 >= v2_scr[a:a + 1, :], float(a), rank2)
        cnt_ref[h, :, ls] = _bf16_pair(cnt)
        p1n_ref[h, :, ls] = _bf16_pair(jnp.exp(c1) / z)
        rank_ref[h, :, ls] = rank2.astype(_BF16)
        p2_ref[h, :, ls] = jnp.exp(c2).astype(_BF16)

    def head(h, carry):
        q0 = pl.multiple_of(h * PEER_DQ, PEER_DQ)
        s1 = _dot_nt(k1, q_scr[:, pl.ds(q0, half)].astype(_BF16))
        s2 = _dot_nt(k2, q_scr[:, pl.ds(q0 + half, half)].astype(_BF16))
        c1_scr[...] = s1 - jnp.max(s1, axis=0, keepdims=True)
        c2_scr[...] = s2 - jnp.max(s2, axis=0, keepdims=True)

        def lane_groups(i, carry):
            for u in range(_SEL_UNROLL):
                lane_group(h, pl.ds(pl.multiple_of((i * _SEL_UNROLL + u) * 128, 128), 128))
            return carry

        lax.fori_loop(0, tb // (128 * _SEL_UNROLL), lane_groups, 0)
        return carry

    lax.fori_loop(0, PEER_HEADS, head, 0)


def _peer_select(x1, g_ffn, wpq_bf, k_sub1, k_sub2, *, tb):
    T, D = x1.shape
    assert T % tb == 0 and tb % (128 * _SEL_UNROLL) == 0
    sel = lambda dt: jax.ShapeDtypeStruct((PEER_HEADS, PEER_NKEYS, T), dt)
    sel_spec = pl.BlockSpec((PEER_HEADS, PEER_NKEYS, tb), lambda i: (0, 0, i))
    const2 = lambda i: (0, 0)
    return pl.pallas_call(
        functools.partial(_peer_select_kernel, tb=tb),
        out_shape=(jax.ShapeDtypeStruct((T, D), _BF16), sel(jnp.uint32), sel(jnp.uint32), sel(_BF16), sel(_BF16)),
        grid=(T // tb,),
        in_specs=[
            pl.BlockSpec((tb, D), lambda i: (i, 0)),
            pl.BlockSpec((1, D), const2),
            pl.BlockSpec((D, PEER_HEADS * PEER_DQ), const2),
            pl.BlockSpec((PEER_NKEYS, PEER_DQ // 2), const2),
            pl.BlockSpec((PEER_NKEYS, PEER_DQ // 2), const2),
        ],
        out_specs=(pl.BlockSpec((tb, D), lambda i: (i, 0)),) + (sel_spec,) * 4,
        scratch_shapes=[
            pltpu.VMEM((tb, PEER_HEADS * PEER_DQ), _F32),
            pltpu.VMEM((PEER_NKEYS, tb), _F32),
            pltpu.VMEM((PEER_NKEYS, tb), _F32),
            pltpu.VMEM((PEER_TOPK, 128), _F32),
            pltpu.VMEM((PEER_TOPK, 128), _F32),
            pltpu.VMEM((PEER_TOPK, 128), _F32),
        ],
        compiler_params=pltpu.CompilerParams(
            dimension_semantics=("arbitrary",), vmem_limit_bytes=_VMEM_LIMIT),
        name="peer_select",
    )(x1, g_ffn.reshape(1, D), wpq_bf, k_sub1, k_sub2)


def _peer_dense_kernel(xn_ref, u_ref, vtp_ref, vtl_ref, cnt_ref, p1n_ref, rank_ref, p2_ref, x1_ref,
                       gfin_ref, y_ref, acc_scr, st_scr, h_scr, *, tb, rows):
    j = pl.program_id(1)
    cur = lax.rem(j, 2)

    @pl.when(j == 0)
    def _():
        acc_scr[...] = jnp.zeros_like(acc_scr)
        h_scr[1] = jnp.zeros(h_scr.shape[1:], _BF16)

    tile = (PEER_NKEYS, 128)
    words = (PEER_NKEYS // 2, 128)
    rb = 2
    for b in range(rows // rb):
        bs = slice(b * rb * PEER_NKEYS, (b + 1) * rb * PEER_NKEYS)
        st_scr[bs, :] = _dot_nt(u_ref[bs, :], xn_ref[...])
        for r in range(b * rb, (b + 1) * rb):
            es = slice(r * PEER_NKEYS, (r + 1) * PEER_NKEYS)
            for lg in range(tb // 128):
                ls = slice(lg * 128, (lg + 1) * 128)
                gate = jnp.zeros(tile, _BF16)
                for h in range(PEER_HEADS):
                    cnt = pltpu.bitcast(jnp.broadcast_to(cnt_ref[h, r:r + 1, ls], words), _BF16)
                    pn = pltpu.bitcast(jnp.broadcast_to(p1n_ref[h, r:r + 1, ls], words), _BF16)
                    sel = jnp.minimum(jnp.maximum(cnt - rank_ref[h, :, ls], 0), pn)
                    gate = gate + sel * p2_ref[h, :, ls]
                act = jax.nn.gelu(st_scr[es, ls].astype(_BF16))
                h_scr[cur, es, ls] = gate * act
    acc_scr[...] += _dot(vtp_ref[0], h_scr[1 - cur])

    @pl.when(j == pl.num_programs(1) - 1)
    def _():
        acc = acc_scr[...] + _dot(vtl_ref[0], h_scr[cur])
        x2 = x1_ref[...] + acc.T
        y_ref[...] = _rms(x2, gfin_ref[...])


def _peer_dense(xn, u_bf, vt_bf, cnt, p1n, rank2, p2, x1, g_final, *, tb, ec):
    T, D = x1.shape
    rows = ec // PEER_NKEYS
    nchunks = PEER_NEXP // ec
    assert T % tb == 0 and PEER_NEXP % ec == 0 and rows % 8 == 0
    sel_all = pl.BlockSpec((PEER_HEADS, PEER_NKEYS, tb), lambda i, j: (0, 0, i))
    sel_rows = pl.BlockSpec((PEER_HEADS, rows, tb), lambda i, j: (0, j, i))
    tok = pl.BlockSpec((tb, D), lambda i, j: (i, 0))
    return pl.pallas_call(
        functools.partial(_peer_dense_kernel, tb=tb, rows=rows),
        out_shape=jax.ShapeDtypeStruct((T, D), _F32),
        grid=(T // tb, nchunks),
        in_specs=[
            tok,
            pl.BlockSpec((ec, D), lambda i, j: (j, 0)),
            pl.BlockSpec((1, D, ec), lambda i, j: (jnp.maximum(j - 1, 0), 0, 0)),
            pl.BlockSpec((1, D, ec), lambda i, j: (nchunks - 1, 0, 0), pipeline_mode=pl.Buffered(1)),
            sel_rows, sel_rows, sel_all, sel_all,
            tok,
            pl.BlockSpec((1, D), lambda i, j: (0, 0)),
        ],
        out_specs=tok,
        scratch_shapes=[
            pltpu.VMEM((D, tb), _F32),
            pltpu.VMEM((ec, tb), _F32),
            pltpu.VMEM((2, ec, tb), _BF16),
        ],
        compiler_params=pltpu.CompilerParams(
            dimension_semantics=("arbitrary", "arbitrary"), vmem_limit_bytes=_VMEM_LIMIT),
        name="peer_dense",
    )(xn, u_bf, vt_bf, vt_bf, cnt, p1n, rank2, p2, x1, g_final.reshape(1, D))


def _mixer_weights(g_mix, w_in, b_in, g_mlh, g_sgu, w_s, b_s, w_out):
    nq = ML_HEADS * ML_DQK
    nv = ML_HEADS * ML_DV
    o_q, o_k, o_v = 0, nq, 2 * nq
    o_ig = o_v + nv
    o_fg = o_ig + ML_HEADS
    o_og = o_fg + ML_HEADS
    o_su = o_og + nv
    o_sv = o_su + D_MODEL
    o_xq = o_sv + D_MODEL
    o_gt = o_xq + D_MODEL
    gate = lambda b: slice(o_gt + b * D_MODEL, o_gt + (b + 1) * D_MODEL)
    main = lambda a: jnp.concatenate(
        [a[..., o_q:o_ig],
         a[..., o_og:o_su], a[..., gate(0)],
         a[..., o_su:o_xq], a[..., gate(1)],
         a[..., o_xq:o_gt], a[..., gate(2)]], axis=-1)
    w_if = w_in[:, o_ig:o_og]
    b_if = b_in[o_ig:o_og]
    return (
        g_mix.reshape(1, D_MODEL),
        main(w_in).astype(_BF16),
        main(b_in).reshape(1, _P_COLS),
        w_if.astype(_BF16),
        w_if.T.astype(_BF16),
        b_if.reshape(1, 8),
        b_if.reshape(8, 1),
        g_mlh.reshape(1, D_MODEL),
        g_sgu.reshape(1, D_MODEL),
        w_s,
        b_s.T,
        w_out.astype(_BF16),
    )


def kernel(x_prompt, x_sample, mem_prompt, cache_mem_k, cache_mem_v, state_mlstm_C, state_mlstm_n,
           state_mlstm_m, g_mix, w_in, b_in, g_mlh, g_sgu, w_s, b_s, g_mem, w_mk, w_mv, w_out, g_ffn,
           w_pq, k_sub1, k_sub2, u_exp, v_exp, g_final):
    depth = g_mix.shape[0]
    assert depth == 1
    l = 0
    Bp, Sp, D = x_prompt.shape
    Bs, Ss, _ = x_sample.shape

    wts = _mixer_weights(g_mix[l], w_in[l], b_in[l], g_mlh[l], g_sgu[l], w_s[l], b_s[l], w_out[l])

    mk, mv = _mem_kv(mem_prompt, g_mem[l], w_mk[l], w_mv[l])
    zc = jnp.zeros((Bp, ML_HEADS, ML_DQK, ML_DV), _F32)
    zn = jnp.zeros((Bp, ML_HEADS, ML_DQK), _F32)
    zm = jnp.zeros((Bp, ML_HEADS), _F32)
    x1p, cp, np_, mp, _ = _mixer(x_prompt, mk, mv, zc, zn, zm, wts, tc=256, want_vn=False)
    x1s, cs, ns, ms, vn = _mixer(
        x_sample, cache_mem_k[l].reshape(Bs, N_MEM, D), cache_mem_v[l].reshape(Bs, N_MEM, D),
        state_mlstm_C[l], state_mlstm_n[l], state_mlstm_m[l], wts, tc=Ss, want_vn=True)

    wpq_bf = w_pq[l].astype(_BF16)
    u_bf = u_exp[l].astype(_BF16)
    vt_bf = v_exp[l].astype(_BF16).reshape(PEER_NEXP // _PEER_EC, _PEER_EC, D).transpose(0, 2, 1)

    def peer(x1):
        B, S, _ = x1.shape
        x1 = x1.reshape(B * S, D)
        xn, cnt, p1n, rank2, p2 = _peer_select(x1, g_ffn[l], wpq_bf, k_sub1[l], k_sub2[l], tb=_PEER_TB)
        y = _peer_dense(xn, u_bf, vt_bf, cnt, p1n, rank2, p2, x1, g_final, tb=_PEER_TB, ec=_PEER_EC)
        return y.reshape(B, S, D)

    hs = (XA_HEADS, XA_DH)
    return (peer(x1p), peer(x1s), cp[None], np_[None], mp[None],
            mk.reshape(1, Bp, N_MEM, *hs), mv.reshape(1, Bp, N_MEM, *hs),
            cs[None], ns[None], ms[None], vn[None])
```

```python
import functools

import jax
import jax.numpy as jnp
from jax import lax
from jax.experimental import pallas as pl
from jax.experimental.pallas import tpu as pltpu

D_MODEL = 1024
EPS = 1e-6
CHUNK = 64
N_MEM = 256
ML_HEADS = 4
ML_DQK = 128
ML_DV = D_MODEL // ML_HEADS
SGU_CHUNK = 128
SGU_GROUPS = 4
SGU_GDIM = D_MODEL // SGU_GROUPS
XA_HEADS = 4
XA_DH = D_MODEL // XA_HEADS
PEER_HEADS = 8
PEER_NKEYS = 128
PEER_DQ = 256
PEER_TOPK = 16
PEER_NEXP = PEER_NKEYS * PEER_NKEYS

_SEG_A = ML_HEADS * (2 * ML_DQK + ML_DV)
_SEG_B = 2 * D_MODEL
_SEG_C = 3 * D_MODEL
_SEG_D = 2 * D_MODEL
_SEG0 = (0, _SEG_A, _SEG_A + _SEG_B, _SEG_A + _SEG_B + _SEG_C)
_P_COLS = _SEG_A + _SEG_B + _SEG_C + _SEG_D
_K0 = ML_HEADS * ML_DQK
_V0 = 2 * ML_HEADS * ML_DQK

_VMEM_LIMIT = 56 * 1024 * 1024

_MIXER_TC = 256
_PEER_TB = 512
_PEER_EC = 2048

_BF16 = jnp.bfloat16
_F32 = jnp.float32
_NEG_INF = float("-inf")


def _rms(xf, g):
    return xf * lax.rsqrt(jnp.mean(xf * xf, axis=-1, keepdims=True) + EPS) * g


def _dot(a, b):
    return jnp.dot(a, b, preferred_element_type=_F32)


def _dot_nt(a, b):
    return lax.dot_general(a, b, (((1,), (1,)), ((), ())), preferred_element_type=_F32)


def _bmm(a, b, ca, cb):
    return lax.dot_general(a, b, (((ca,), (cb,)), ((0,), (0,))), preferred_element_type=_F32)


def _mem_kv_kernel(mem_ref, g_ref, wk_ref, wv_ref, k_ref, v_ref):
    mn = _rms(mem_ref[0], g_ref[...]).astype(_BF16)
    k_ref[0] = _dot(mn, wk_ref[...])
    v_ref[0] = _dot(mn, wv_ref[...])


def _mem_kv(mem, g_mem, w_mk, w_mv):
    B = mem.shape[0]
    full = lambda b: (0, 0)
    return pl.pallas_call(
        _mem_kv_kernel,
        out_shape=(jax.ShapeDtypeStruct((B, N_MEM, D_MODEL), _F32),) * 2,
        grid=(B,),
        in_specs=[
            pl.BlockSpec((1, N_MEM, D_MODEL), lambda b: (b, 0, 0)),
            pl.BlockSpec((1, D_MODEL), full),
            pl.BlockSpec((D_MODEL, D_MODEL), full),
            pl.BlockSpec((D_MODEL, D_MODEL), full),
        ],
        out_specs=(pl.BlockSpec((1, N_MEM, D_MODEL), lambda b: (b, 0, 0)),) * 2,
        compiler_params=pltpu.CompilerParams(
            dimension_semantics=("arbitrary",), vmem_limit_bytes=_VMEM_LIMIT),
        name="mem_kv",
    )(mem, g_mem.reshape(1, D_MODEL), w_mk.astype(_BF16), w_mv.astype(_BF16))


def _mixer_kernel(x_ref, mk_ref, mv_ref, c0_ref, n0_ref, m0_ref, gmix_ref, win_ref, bin_ref,
                  wif_ref, wift_ref, bif_ref, bift_ref, gmlh_ref, gsgu_ref, ws_ref, bst_ref,
                  wout_ref,
                  x1_ref, c_out_ref, n_out_ref, m_out_ref, vn_ref,
                  pa_scr, pb_scr, pc_scr, pd_scr, mrg_scr, c_scr, n_scr, m_scr, *, tc, sgu_len):
    ci = pl.program_id(1)
    nsub = tc // CHUNK

    @pl.when(ci == 0)
    def _():
        c_scr[...] = c0_ref[0]
        n_scr[...] = n0_ref[0]
        m_scr[...] = m0_ref[0]

    x = x_ref[0]
    xn = _rms(x, gmix_ref[...]).astype(_BF16)

    def project(seg, out_scr):
        cols = slice(_SEG0[seg], _SEG0[seg] + out_scr.shape[1])
        out_scr[...] = _dot(xn, win_ref[:, cols]) + bin_ref[:, cols]

    project(0, pa_scr)
    gif = _dot(xn, wif_ref[...]) + bif_ref[...]
    gift = _dot_nt(wift_ref[...], xn) + bift_ref[...]

    row = lax.broadcasted_iota(jnp.int32, (CHUNK, CHUNK), 0)
    col = lax.broadcasted_iota(jnp.int32, (CHUNK, CHUNK), 1)
    tril = row >= col
    tril_f = tril.astype(_F32)
    triu_f = (row <= col).astype(_F32)
    stack = lambda f: jnp.stack([f(h) for h in range(ML_HEADS)])
    for j in range(nsub):
        r0 = j * CHUNK
        rows = slice(r0, r0 + CHUNK)
        ig_c = gif[rows, 0:ML_HEADS]
        lf_c = jax.nn.log_sigmoid(gif[rows, ML_HEADS:2 * ML_HEADS])
        ig_r = gift[0:ML_HEADS, rows]
        lf_r = jax.nn.log_sigmoid(gift[ML_HEADS:2 * ML_HEADS, rows])
        b_c = jnp.dot(tril_f, lf_c, precision=lax.Precision.HIGHEST,
                      preferred_element_type=_F32)
        b_r = jnp.dot(lf_r, triu_f, precision=lax.Precision.HIGHEST,
                      preferred_element_type=_F32)
        q = stack(lambda h: pa_scr[rows, h * ML_DQK:(h + 1) * ML_DQK]).astype(_BF16)
        k = stack(lambda h: pa_scr[rows, _K0 + h * ML_DQK:_K0 + (h + 1) * ML_DQK]) * (ML_DQK ** -0.5)
        v = stack(lambda h: pa_scr[rows, _V0 + h * ML_DV:_V0 + (h + 1) * ML_DV]).astype(_BF16)
        c_st = c_scr[...]
        n_st = stack(lambda h: n_scr[h:h + 1, :])
        m_st = stack(lambda h: m_scr[h:h + 1, 0:1])
        bc = stack(lambda h: b_c[:, h:h + 1])
        br = stack(lambda h: b_r[h:h + 1, :])
        ig_row = stack(lambda h: ig_r[h:h + 1, :])
        ig_col = stack(lambda h: ig_c[:, h:h + 1])
        logd = jnp.where(tril[None], bc - br + ig_row, _NEG_INF)
        log_prev = bc + m_st
        m_t = jnp.maximum(log_prev, jnp.max(logd, axis=2, keepdims=True))
        a = _bmm(q, k.astype(_BF16), 2, 2) * jnp.exp(logd - m_t)
        wp = jnp.exp(log_prev - m_t)
        num = _bmm(a.astype(_BF16), v, 2, 1) + wp * _bmm(q, c_st.astype(_BF16), 2, 1)
        qn = jnp.sum(q.astype(_F32) * n_st, axis=2, keepdims=True)
        den = jnp.sum(a, axis=2, keepdims=True) + wp * qn
        hh = num / jnp.maximum(jnp.abs(den), jnp.exp(-m_t))
        hn = hh * lax.rsqrt(jnp.mean(hh * hh, axis=2, keepdims=True) + EPS)
        for h in range(ML_HEADS):
            mrg_scr[rows, h * ML_DV:(h + 1) * ML_DV] = hn[h]
        b_last = bc[:, CHUNK - 1:CHUNK, :]
        log_in = b_last - bc + ig_col
        m_new = jnp.maximum(b_last + m_st, jnp.max(log_in, axis=1, keepdims=True))
        wi = jnp.exp(log_in - m_new)
        wc = jnp.exp(b_last + m_st - m_new)
        kw = wi * k
        c_scr[...] = wc * c_st + _bmm(kw.astype(_BF16), v, 1, 1)
        n_new = wc * n_st + jnp.sum(kw, axis=1, keepdims=True)
        for h in range(ML_HEADS):
            n_scr[h:h + 1, :] = n_new[h]
            m_scr[h:h + 1, :] = jnp.broadcast_to(m_new[h], (1, 128))

    project(1, pb_scr)
    h_a = jax.nn.sigmoid(pb_scr[:, 0:D_MODEL]) * mrg_scr[...] * gmlh_ref[...]
    mrg_scr[...] = jax.nn.sigmoid(pb_scr[:, D_MODEL:2 * D_MODEL]) * h_a

    project(2, pc_scr)
    sv = jax.nn.gelu(pc_scr[:, D_MODEL:2 * D_MODEL])
    vn = _rms(sv, gsgu_ref[...])
    if vn_ref is not None:
        vn_ref[0] = vn
    vnb = vn.astype(_BF16)
    rs = lax.broadcasted_iota(jnp.int32, (sgu_len, sgu_len), 0)
    cs = lax.broadcasted_iota(jnp.int32, (sgu_len, sgu_len), 1)
    for g in range(SGU_GROUPS):
        wsg = jnp.where(rs >= cs, ws_ref[g, 0:sgu_len, 0:sgu_len], 0.0).astype(_BF16)
        bsg = bst_ref[0:sgu_len, g:g + 1]
        gl = slice(g * SGU_GDIM, (g + 1) * SGU_GDIM)
        for c in range(tc // sgu_len):
            rows = slice(c * sgu_len, (c + 1) * sgu_len)
            mix = _dot(wsg, vnb[rows, gl]) + bsg
            u = jax.nn.gelu(pc_scr[rows, g * SGU_GDIM:(g + 1) * SGU_GDIM])
            g1 = jax.nn.sigmoid(pc_scr[rows, 2 * D_MODEL + g * SGU_GDIM:2 * D_MODEL + (g + 1) * SGU_GDIM])
            mrg_scr[rows, gl] += g1 * (u * mix)

    project(3, pd_scr)
    for h in range(XA_HEADS):
        hl = slice(h * XA_DH, (h + 1) * XA_DH)
        xq = pd_scr[:, h * XA_DH:(h + 1) * XA_DH].astype(_BF16)
        sc = _dot_nt(xq, mk_ref[0, :, hl].astype(_BF16)) * (XA_DH ** -0.5)
        sc = sc - jnp.max(sc, axis=1, keepdims=True)
        e = jnp.exp(sc)
        att = e / jnp.sum(e, axis=1, keepdims=True)
        h_c = _dot(att.astype(_BF16), mv_ref[0, :, hl].astype(_BF16))
        g2 = jax.nn.sigmoid(pd_scr[:, D_MODEL + h * XA_DH:D_MODEL + (h + 1) * XA_DH])
        mrg_scr[:, hl] += g2 * h_c

    x1_ref[0] = x + _dot(mrg_scr[...].astype(_BF16), wout_ref[...])

    @pl.when(ci == pl.num_programs(1) - 1)
    def _():
        c_out_ref[0] = c_scr[...]
        n_out_ref[0] = n_scr[...]
        m_out_ref[0] = m_scr[...]


def _mixer(x, mem_k, mem_v, c0, n0, m0, wts, *, tc, want_vn):
    B, S, D = x.shape
    sgu_len = min(S, SGU_CHUNK)
    assert S % tc == 0 and tc % CHUNK == 0 and tc % sgu_len == 0
    nchunks = S // tc
    m0p = jnp.broadcast_to(m0[:, :, None], (B, ML_HEADS, 128))
    m0p = jnp.concatenate([m0p, jnp.zeros((B, 8 - ML_HEADS, 128), _F32)], axis=1)

    def body(*refs):
        ins, rest = refs[:18], refs[18:]
        if want_vn:
            outs, scr = rest[:5], rest[5:]
        else:
            outs, scr = rest[:4] + (None,), rest[4:]
        _mixer_kernel(*ins, *outs, *scr, tc=tc, sgu_len=sgu_len)

    const2 = lambda b, c: (0, 0)
    const3 = lambda b, c: (0, 0, 0)
    per_b3 = lambda b, c: (b, 0, 0)
    per_b4 = lambda b, c: (b, 0, 0, 0)
    once = dict(pipeline_mode=pl.Buffered(1))
    in_specs = [
        pl.BlockSpec((1, tc, D), lambda b, c: (b, c, 0)),
        pl.BlockSpec((1, N_MEM, D), per_b3),
        pl.BlockSpec((1, N_MEM, D), per_b3),
        pl.BlockSpec((1, ML_HEADS, ML_DQK, ML_DV), per_b4),
        pl.BlockSpec((1, ML_HEADS, ML_DQK), per_b3),
        pl.BlockSpec((1, 8, 128), per_b3),
        pl.BlockSpec((1, D), const2, **once),
        pl.BlockSpec((D, _P_COLS), const2, **once),
        pl.BlockSpec((1, _P_COLS), const2, **once),
        pl.BlockSpec((D, 8), const2, **once),
        pl.BlockSpec((8, D), const2, **once),
        pl.BlockSpec((1, 8), const2, **once),
        pl.BlockSpec((8, 1), const2, **once),
        pl.BlockSpec((1, D), const2, **once),
        pl.BlockSpec((1, D), const2, **once),
        pl.BlockSpec((SGU_GROUPS, SGU_CHUNK, SGU_CHUNK), const3, **once),
        pl.BlockSpec((SGU_CHUNK, SGU_GROUPS), const2, **once),
        pl.BlockSpec((D, D), const2, **once),
    ]
    out_shape = [
        jax.ShapeDtypeStruct((B, S, D), _F32),
        jax.ShapeDtypeStruct((B, ML_HEADS, ML_DQK, ML_DV), _F32),
        jax.ShapeDtypeStruct((B, ML_HEADS, ML_DQK), _F32),
        jax.ShapeDtypeStruct((B, 8, 128), _F32),
    ]
    out_specs = [
        pl.BlockSpec((1, tc, D), lambda b, c: (b, c, 0)),
        pl.BlockSpec((1, ML_HEADS, ML_DQK, ML_DV), per_b4),
        pl.BlockSpec((1, ML_HEADS, ML_DQK), per_b3),
        pl.BlockSpec((1, 8, 128), per_b3),
    ]
    if want_vn:
        out_shape.append(jax.ShapeDtypeStruct((B, S, D), _F32))
        out_specs.append(pl.BlockSpec((1, tc, D), lambda b, c: (b, c, 0)))
    outs = pl.pallas_call(
        body,
        out_shape=tuple(out_shape),
        grid=(B, nchunks),
        in_specs=in_specs,
        out_specs=tuple(out_specs),
        scratch_shapes=[
            pltpu.VMEM((tc, _SEG_A), _F32),
            pltpu.VMEM((tc, _SEG_B), _F32),
            pltpu.VMEM((tc, _SEG_C), _F32),
            pltpu.VMEM((tc, _SEG_D), _F32),
            pltpu.VMEM((tc, D), _F32),
            pltpu.VMEM((ML_HEADS, ML_DQK, ML_DV), _F32),
            pltpu.VMEM((ML_HEADS, ML_DQK), _F32),
            pltpu.VMEM((8, 128), _F32),
        ],
        compiler_params=pltpu.CompilerParams(
            dimension_semantics=("arbitrary", "arbitrary"), vmem_limit_bytes=_VMEM_LIMIT),
        name="mixer",
    )(x, mem_k, mem_v, c0, n0, m0p, *wts)
    x1, c1, n1, m1p = outs[:4]
    vn = outs[4] if want_vn else None
    return x1, c1, n1, m1p[:, :ML_HEADS, 0], vn


def _batcher_pairs(n):
    pairs = []
    p = 1
    while p < n:
        k = p
        while k >= 1:
            for j in range(k % p, n - k, 2 * k):
                for i in range(min(k, n - j - k)):
                    if (i + j) // (2 * p) == (i + j + k) // (2 * p):
                        pairs.append((i + j, i + j + k))
            k //= 2
        p *= 2
    return pairs


_SORT16 = _batcher_pairs(PEER_NKEYS // 8)
_SEL_UNROLL = 2


def _sorted_columns(c):
    cols = [c[8 * k:8 * k + 8, :] for k in range(PEER_NKEYS // 8)]
    for i, j in _SORT16:
        cols[i], cols[j] = jnp.maximum(cols[i], cols[j]), jnp.minimum(cols[i], cols[j])
    return cols


def _pop_top(cols, out_scr):
    for r in range(PEER_TOPK):
        mx = jnp.max(cols[0], axis=0, keepdims=True)
        out_scr[r:r + 1, :] = mx
        left = PEER_TOPK - 1 - r
        if left:
            eq = cols[0] == mx
            for k in range(left):
                cols[k] = jnp.where(eq, cols[k + 1], cols[k])


def _bf16_pair(x, exact=False):
    u = pltpu.bitcast(x, jnp.uint32)
    if not exact:
        u = u + jnp.uint32(0x7FFF) + ((u >> 16) & jnp.uint32(1))
    hi = u & jnp.uint32(0xFFFF0000)
    return hi | (hi >> 16)


def _peer_select_kernel(x1_ref, gffn_ref, wpq_ref, k1_ref, k2_ref,
                        xn_ref, cnt_ref, p1n_ref, rank_ref, p2_ref,
                        q_scr, c1_scr, c2_scr, v1_scr, v2_scr, n_scr, *, tb):
    xn = _rms(x1_ref[...], gffn_ref[...]).astype(_BF16)
    xn_ref[...] = xn
    q_scr[...] = _dot(xn, wpq_ref[...])
    k1 = k1_ref[...].astype(_BF16)
    k2 = k2_ref[...].astype(_BF16)
    half = PEER_DQ // 2
    K = PEER_TOPK

    def lane_group(h, ls):
        c1 = c1_scr[:, ls]
        c2 = c2_scr[:, ls]
        _pop_top(_sorted_columns(c1), v1_scr)
        _pop_top(_sorted_columns(c2), v2_scr)
        v1 = v1_scr[...]
        lists = [v1_scr[0:8, :] + v2_scr[b:b + 1, :] for b in range(8)]
        ea = v1_scr[8:16, :] + v2_scr[0:1, :]
        eb = v1_scr[0:1, :] + v2_scr[8:16, :]
        cands = lists + [ea, eb]
        tau = None
        for r in range(K):
            tau = jnp.max(jnp.maximum(jnp.maximum(lists[0], ea), eb), axis=0, keepdims=True)
            left = K - 1 - r
            if left:
                eq = lists[0] == tau
                for k in range(min(left, 7)):
                    lists[k] = jnp.where(eq, lists[k + 1], lists[k])
                if left >= 8:
                    lists[7] = jnp.where(eq, _NEG_INF, lists[7])
                ea = jnp.where(ea == tau, _NEG_INF, ea)
                eb = jnp.where(eb == tau, _NEG_INF, eb)
        z = sum(jnp.sum(jnp.where(c >= tau, jnp.exp(c), 0.0), axis=0, keepdims=True) for c in cands)
        n = jnp.zeros(v1.shape, _F32)
        for b in range(K):
            n = jnp.where(v1 + v2_scr[b:b + 1, :] >= tau, float(b + 1), n)
        n_scr[...] = n
        cnt = jnp.zeros(c1.shape, _F32)
        rank2 = jnp.full(c2.shape, float(K), _F32)
        for a in range(K - 1, -1, -1):
            cnt = jnp.where(c1 >= v1_scr[a:a + 1, :], n_scr[a:a + 1, :], cnt)
            rank2 = jnp.where(c2 >= v2_scr[a:a + 1, :], float(a), rank2)
        cnt_ref[h, :, ls] = _bf16_pair(cnt, exact=True)
        p1n_ref[h, :, ls] = _bf16_pair(jnp.exp(c1) / z)
        rank_ref[h, :, ls] = rank2.astype(_BF16)
        p2_ref[h, :, ls] = jnp.exp(c2).astype(_BF16)

    def head(h, carry):
        q0 = pl.multiple_of(h * PEER_DQ, PEER_DQ)
        s1 = _dot_nt(k1, q_scr[:, pl.ds(q0, half)].astype(_BF16))
        s2 = _dot_nt(k2, q_scr[:, pl.ds(q0 + half, half)].astype(_BF16))
        c1_scr[...] = s1 - jnp.max(s1, axis=0, keepdims=True)
        c2_scr[...] = s2 - jnp.max(s2, axis=0, keepdims=True)

        def lane_groups(i, carry):
            for u in range(_SEL_UNROLL):
                lane_group(h, pl.ds(pl.multiple_of((i * _SEL_UNROLL + u) * 128, 128), 128))
            return carry

        lax.fori_loop(0, tb // (128 * _SEL_UNROLL), lane_groups, 0)
        return carry

    lax.fori_loop(0, PEER_HEADS, head, 0)


def _peer_select(x1, g_ffn, wpq_bf, k_sub1, k_sub2, *, tb):
    T, D = x1.shape
    assert T % tb == 0 and tb % (128 * _SEL_UNROLL) == 0
    sel = lambda dt: jax.ShapeDtypeStruct((PEER_HEADS, PEER_NKEYS, T), dt)
    sel_spec = pl.BlockSpec((PEER_HEADS, PEER_NKEYS, tb), lambda i: (0, 0, i))
    const2 = lambda i: (0, 0)
    return pl.pallas_call(
        functools.partial(_peer_select_kernel, tb=tb),
        out_shape=(jax.ShapeDtypeStruct((T, D), _BF16), sel(jnp.uint32), sel(jnp.uint32), sel(_BF16), sel(_BF16)),
        grid=(T // tb,),
        in_specs=[
            pl.BlockSpec((tb, D), lambda i: (i, 0)),
            pl.BlockSpec((1, D), const2),
            pl.BlockSpec((D, PEER_HEADS * PEER_DQ), const2),
            pl.BlockSpec((PEER_NKEYS, PEER_DQ // 2), const2),
            pl.BlockSpec((PEER_NKEYS, PEER_DQ // 2), const2),
        ],
        out_specs=(pl.BlockSpec((tb, D), lambda i: (i, 0)),) + (sel_spec,) * 4,
        scratch_shapes=[
            pltpu.VMEM((tb, PEER_HEADS * PEER_DQ), _F32),
            pltpu.VMEM((PEER_NKEYS, tb), _F32),
            pltpu.VMEM((PEER_NKEYS, tb), _F32),
            pltpu.VMEM((PEER_TOPK, 128), _F32),
            pltpu.VMEM((PEER_TOPK, 128), _F32),
            pltpu.VMEM((PEER_TOPK, 128), _F32),
        ],
        compiler_params=pltpu.CompilerParams(
            dimension_semantics=("arbitrary",), vmem_limit_bytes=_VMEM_LIMIT),
        name="peer_select",
    )(x1, g_ffn.reshape(1, D), wpq_bf, k_sub1, k_sub2)


def _peer_dense_kernel(xn_ref, u_ref, vtp_ref, vtl_ref, cnt_ref, p1n_ref, rank_ref, p2_ref, x1_ref,
                       gfin_ref, y_ref, acc_scr, st_scr, h_scr, *, tb, rows):
    j = pl.program_id(1)
    cur = lax.rem(j, 2)

    @pl.when(j == 0)
    def _():
        acc_scr[...] = jnp.zeros_like(acc_scr)
        h_scr[1] = jnp.zeros(h_scr.shape[1:], _BF16)

    tile = (PEER_NKEYS, 128)
    words = (PEER_NKEYS // 2, 128)
    rb = 2
    for b in range(rows // rb):
        bs = slice(b * rb * PEER_NKEYS, (b + 1) * rb * PEER_NKEYS)
        st_scr[bs, :] = _dot_nt(u_ref[bs, :], xn_ref[...]).astype(_BF16)
        for r in range(b * rb, (b + 1) * rb):
            es = slice(r * PEER_NKEYS, (r + 1) * PEER_NKEYS)
            for lg in range(tb // 128):
                ls = slice(lg * 128, (lg + 1) * 128)
                gate = jnp.zeros(tile, _BF16)
                for h in range(PEER_HEADS):
                    cnt = pltpu.bitcast(jnp.broadcast_to(cnt_ref[h, r:r + 1, ls], words), _BF16)
                    pn = pltpu.bitcast(jnp.broadcast_to(p1n_ref[h, r:r + 1, ls], words), _BF16)
                    sel = jnp.minimum(jnp.maximum(cnt - rank_ref[h, :, ls], 0), pn)
                    gate = gate + sel * p2_ref[h, :, ls]
                act = jax.nn.gelu(st_scr[es, ls])
                h_scr[cur, es, ls] = gate * act
    acc_scr[...] += _dot(vtp_ref[0], h_scr[1 - cur])

    @pl.when(j == pl.num_programs(1) - 1)
    def _():
        acc = acc_scr[...] + _dot(vtl_ref[0], h_scr[cur])
        x2 = x1_ref[...] + acc.T
        y_ref[...] = _rms(x2, gfin_ref[...])


def _peer_dense(xn, u_bf, vt_bf, cnt, p1n, rank2, p2, x1, g_final, *, tb, ec):
    T, D = x1.shape
    rows = ec // PEER_NKEYS
    nchunks = PEER_NEXP // ec
    assert T % tb == 0 and PEER_NEXP % ec == 0 and rows % 8 == 0
    sel_all = pl.BlockSpec((PEER_HEADS, PEER_NKEYS, tb), lambda i, j: (0, 0, i))
    sel_rows = pl.BlockSpec((PEER_HEADS, rows, tb), lambda i, j: (0, j, i))
    tok = pl.BlockSpec((tb, D), lambda i, j: (i, 0))
    return pl.pallas_call(
        functools.partial(_peer_dense_kernel, tb=tb, rows=rows),
        out_shape=jax.ShapeDtypeStruct((T, D), _F32),
        grid=(T // tb, nchunks),
        in_specs=[
            tok,
            pl.BlockSpec((ec, D), lambda i, j: (j, 0)),
            pl.BlockSpec((1, D, ec), lambda i, j: (jnp.maximum(j - 1, 0), 0, 0)),
            pl.BlockSpec((1, D, ec), lambda i, j: (nchunks - 1, 0, 0), pipeline_mode=pl.Buffered(1)),
            sel_rows, sel_rows, sel_all, sel_all,
            tok,
            pl.BlockSpec((1, D), lambda i, j: (0, 0)),
        ],
        out_specs=tok,
        scratch_shapes=[
            pltpu.VMEM((D, tb), _F32),
            pltpu.VMEM((ec, tb), _BF16),
            pltpu.VMEM((2, ec, tb), _BF16),
        ],
        compiler_params=pltpu.CompilerParams(
            dimension_semantics=("arbitrary", "arbitrary"), vmem_limit_bytes=_VMEM_LIMIT),
        name="peer_dense",
    )(xn, u_bf, vt_bf, vt_bf, cnt, p1n, rank2, p2, x1, g_final.reshape(1, D))


def _mixer_weights(g_mix, w_in, b_in, g_mlh, g_sgu, w_s, b_s, w_out):
    nq = ML_HEADS * ML_DQK
    nv = ML_HEADS * ML_DV
    o_q, o_k, o_v = 0, nq, 2 * nq
    o_ig = o_v + nv
    o_fg = o_ig + ML_HEADS
    o_og = o_fg + ML_HEADS
    o_su = o_og + nv
    o_sv = o_su + D_MODEL
    o_xq = o_sv + D_MODEL
    o_gt = o_xq + D_MODEL
    gate = lambda b: slice(o_gt + b * D_MODEL, o_gt + (b + 1) * D_MODEL)
    main = lambda a: jnp.concatenate(
        [a[..., o_q:o_ig],
         a[..., o_og:o_su], a[..., gate(0)],
         a[..., o_su:o_xq], a[..., gate(1)],
         a[..., o_xq:o_gt], a[..., gate(2)]], axis=-1)
    w_if = w_in[:, o_ig:o_og]
    b_if = b_in[o_ig:o_og]
    return (
        g_mix.reshape(1, D_MODEL),
        main(w_in).astype(_BF16),
        main(b_in).reshape(1, _P_COLS),
        w_if.astype(_BF16),
        w_if.T.astype(_BF16),
        b_if.reshape(1, 8),
        b_if.reshape(8, 1),
        g_mlh.reshape(1, D_MODEL),
        g_sgu.reshape(1, D_MODEL),
        w_s,
        b_s.T,
        w_out.astype(_BF16),
    )


def kernel(x_prompt, x_sample, mem_prompt, cache_mem_k, cache_mem_v, state_mlstm_C, state_mlstm_n,
           state_mlstm_m, g_mix, w_in, b_in, g_mlh, g_sgu, w_s, b_s, g_mem, w_mk, w_mv, w_out, g_ffn,
           w_pq, k_sub1, k_sub2, u_exp, v_exp, g_final):
    depth = g_mix.shape[0]
    assert depth == 1
    l = 0
    Bp, Sp, D = x_prompt.shape
    Bs, Ss, _ = x_sample.shape

    wts = _mixer_weights(g_mix[l], w_in[l], b_in[l], g_mlh[l], g_sgu[l], w_s[l], b_s[l], w_out[l])

    mk, mv = _mem_kv(mem_prompt, g_mem[l], w_mk[l], w_mv[l])
    zc = jnp.zeros((Bp, ML_HEADS, ML_DQK, ML_DV), _F32)
    zn = jnp.zeros((Bp, ML_HEADS, ML_DQK), _F32)
    zm = jnp.zeros((Bp, ML_HEADS), _F32)
    x1p, cp, np_, mp, _ = _mixer(x_prompt, mk, mv, zc, zn, zm, wts,
                                 tc=min(_MIXER_TC, Sp), want_vn=False)
    x1s, cs, ns, ms, vn = _mixer(
        x_sample, cache_mem_k[l].reshape(Bs, N_MEM, D), cache_mem_v[l].reshape(Bs, N_MEM, D),
        state_mlstm_C[l], state_mlstm_n[l], state_mlstm_m[l], wts,
        tc=min(_MIXER_TC, Ss), want_vn=True)

    wpq_bf = w_pq[l].astype(_BF16)
    u_bf = u_exp[l].astype(_BF16)
    vt_bf = v_exp[l].astype(_BF16).reshape(PEER_NEXP // _PEER_EC, _PEER_EC, D).transpose(0, 2, 1)

    def peer(x1):
        B, S, _ = x1.shape
        x1 = x1.reshape(B * S, D)
        xn, cnt, p1n, rank2, p2 = _peer_select(x1, g_ffn[l], wpq_bf, k_sub1[l], k_sub2[l], tb=_PEER_TB)
        y = _peer_dense(xn, u_bf, vt_bf, cnt, p1n, rank2, p2, x1, g_final, tb=_PEER_TB, ec=_PEER_EC)
        return y.reshape(B, S, D)

    hs = (XA_HEADS, XA_DH)
    return (peer(x1p), peer(x1s), cp[None], np_[None], mp[None],
            mk.reshape(1, Bp, N_MEM, *hs), mv.reshape(1, Bp, N_MEM, *hs),
            cs[None], ns[None], ms[None], vn[None])
```

```python
import functools

import jax
import jax.numpy as jnp
from jax import lax
from jax.experimental import pallas as pl
from jax.experimental.pallas import tpu as pltpu

D_MODEL = 1024
EPS = 1e-6
CHUNK = 64
N_MEM = 256
ML_HEADS = 4
ML_DQK = 128
ML_DV = D_MODEL // ML_HEADS
SGU_CHUNK = 128
SGU_GROUPS = 4
SGU_GDIM = D_MODEL // SGU_GROUPS
XA_HEADS = 4
XA_DH = D_MODEL // XA_HEADS
PEER_HEADS = 8
PEER_NKEYS = 128
PEER_DQ = 256
PEER_TOPK = 16
PEER_NEXP = PEER_NKEYS * PEER_NKEYS

_SEG_A = ML_HEADS * (2 * ML_DQK + ML_DV)
_SEG_B = 2 * D_MODEL
_SEG_C = 3 * D_MODEL
_SEG_D = 2 * D_MODEL
_SEG0 = (0, _SEG_A, _SEG_A + _SEG_B, _SEG_A + _SEG_B + _SEG_C)
_P_COLS = _SEG_A + _SEG_B + _SEG_C + _SEG_D
_K0 = ML_HEADS * ML_DQK
_V0 = 2 * ML_HEADS * ML_DQK

_VMEM_LIMIT = 56 * 1024 * 1024

_MIXER_TC = 256
_PEER_TB = 512
_PEER_EC = 2048

_BF16 = jnp.bfloat16
_F32 = jnp.float32
_NEG_INF = float("-inf")


def _rms(xf, g):
    return xf * lax.rsqrt(jnp.mean(xf * xf, axis=-1, keepdims=True) + EPS) * g


def _dot(a, b):
    return jnp.dot(a, b, preferred_element_type=_F32)


def _dot_nt(a, b):
    return lax.dot_general(a, b, (((1,), (1,)), ((), ())), preferred_element_type=_F32)


def _bmm(a, b, ca, cb):
    return lax.dot_general(a, b, (((ca,), (cb,)), ((0,), (0,))), preferred_element_type=_F32)


def _mem_kv_kernel(mem_ref, g_ref, wk_ref, wv_ref, k_ref, v_ref):
    mn = _rms(mem_ref[0], g_ref[...]).astype(_BF16)
    k_ref[0] = _dot(mn, wk_ref[...])
    v_ref[0] = _dot(mn, wv_ref[...])


def _mem_kv(mem, g_mem, w_mk, w_mv):
    B = mem.shape[0]
    full = lambda b: (0, 0)
    return pl.pallas_call(
        _mem_kv_kernel,
        out_shape=(jax.ShapeDtypeStruct((B, N_MEM, D_MODEL), _F32),) * 2,
        grid=(B,),
        in_specs=[
            pl.BlockSpec((1, N_MEM, D_MODEL), lambda b: (b, 0, 0)),
            pl.BlockSpec((1, D_MODEL), full),
            pl.BlockSpec((D_MODEL, D_MODEL), full),
            pl.BlockSpec((D_MODEL, D_MODEL), full),
        ],
        out_specs=(pl.BlockSpec((1, N_MEM, D_MODEL), lambda b: (b, 0, 0)),) * 2,
        compiler_params=pltpu.CompilerParams(
            dimension_semantics=("arbitrary",), vmem_limit_bytes=_VMEM_LIMIT),
        name="mem_kv",
    )(mem, g_mem.reshape(1, D_MODEL), w_mk.astype(_BF16), w_mv.astype(_BF16))


def _mixer_kernel(x_ref, mk_ref, mv_ref, c0_ref, n0_ref, m0_ref, gmix_ref, win_ref, bin_ref,
                  wif_ref, wift_ref, bif_ref, bift_ref, gmlh_ref, gsgu_ref, ws_ref, bst_ref,
                  wout_ref,
                  x1_ref, c_out_ref, n_out_ref, m_out_ref, vn_ref,
                  pa_scr, pb_scr, pc_scr, pd_scr, mrg_scr, c_scr, n_scr, m_scr, *, tc, sgu_len):
    ci = pl.program_id(1)
    nsub = tc // CHUNK

    @pl.when(ci == 0)
    def _():
        c_scr[...] = c0_ref[0]
        n_scr[...] = n0_ref[0]
        m_scr[...] = m0_ref[0]

    x = x_ref[0]
    xn = _rms(x, gmix_ref[...]).astype(_BF16)

    def project(seg, out_scr):
        cols = slice(_SEG0[seg], _SEG0[seg] + out_scr.shape[1])
        out_scr[...] = _dot(xn, win_ref[:, cols]) + bin_ref[:, cols]

    project(0, pa_scr)
    gif = _dot(xn, wif_ref[...]) + bif_ref[...]
    gift = _dot_nt(wift_ref[...], xn) + bift_ref[...]

    row = lax.broadcasted_iota(jnp.int32, (CHUNK, CHUNK), 0)
    col = lax.broadcasted_iota(jnp.int32, (CHUNK, CHUNK), 1)
    tril = row >= col
    tril_f = tril.astype(_F32)
    triu_f = (row <= col).astype(_F32)
    stack = lambda f: jnp.stack([f(h) for h in range(ML_HEADS)])
    for j in range(nsub):
        r0 = j * CHUNK
        rows = slice(r0, r0 + CHUNK)
        ig_c = gif[rows, 0:ML_HEADS]
        lf_c = jax.nn.log_sigmoid(gif[rows, ML_HEADS:2 * ML_HEADS])
        ig_r = gift[0:ML_HEADS, rows]
        lf_r = jax.nn.log_sigmoid(gift[ML_HEADS:2 * ML_HEADS, rows])
        b_c = jnp.dot(tril_f, lf_c, precision=lax.Precision.HIGHEST,
                      preferred_element_type=_F32)
        b_r = jnp.dot(lf_r, triu_f, precision=lax.Precision.HIGHEST,
                      preferred_element_type=_F32)
        q = stack(lambda h: pa_scr[rows, h * ML_DQK:(h + 1) * ML_DQK]).astype(_BF16)
        k = stack(lambda h: pa_scr[rows, _K0 + h * ML_DQK:_K0 + (h + 1) * ML_DQK]) * (ML_DQK ** -0.5)
        v = stack(lambda h: pa_scr[rows, _V0 + h * ML_DV:_V0 + (h + 1) * ML_DV]).astype(_BF16)
        c_st = c_scr[...]
        n_st = stack(lambda h: n_scr[h:h + 1, :])
        m_st = stack(lambda h: m_scr[h:h + 1, 0:1])
        bc = stack(lambda h: b_c[:, h:h + 1])
        br = stack(lambda h: b_r[h:h + 1, :])
        ig_row = stack(lambda h: ig_r[h:h + 1, :])
        ig_col = stack(lambda h: ig_c[:, h:h + 1])
        logd = jnp.where(tril[None], bc - br + ig_row, _NEG_INF)
        log_prev = bc + m_st
        m_t = jnp.maximum(log_prev, jnp.max(logd, axis=2, keepdims=True))
        a = _bmm(q, k.astype(_BF16), 2, 2) * jnp.exp(logd - m_t)
        wp = jnp.exp(log_prev - m_t)
        num = _bmm(a.astype(_BF16), v, 2, 1) + wp * _bmm(q, c_st.astype(_BF16), 2, 1)
        qn = jnp.sum(q.astype(_F32) * n_st, axis=2, keepdims=True)
        den = jnp.sum(a, axis=2, keepdims=True) + wp * qn
        hh = num / jnp.maximum(jnp.abs(den), jnp.exp(-m_t))
        hn = hh * lax.rsqrt(jnp.mean(hh * hh, axis=2, keepdims=True) + EPS)
        for h in range(ML_HEADS):
            mrg_scr[rows, h * ML_DV:(h + 1) * ML_DV] = hn[h]
        b_last = bc[:, CHUNK - 1:CHUNK, :]
        log_in = b_last - bc + ig_col
        m_new = jnp.maximum(b_last + m_st, jnp.max(log_in, axis=1, keepdims=True))
        wi = jnp.exp(log_in - m_new)
        wc = jnp.exp(b_last + m_st - m_new)
        kw = wi * k
        c_scr[...] = wc * c_st + _bmm(kw.astype(_BF16), v, 1, 1)
        n_new = wc * n_st + jnp.sum(kw, axis=1, keepdims=True)
        for h in range(ML_HEADS):
            n_scr[h:h + 1, :] = n_new[h]
            m_scr[h:h + 1, :] = jnp.broadcast_to(m_new[h], (1, 128))

    project(1, pb_scr)
    h_a = jax.nn.sigmoid(pb_scr[:, 0:D_MODEL]) * mrg_scr[...] * gmlh_ref[...]
    mrg_scr[...] = jax.nn.sigmoid(pb_scr[:, D_MODEL:2 * D_MODEL]) * h_a

    project(2, pc_scr)
    sv = jax.nn.gelu(pc_scr[:, D_MODEL:2 * D_MODEL])
    vn = _rms(sv, gsgu_ref[...])
    if vn_ref is not None:
        vn_ref[0] = vn
    vnb = vn.astype(_BF16)
    rs = lax.broadcasted_iota(jnp.int32, (sgu_len, sgu_len), 0)
    cs = lax.broadcasted_iota(jnp.int32, (sgu_len, sgu_len), 1)
    for g in range(SGU_GROUPS):
        wsg = jnp.where(rs >= cs, ws_ref[g, 0:sgu_len, 0:sgu_len], 0.0).astype(_BF16)
        bsg = bst_ref[0:sgu_len, g:g + 1]
        gl = slice(g * SGU_GDIM, (g + 1) * SGU_GDIM)
        for c in range(tc // sgu_len):
            rows = slice(c * sgu_len, (c + 1) * sgu_len)
            mix = _dot(wsg, vnb[rows, gl]) + bsg
            u = jax.nn.gelu(pc_scr[rows, g * SGU_GDIM:(g + 1) * SGU_GDIM])
            g1 = jax.nn.sigmoid(pc_scr[rows, 2 * D_MODEL + g * SGU_GDIM:2 * D_MODEL + (g + 1) * SGU_GDIM])
            mrg_scr[rows, gl] += g1 * (u * mix)

    project(3, pd_scr)
    for h in range(XA_HEADS):
        hl = slice(h * XA_DH, (h + 1) * XA_DH)
        xq = pd_scr[:, h * XA_DH:(h + 1) * XA_DH].astype(_BF16)
        sc = _dot_nt(xq, mk_ref[0, :, hl].astype(_BF16)) * (XA_DH ** -0.5)
        sc = sc - jnp.max(sc, axis=1, keepdims=True)
        e = jnp.exp(sc)
        att = e / jnp.sum(e, axis=1, keepdims=True)
        h_c = _dot(att.astype(_BF16), mv_ref[0, :, hl].astype(_BF16))
        g2 = jax.nn.sigmoid(pd_scr[:, D_MODEL + h * XA_DH:D_MODEL + (h + 1) * XA_DH])
        mrg_scr[:, hl] += g2 * h_c

    x1_ref[0] = x + _dot(mrg_scr[...].astype(_BF16), wout_ref[...])

    @pl.when(ci == pl.num_programs(1) - 1)
    def _():
        c_out_ref[0] = c_scr[...]
        n_out_ref[0] = n_scr[...]
        m_out_ref[0] = m_scr[...]


def _mixer(x, mem_k, mem_v, c0, n0, m0, wts, *, tc, want_vn):
    B, S, D = x.shape
    sgu_len = min(S, SGU_CHUNK)
    assert S % tc == 0 and tc % CHUNK == 0 and tc % sgu_len == 0
    nchunks = S // tc
    m0p = jnp.broadcast_to(m0[:, :, None], (B, ML_HEADS, 128))
    m0p = jnp.concatenate([m0p, jnp.zeros((B, 8 - ML_HEADS, 128), _F32)], axis=1)

    def body(*refs):
        ins, rest = refs[:18], refs[18:]
        if want_vn:
            outs, scr = rest[:5], rest[5:]
        else:
            outs, scr = rest[:4] + (None,), rest[4:]
        _mixer_kernel(*ins, *outs, *scr, tc=tc, sgu_len=sgu_len)

    const2 = lambda b, c: (0, 0)
    const3 = lambda b, c: (0, 0, 0)
    per_b3 = lambda b, c: (b, 0, 0)
    per_b4 = lambda b, c: (b, 0, 0, 0)
    once = dict(pipeline_mode=pl.Buffered(1))
    in_specs = [
        pl.BlockSpec((1, tc, D), lambda b, c: (b, c, 0)),
        pl.BlockSpec((1, N_MEM, D), per_b3),
        pl.BlockSpec((1, N_MEM, D), per_b3),
        pl.BlockSpec((1, ML_HEADS, ML_DQK, ML_DV), per_b4),
        pl.BlockSpec((1, ML_HEADS, ML_DQK), per_b3),
        pl.BlockSpec((1, 8, 128), per_b3),
        pl.BlockSpec((1, D), const2, **once),
        pl.BlockSpec((D, _P_COLS), const2, **once),
        pl.BlockSpec((1, _P_COLS), const2, **once),
        pl.BlockSpec((D, 8), const2, **once),
        pl.BlockSpec((8, D), const2, **once),
        pl.BlockSpec((1, 8), const2, **once),
        pl.BlockSpec((8, 1), const2, **once),
        pl.BlockSpec((1, D), const2, **once),
        pl.BlockSpec((1, D), const2, **once),
        pl.BlockSpec((SGU_GROUPS, SGU_CHUNK, SGU_CHUNK), const3, **once),
        pl.BlockSpec((SGU_CHUNK, SGU_GROUPS), const2, **once),
        pl.BlockSpec((D, D), const2, **once),
    ]
    out_shape = [
        jax.ShapeDtypeStruct((B, S, D), _F32),
        jax.ShapeDtypeStruct((B, ML_HEADS, ML_DQK, ML_DV), _F32),
        jax.ShapeDtypeStruct((B, ML_HEADS, ML_DQK), _F32),
        jax.ShapeDtypeStruct((B, 8, 128), _F32),
    ]
    out_specs = [
        pl.BlockSpec((1, tc, D), lambda b, c: (b, c, 0)),
        pl.BlockSpec((1, ML_HEADS, ML_DQK, ML_DV), per_b4),
        pl.BlockSpec((1, ML_HEADS, ML_DQK), per_b3),
        pl.BlockSpec((1, 8, 128), per_b3),
    ]
    if want_vn:
        out_shape.append(jax.ShapeDtypeStruct((B, S, D), _F32))
        out_specs.append(pl.BlockSpec((1, tc, D), lambda b, c: (b, c, 0)))
    outs = pl.pallas_call(
        body,
        out_shape=tuple(out_shape),
        grid=(B, nchunks),
        in_specs=in_specs,
        out_specs=tuple(out_specs),
        scratch_shapes=[
            pltpu.VMEM((tc, _SEG_A), _F32),
            pltpu.VMEM((tc, _SEG_B), _F32),
            pltpu.VMEM((tc, _SEG_C), _F32),
            pltpu.VMEM((tc, _SEG_D), _F32),
            pltpu.VMEM((tc, D), _F32),
            pltpu.VMEM((ML_HEADS, ML_DQK, ML_DV), _F32),
            pltpu.VMEM((ML_HEADS, ML_DQK), _F32),
            pltpu.VMEM((8, 128), _F32),
        ],
        compiler_params=pltpu.CompilerParams(
            dimension_semantics=("arbitrary", "arbitrary"), vmem_limit_bytes=_VMEM_LIMIT),
        name="mixer",
    )(x, mem_k, mem_v, c0, n0, m0p, *wts)
    x1, c1, n1, m1p = outs[:4]
    vn = outs[4] if want_vn else None
    return x1, c1, n1, m1p[:, :ML_HEADS, 0], vn


def _batcher_pairs(n):
    pairs = []
    p = 1
    while p < n:
        k = p
        while k >= 1:
            for j in range(k % p, n - k, 2 * k):
                for i in range(min(k, n - j - k)):
                    if (i + j) // (2 * p) == (i + j + k) // (2 * p):
                        pairs.append((i + j, i + j + k))
            k //= 2
        p *= 2
    return pairs


_SORT16 = _batcher_pairs(PEER_NKEYS // 8)
_SEL_UNROLL = 2


def _sorted_columns(c):
    cols = [c[8 * k:8 * k + 8, :] for k in range(PEER_NKEYS // 8)]
    for i, j in _SORT16:
        cols[i], cols[j] = jnp.maximum(cols[i], cols[j]), jnp.minimum(cols[i], cols[j])
    return cols


def _pop_top(cols, out_scr):
    for r in range(PEER_TOPK):
        mx = jnp.max(cols[0], axis=0, keepdims=True)
        out_scr[r:r + 1, :] = mx
        left = PEER_TOPK - 1 - r
        if left:
            eq = cols[0] == mx
            for k in range(left):
                cols[k] = jnp.where(eq, cols[k + 1], cols[k])


def _bf16_pair(x, exact=False):
    u = pltpu.bitcast(x, jnp.uint32)
    if not exact:
        u = u + jnp.uint32(0x7FFF) + ((u >> 16) & jnp.uint32(1))
    hi = u & jnp.uint32(0xFFFF0000)
    return hi | (hi >> 16)


def _peer_select_kernel(x1_ref, gffn_ref, wpq_ref, k1_ref, k2_ref,
                        xn_ref, cnt_ref, p1n_ref, rank_ref, p2_ref,
                        q_scr, c1_scr, c2_scr, v1_scr, v2_scr, n_scr, *, tb):
    xn = _rms(x1_ref[...], gffn_ref[...]).astype(_BF16)
    xn_ref[...] = xn
    q_scr[...] = _dot(xn, wpq_ref[...])
    k1 = k1_ref[...].astype(_BF16)
    k2 = k2_ref[...].astype(_BF16)
    half = PEER_DQ // 2
    K = PEER_TOPK

    def lane_group(h, ls):
        c1 = c1_scr[:, ls]
        c2 = c2_scr[:, ls]
        _pop_top(_sorted_columns(c1), v1_scr)
        _pop_top(_sorted_columns(c2), v2_scr)
        v1 = v1_scr[...]
        lists = [v1_scr[0:8, :] + v2_scr[b:b + 1, :] for b in range(8)]
        ea = v1_scr[8:16, :] + v2_scr[0:1, :]
        eb = v1_scr[0:1, :] + v2_scr[8:16, :]
        cands = lists + [ea, eb]
        tau = None
        for r in range(K):
            tau = jnp.max(jnp.maximum(jnp.maximum(lists[0], ea), eb), axis=0, keepdims=True)
            left = K - 1 - r
            if left:
                eq = lists[0] == tau
                for k in range(min(left, 7)):
                    lists[k] = jnp.where(eq, lists[k + 1], lists[k])
                if left >= 8:
                    lists[7] = jnp.where(eq, _NEG_INF, lists[7])
                ea = jnp.where(ea == tau, _NEG_INF, ea)
                eb = jnp.where(eb == tau, _NEG_INF, eb)
        z = sum(jnp.sum(jnp.where(c >= tau, jnp.exp(c), 0.0), axis=0, keepdims=True) for c in cands)
        n = jnp.zeros(v1.shape, _F32)
        for b in range(K):
            n = jnp.where(v1 + v2_scr[b:b + 1, :] >= tau, float(b + 1), n)
        n_scr[...] = n
        cnt = jnp.zeros(c1.shape, _F32)
        rank2 = jnp.full(c2.shape, float(K), _F32)
        for a in range(K - 1, -1, -1):
            cnt = jnp.where(c1 >= v1_scr[a:a + 1, :], n_scr[a:a + 1, :], cnt)
            rank2 = jnp.where(c2 >= v2_scr[a:a + 1, :], float(a), rank2)
        cnt_ref[h, :, ls] = _bf16_pair(cnt, exact=True)
        p1n_ref[h, :, ls] = _bf16_pair(jnp.exp(c1) / z)
        rank_ref[h, :, ls] = rank2.astype(_BF16)
        p2_ref[h, :, ls] = jnp.exp(c2).astype(_BF16)

    def head(h, carry):
        q0 = pl.multiple_of(h * PEER_DQ, PEER_DQ)
        s1 = _dot_nt(k1, q_scr[:, pl.ds(q0, half)].astype(_BF16))
        s2 = _dot_nt(k2, q_scr[:, pl.ds(q0 + half, half)].astype(_BF16))
        c1_scr[...] = s1 - jnp.max(s1, axis=0, keepdims=True)
        c2_scr[...] = s2 - jnp.max(s2, axis=0, keepdims=True)

        def lane_groups(i, carry):
            for u in range(_SEL_UNROLL):
                lane_group(h, pl.ds(pl.multiple_of((i * _SEL_UNROLL + u) * 128, 128), 128))
            return carry

        lax.fori_loop(0, tb // (128 * _SEL_UNROLL), lane_groups, 0)
        return carry

    lax.fori_loop(0, PEER_HEADS, head, 0)


def _peer_select(x1, g_ffn, wpq_bf, k_sub1, k_sub2, *, tb):
    T, D = x1.shape
    assert T % tb == 0 and tb % (128 * _SEL_UNROLL) == 0
    sel = lambda dt: jax.ShapeDtypeStruct((PEER_HEADS, PEER_NKEYS, T), dt)
    sel_spec = pl.BlockSpec((PEER_HEADS, PEER_NKEYS, tb), lambda i: (0, 0, i))
    const2 = lambda i: (0, 0)
    return pl.pallas_call(
        functools.partial(_peer_select_kernel, tb=tb),
        out_shape=(jax.ShapeDtypeStruct((T, D), _BF16), sel(jnp.uint32), sel(jnp.uint32), sel(_BF16), sel(_BF16)),
        grid=(T // tb,),
        in_specs=[
            pl.BlockSpec((tb, D), lambda i: (i, 0)),
            pl.BlockSpec((1, D), const2),
            pl.BlockSpec((D, PEER_HEADS * PEER_DQ), const2),
            pl.BlockSpec((PEER_NKEYS, PEER_DQ // 2), const2),
            pl.BlockSpec((PEER_NKEYS, PEER_DQ // 2), const2),
        ],
        out_specs=(pl.BlockSpec((tb, D), lambda i: (i, 0)),) + (sel_spec,) * 4,
        scratch_shapes=[
            pltpu.VMEM((tb, PEER_HEADS * PEER_DQ), _F32),
            pltpu.VMEM((PEER_NKEYS, tb), _F32),
            pltpu.VMEM((PEER_NKEYS, tb), _F32),
            pltpu.VMEM((PEER_TOPK, 128), _F32),
            pltpu.VMEM((PEER_TOPK, 128), _F32),
            pltpu.VMEM((PEER_TOPK, 128), _F32),
        ],
        compiler_params=pltpu.CompilerParams(
            dimension_semantics=("arbitrary",), vmem_limit_bytes=_VMEM_LIMIT),
        name="peer_select",
    )(x1, g_ffn.reshape(1, D), wpq_bf, k_sub1, k_sub2)


def _peer_dense_kernel(xn_ref, u_ref, vtp_ref, vtl_ref, cnt_ref, p1n_ref, rank_ref, p2_ref, x1_ref,
                       gfin_ref, y_ref, acc_scr, st_scr, h_scr, *, tb, rows):
    j = pl.program_id(1)
    cur = lax.rem(j, 2)

    @pl.when(j == 0)
    def _():
        acc_scr[...] = jnp.zeros_like(acc_scr)
        h_scr[1] = jnp.zeros(h_scr.shape[1:], _BF16)

    tile = (PEER_NKEYS, 128)
    words = (PEER_NKEYS // 2, 128)
    rb = 2
    for b in range(rows // rb):
        bs = slice(b * rb * PEER_NKEYS, (b + 1) * rb * PEER_NKEYS)
        st_scr[bs, :] = _dot_nt(u_ref[bs, :], xn_ref[...])
        for r in range(b * rb, (b + 1) * rb):
            es = slice(r * PEER_NKEYS, (r + 1) * PEER_NKEYS)
            for lg in range(tb // 128):
                ls = slice(lg * 128, (lg + 1) * 128)
                gate = jnp.zeros(tile, _BF16)
                for h in range(PEER_HEADS):
                    cnt = pltpu.bitcast(jnp.broadcast_to(cnt_ref[h, r:r + 1, ls], words), _BF16)
                    pn = pltpu.bitcast(jnp.broadcast_to(p1n_ref[h, r:r + 1, ls], words), _BF16)
                    sel = jnp.minimum(jnp.maximum(cnt - rank_ref[h, :, ls], 0), pn)
                    gate = gate + sel * p2_ref[h, :, ls]
                act = jax.nn.gelu(st_scr[es, ls].astype(_BF16))
                h_scr[cur, es, ls] = gate * act
    acc_scr[...] += _dot(vtp_ref[0], h_scr[1 - cur])

    @pl.when(j == pl.num_programs(1) - 1)
    def _():
        acc = acc_scr[...] + _dot(vtl_ref[0], h_scr[cur])
        x2 = x1_ref[...] + acc.T
        y_ref[...] = _rms(x2, gfin_ref[...])


def _peer_dense(xn, u_bf, vt_bf, cnt, p1n, rank2, p2, x1, g_final, *, tb, ec):
    T, D = x1.shape
    rows = ec // PEER_NKEYS
    nchunks = PEER_NEXP // ec
    assert T % tb == 0 and PEER_NEXP % ec == 0 and rows % 8 == 0
    sel_all = pl.BlockSpec((PEER_HEADS, PEER_NKEYS, tb), lambda i, j: (0, 0, i))
    sel_rows = pl.BlockSpec((PEER_HEADS, rows, tb), lambda i, j: (0, j, i))
    tok = pl.BlockSpec((tb, D), lambda i, j: (i, 0))
    return pl.pallas_call(
        functools.partial(_peer_dense_kernel, tb=tb, rows=rows),
        out_shape=jax.ShapeDtypeStruct((T, D), _F32),
        grid=(T // tb, nchunks),
        in_specs=[
            tok,
            pl.BlockSpec((ec, D), lambda i, j: (j, 0)),
            pl.BlockSpec((1, D, ec), lambda i, j: (jnp.maximum(j - 1, 0), 0, 0)),
            pl.BlockSpec((1, D, ec), lambda i, j: (nchunks - 1, 0, 0), pipeline_mode=pl.Buffered(1)),
            sel_rows, sel_rows, sel_all, sel_all,
            tok,
            pl.BlockSpec((1, D), lambda i, j: (0, 0)),
        ],
        out_specs=tok,
        scratch_shapes=[
            pltpu.VMEM((D, tb), _F32),
            pltpu.VMEM((ec, tb), _F32),
            pltpu.VMEM((2, ec, tb), _BF16),
        ],
        compiler_params=pltpu.CompilerParams(
            dimension_semantics=("arbitrary", "arbitrary"), vmem_limit_bytes=_VMEM_LIMIT),
        name="peer_dense",
    )(xn, u_bf, vt_bf, vt_bf, cnt, p1n, rank2, p2, x1, g_final.reshape(1, D))


def _mixer_weights(g_mix, w_in, b_in, g_mlh, g_sgu, w_s, b_s, w_out):
    nq = ML_HEADS * ML_DQK
    nv = ML_HEADS * ML_DV
    o_q, o_k, o_v = 0, nq, 2 * nq
    o_ig = o_v + nv
    o_fg = o_ig + ML_HEADS
    o_og = o_fg + ML_HEADS
    o_su = o_og + nv
    o_sv = o_su + D_MODEL
    o_xq = o_sv + D_MODEL
    o_gt = o_xq + D_MODEL
    gate = lambda b: slice(o_gt + b * D_MODEL, o_gt + (b + 1) * D_MODEL)
    main = lambda a: jnp.concatenate(
        [a[..., o_q:o_ig],
         a[..., o_og:o_su], a[..., gate(0)],
         a[..., o_su:o_xq], a[..., gate(1)],
         a[..., o_xq:o_gt], a[..., gate(2)]], axis=-1)
    w_if = w_in[:, o_ig:o_og]
    b_if = b_in[o_ig:o_og]
    return (
        g_mix.reshape(1, D_MODEL),
        main(w_in).astype(_BF16),
        main(b_in).reshape(1, _P_COLS),
        w_if.astype(_BF16),
        w_if.T.astype(_BF16),
        b_if.reshape(1, 8),
        b_if.reshape(8, 1),
        g_mlh.reshape(1, D_MODEL),
        g_sgu.reshape(1, D_MODEL),
        w_s,
        b_s.T,
        w_out.astype(_BF16),
    )


def kernel(x_prompt, x_sample, mem_prompt, cache_mem_k, cache_mem_v, state_mlstm_C, state_mlstm_n,
           state_mlstm_m, g_mix, w_in, b_in, g_mlh, g_sgu, w_s, b_s, g_mem, w_mk, w_mv, w_out, g_ffn,
           w_pq, k_sub1, k_sub2, u_exp, v_exp, g_final):
    depth = g_mix.shape[0]
    assert depth == 1
    l = 0
    Bp, Sp, D = x_prompt.shape
    Bs, Ss, _ = x_sample.shape

    wts = _mixer_weights(g_mix[l], w_in[l], b_in[l], g_mlh[l], g_sgu[l], w_s[l], b_s[l], w_out[l])

    mk, mv = _mem_kv(mem_prompt, g_mem[l], w_mk[l], w_mv[l])
    zc = jnp.zeros((Bp, ML_HEADS, ML_DQK, ML_DV), _F32)
    zn = jnp.zeros((Bp, ML_HEADS, ML_DQK), _F32)
    zm = jnp.zeros((Bp, ML_HEADS), _F32)
    x1p, cp, np_, mp, _ = _mixer(x_prompt, mk, mv, zc, zn, zm, wts,
                                 tc=min(_MIXER_TC, Sp), want_vn=False)
    x1s, cs, ns, ms, vn = _mixer(
        x_sample, cache_mem_k[l].reshape(Bs, N_MEM, D), cache_mem_v[l].reshape(Bs, N_MEM, D),
        state_mlstm_C[l], state_mlstm_n[l], state_mlstm_m[l], wts,
        tc=min(_MIXER_TC, Ss), want_vn=True)

    wpq_bf = w_pq[l].astype(_BF16)
    u_bf = u_exp[l].astype(_BF16)
    vt_bf = v_exp[l].astype(_BF16).reshape(PEER_NEXP // _PEER_EC, _PEER_EC, D).transpose(0, 2, 1)

    def peer(x1):
        B, S, _ = x1.shape
        x1 = x1.reshape(B * S, D)
        xn, cnt, p1n, rank2, p2 = _peer_select(x1, g_ffn[l], wpq_bf, k_sub1[l], k_sub2[l], tb=_PEER_TB)
        y = _peer_dense(xn, u_bf, vt_bf, cnt, p1n, rank2, p2, x1, g_final, tb=_PEER_TB, ec=_PEER_EC)
        return y.reshape(B, S, D)

    hs = (XA_HEADS, XA_DH)
    return (peer(x1p), peer(x1s), cp[None], np_[None], mp[None],
            mk.reshape(1, Bp, N_MEM, *hs), mv.reshape(1, Bp, N_MEM, *hs),
            cs[None], ns[None], ms[None], vn[None])
```

```python
import functools

import jax
import jax.numpy as jnp
from jax import lax
from jax.experimental import pallas as pl
from jax.experimental.pallas import tpu as pltpu

D_MODEL = 1024
EPS = 1e-6
CHUNK = 64
N_MEM = 256
ML_HEADS = 4
ML_DQK = 128
ML_DV = D_MODEL // ML_HEADS
SGU_CHUNK = 128
SGU_GROUPS = 4
SGU_GDIM = D_MODEL // SGU_GROUPS
XA_HEADS = 4
XA_DH = D_MODEL // XA_HEADS
PEER_HEADS = 8
PEER_NKEYS = 128
PEER_DQ = 256
PEER_TOPK = 16
PEER_NEXP = PEER_NKEYS * PEER_NKEYS

_SEG_A = ML_HEADS * (2 * ML_DQK + ML_DV)
_SEG_B = 2 * D_MODEL
_SEG_C = 3 * D_MODEL
_SEG_D = 2 * D_MODEL
_SEG0 = (0, _SEG_A, _SEG_A + _SEG_B, _SEG_A + _SEG_B + _SEG_C)
_P_COLS = _SEG_A + _SEG_B + _SEG_C + _SEG_D
_K0 = ML_HEADS * ML_DQK
_V0 = 2 * ML_HEADS * ML_DQK

_VMEM_LIMIT = 56 * 1024 * 1024

_MIXER_TC = 256
_PEER_TB = 512
_PEER_EC = 2048

_BF16 = jnp.bfloat16
_F32 = jnp.float32
_NEG_INF = float("-inf")


def _rms(xf, g):
    return xf * lax.rsqrt(jnp.mean(xf * xf, axis=-1, keepdims=True) + EPS) * g


def _dot(a, b):
    return jnp.dot(a, b, preferred_element_type=_F32)


def _dot_nt(a, b):
    return lax.dot_general(a, b, (((1,), (1,)), ((), ())), preferred_element_type=_F32)


def _bmm(a, b, ca, cb):
    return lax.dot_general(a, b, (((ca,), (cb,)), ((0,), (0,))), preferred_element_type=_F32)


def _mem_kv_kernel(mem_ref, g_ref, wk_ref, wv_ref, k_ref, v_ref):
    mn = _rms(mem_ref[0], g_ref[...]).astype(_BF16)
    k_ref[0] = _dot(mn, wk_ref[...])
    v_ref[0] = _dot(mn, wv_ref[...])


def _mem_kv(mem, g_mem, w_mk, w_mv):
    B = mem.shape[0]
    full = lambda b: (0, 0)
    return pl.pallas_call(
        _mem_kv_kernel,
        out_shape=(jax.ShapeDtypeStruct((B, N_MEM, D_MODEL), _F32),) * 2,
        grid=(B,),
        in_specs=[
            pl.BlockSpec((1, N_MEM, D_MODEL), lambda b: (b, 0, 0)),
            pl.BlockSpec((1, D_MODEL), full),
            pl.BlockSpec((D_MODEL, D_MODEL), full),
            pl.BlockSpec((D_MODEL, D_MODEL), full),
        ],
        out_specs=(pl.BlockSpec((1, N_MEM, D_MODEL), lambda b: (b, 0, 0)),) * 2,
        compiler_params=pltpu.CompilerParams(
            dimension_semantics=("arbitrary",), vmem_limit_bytes=_VMEM_LIMIT),
        name="mem_kv",
    )(mem, g_mem.reshape(1, D_MODEL), w_mk.astype(_BF16), w_mv.astype(_BF16))


def _mixer_kernel(x_ref, mk_ref, mv_ref, c0_ref, n0_ref, m0_ref, gmix_ref, win_ref, bin_ref,
                  wif_ref, wift_ref, bif_ref, bift_ref, gmlh_ref, gsgu_ref, ws_ref, bst_ref,
                  wout_ref,
                  x1_ref, c_out_ref, n_out_ref, m_out_ref, vn_ref,
                  pa_scr, pb_scr, pc_scr, pd_scr, mrg_scr, c_scr, n_scr, m_scr, *, tc, sgu_len):
    ci = pl.program_id(1)
    nsub = tc // CHUNK

    @pl.when(ci == 0)
    def _():
        c_scr[...] = c0_ref[0]
        n_scr[...] = n0_ref[0]
        m_scr[...] = m0_ref[0]

    x = x_ref[0]
    xn = _rms(x, gmix_ref[...]).astype(_BF16)

    def project(seg, out_scr):
        cols = slice(_SEG0[seg], _SEG0[seg] + out_scr.shape[1])
        out_scr[...] = _dot(xn, win_ref[:, cols]) + bin_ref[:, cols]

    project(0, pa_scr)
    gif = _dot(xn, wif_ref[...]) + bif_ref[...]
    gift = _dot_nt(wift_ref[...], xn) + bift_ref[...]

    row = lax.broadcasted_iota(jnp.int32, (CHUNK, CHUNK), 0)
    col = lax.broadcasted_iota(jnp.int32, (CHUNK, CHUNK), 1)
    tril = row >= col
    tril_f = tril.astype(_F32)
    triu_f = (row <= col).astype(_F32)
    stack = lambda f: jnp.stack([f(h) for h in range(ML_HEADS)])
    for j in range(nsub):
        r0 = j * CHUNK
        rows = slice(r0, r0 + CHUNK)
        ig_c = gif[rows, 0:ML_HEADS]
        lf_c = jax.nn.log_sigmoid(gif[rows, ML_HEADS:2 * ML_HEADS])
        ig_r = gift[0:ML_HEADS, rows]
        lf_r = jax.nn.log_sigmoid(gift[ML_HEADS:2 * ML_HEADS, rows])
        b_c = jnp.dot(tril_f, lf_c, precision=lax.Precision.HIGHEST,
                      preferred_element_type=_F32)
        b_r = jnp.dot(lf_r, triu_f, precision=lax.Precision.HIGHEST,
                      preferred_element_type=_F32)
        q = stack(lambda h: pa_scr[rows, h * ML_DQK:(h + 1) * ML_DQK]).astype(_BF16)
        k = stack(lambda h: pa_scr[rows, _K0 + h * ML_DQK:_K0 + (h + 1) * ML_DQK]) * (ML_DQK ** -0.5)
        v = stack(lambda h: pa_scr[rows, _V0 + h * ML_DV:_V0 + (h + 1) * ML_DV]).astype(_BF16)
        c_st = c_scr[...]
        n_st = stack(lambda h: n_scr[h:h + 1, :])
        m_st = stack(lambda h: m_scr[h:h + 1, 0:1])
        bc = stack(lambda h: b_c[:, h:h + 1])
        br = stack(lambda h: b_r[h:h + 1, :])
        ig_row = stack(lambda h: ig_r[h:h + 1, :])
        ig_col = stack(lambda h: ig_c[:, h:h + 1])
        logd = jnp.where(tril[None], bc - br + ig_row, _NEG_INF)
        log_prev = bc + m_st
        m_t = jnp.maximum(log_prev, jnp.max(logd, axis=2, keepdims=True))
        a = _bmm(q, k.astype(_BF16), 2, 2) * jnp.exp(logd - m_t)
        wp = jnp.exp(log_prev - m_t)
        num = _bmm(a.astype(_BF16), v, 2, 1) + wp * _bmm(q, c_st.astype(_BF16), 2, 1)
        qn = jnp.sum(q.astype(_F32) * n_st, axis=2, keepdims=True)
        den = jnp.sum(a, axis=2, keepdims=True) + wp * qn
        hh = num / jnp.maximum(jnp.abs(den), jnp.exp(-m_t))
        hn = hh * lax.rsqrt(jnp.mean(hh * hh, axis=2, keepdims=True) + EPS)
        for h in range(ML_HEADS):
            mrg_scr[rows, h * ML_DV:(h + 1) * ML_DV] = hn[h]
        b_last = bc[:, CHUNK - 1:CHUNK, :]
        log_in = b_last - bc + ig_col
        m_new = jnp.maximum(b_last + m_st, jnp.max(log_in, axis=1, keepdims=True))
        wi = jnp.exp(log_in - m_new)
        wc = jnp.exp(b_last + m_st - m_new)
        kw = wi * k
        c_scr[...] = wc * c_st + _bmm(kw.astype(_BF16), v, 1, 1)
        n_new = wc * n_st + jnp.sum(kw, axis=1, keepdims=True)
        for h in range(ML_HEADS):
            n_scr[h:h + 1, :] = n_new[h]
            m_scr[h:h + 1, :] = jnp.broadcast_to(m_new[h], (1, 128))

    project(1, pb_scr)
    h_a = jax.nn.sigmoid(pb_scr[:, 0:D_MODEL]) * mrg_scr[...] * gmlh_ref[...]
    mrg_scr[...] = jax.nn.sigmoid(pb_scr[:, D_MODEL:2 * D_MODEL]) * h_a

    project(2, pc_scr)
    sv = jax.nn.gelu(pc_scr[:, D_MODEL:2 * D_MODEL])
    vn = _rms(sv, gsgu_ref[...])
    if vn_ref is not None:
        vn_ref[0] = vn
    vnb = vn.astype(_BF16)
    rs = lax.broadcasted_iota(jnp.int32, (sgu_len, sgu_len), 0)
    cs = lax.broadcasted_iota(jnp.int32, (sgu_len, sgu_len), 1)
    for g in range(SGU_GROUPS):
        wsg = jnp.where(rs >= cs, ws_ref[g, 0:sgu_len, 0:sgu_len], 0.0).astype(_BF16)
        bsg = bst_ref[0:sgu_len, g:g + 1]
        gl = slice(g * SGU_GDIM, (g + 1) * SGU_GDIM)
        for c in range(tc // sgu_len):
            rows = slice(c * sgu_len, (c + 1) * sgu_len)
            mix = _dot(wsg, vnb[rows, gl]) + bsg
            u = jax.nn.gelu(pc_scr[rows, g * SGU_GDIM:(g + 1) * SGU_GDIM])
            g1 = jax.nn.sigmoid(pc_scr[rows, 2 * D_MODEL + g * SGU_GDIM:2 * D_MODEL + (g + 1) * SGU_GDIM])
            mrg_scr[rows, gl] += g1 * (u * mix)

    project(3, pd_scr)
    for h in range(XA_HEADS):
        hl = slice(h * XA_DH, (h + 1) * XA_DH)
        xq = pd_scr[:, h * XA_DH:(h + 1) * XA_DH].astype(_BF16)
        sc = _dot_nt(xq, mk_ref[0, :, hl].astype(_BF16)) * (XA_DH ** -0.5)
        sc = sc - jnp.max(sc, axis=1, keepdims=True)
        e = jnp.exp(sc)
        att = e / jnp.sum(e, axis=1, keepdims=True)
        h_c = _dot(att.astype(_BF16), mv_ref[0, :, hl].astype(_BF16))
        g2 = jax.nn.sigmoid(pd_scr[:, D_MODEL + h * XA_DH:D_MODEL + (h + 1) * XA_DH])
        mrg_scr[:, hl] += g2 * h_c

    x1_ref[0] = x + _dot(mrg_scr[...].astype(_BF16), wout_ref[...])

    @pl.when(ci == pl.num_programs(1) - 1)
    def _():
        c_out_ref[0] = c_scr[...]
        n_out_ref[0] = n_scr[...]
        m_out_ref[0] = m_scr[...]


def _mixer(x, mem_k, mem_v, c0, n0, m0, wts, *, tc, want_vn):
    B, S, D = x.shape
    sgu_len = min(S, SGU_CHUNK)
    assert S % tc == 0 and tc % CHUNK == 0 and tc % sgu_len == 0
    nchunks = S // tc
    m0p = jnp.broadcast_to(m0[:, :, None], (B, ML_HEADS, 128))
    m0p = jnp.concatenate([m0p, jnp.zeros((B, 8 - ML_HEADS, 128), _F32)], axis=1)

    def body(*refs):
        ins, rest = refs[:18], refs[18:]
        if want_vn:
            outs, scr = rest[:5], rest[5:]
        else:
            outs, scr = rest[:4] + (None,), rest[4:]
        _mixer_kernel(*ins, *outs, *scr, tc=tc, sgu_len=sgu_len)

    const2 = lambda b, c: (0, 0)
    const3 = lambda b, c: (0, 0, 0)
    per_b3 = lambda b, c: (b, 0, 0)
    per_b4 = lambda b, c: (b, 0, 0, 0)
    once = dict(pipeline_mode=pl.Buffered(1))
    in_specs = [
        pl.BlockSpec((1, tc, D), lambda b, c: (b, c, 0)),
        pl.BlockSpec((1, N_MEM, D), per_b3),
        pl.BlockSpec((1, N_MEM, D), per_b3),
        pl.BlockSpec((1, ML_HEADS, ML_DQK, ML_DV), per_b4),
        pl.BlockSpec((1, ML_HEADS, ML_DQK), per_b3),
        pl.BlockSpec((1, 8, 128), per_b3),
        pl.BlockSpec((1, D), const2, **once),
        pl.BlockSpec((D, _P_COLS), const2, **once),
        pl.BlockSpec((1, _P_COLS), const2, **once),
        pl.BlockSpec((D, 8), const2, **once),
        pl.BlockSpec((8, D), const2, **once),
        pl.BlockSpec((1, 8), const2, **once),
        pl.BlockSpec((8, 1), const2, **once),
        pl.BlockSpec((1, D), const2, **once),
        pl.BlockSpec((1, D), const2, **once),
        pl.BlockSpec((SGU_GROUPS, SGU_CHUNK, SGU_CHUNK), const3, **once),
        pl.BlockSpec((SGU_CHUNK, SGU_GROUPS), const2, **once),
        pl.BlockSpec((D, D), const2, **once),
    ]
    out_shape = [
        jax.ShapeDtypeStruct((B, S, D), _F32),
        jax.ShapeDtypeStruct((B, ML_HEADS, ML_DQK, ML_DV), _F32),
        jax.ShapeDtypeStruct((B, ML_HEADS, ML_DQK), _F32),
        jax.ShapeDtypeStruct((B, 8, 128), _F32),
    ]
    out_specs = [
        pl.BlockSpec((1, tc, D), lambda b, c: (b, c, 0)),
        pl.BlockSpec((1, ML_HEADS, ML_DQK, ML_DV), per_b4),
        pl.BlockSpec((1, ML_HEADS, ML_DQK), per_b3),
        pl.BlockSpec((1, 8, 128), per_b3),
    ]
    if want_vn:
        out_shape.append(jax.ShapeDtypeStruct((B, S, D), _F32))
        out_specs.append(pl.BlockSpec((1, tc, D), lambda b, c: (b, c, 0)))
    outs = pl.pallas_call(
        body,
        out_shape=tuple(out_shape),
        grid=(B, nchunks),
        in_specs=in_specs,
        out_specs=tuple(out_specs),
        scratch_shapes=[
            pltpu.VMEM((tc, _SEG_A), _F32),
            pltpu.VMEM((tc, _SEG_B), _F32),
            pltpu.VMEM((tc, _SEG_C), _F32),
            pltpu.VMEM((tc, _SEG_D), _F32),
            pltpu.VMEM((tc, D), _F32),
            pltpu.VMEM((ML_HEADS, ML_DQK, ML_DV), _F32),
            pltpu.VMEM((ML_HEADS, ML_DQK), _F32),
            pltpu.VMEM((8, 128), _F32),
        ],
        compiler_params=pltpu.CompilerParams(
            dimension_semantics=("arbitrary", "arbitrary"), vmem_limit_bytes=_VMEM_LIMIT),
        name="mixer",
    )(x, mem_k, mem_v, c0, n0, m0p, *wts)
    x1, c1, n1, m1p = outs[:4]
    vn = outs[4] if want_vn else None
    return x1, c1, n1, m1p[:, :ML_HEADS, 0], vn


def _batcher_pairs(n):
    pairs = []
    p = 1
    while p < n:
        k = p
        while k >= 1:
            for j in range(k % p, n - k, 2 * k):
                for i in range(min(k, n - j - k)):
                    if (i + j) // (2 * p) == (i + j + k) // (2 * p):
                        pairs.append((i + j, i + j + k))
            k //= 2
        p *= 2
    return pairs


_SORT16 = _batcher_pairs(PEER_NKEYS // 8)
_SEL_UNROLL = 2


def _sorted_columns(c):
    cols = [c[8 * k:8 * k + 8, :] for k in range(PEER_NKEYS // 8)]
    for i, j in _SORT16:
        cols[i], cols[j] = jnp.maximum(cols[i], cols[j]), jnp.minimum(cols[i], cols[j])
    return cols


def _pop_top(cols, out_scr):
    for r in range(PEER_TOPK):
        mx = jnp.max(cols[0], axis=0, keepdims=True)
        out_scr[r:r + 1, :] = mx
        left = PEER_TOPK - 1 - r
        if left:
            eq = cols[0] == mx
            for k in range(left):
                cols[k] = jnp.where(eq, cols[k + 1], cols[k])


def _bf16_pair(x, exact=False):
    u = pltpu.bitcast(x, jnp.uint32)
    if not exact:
        u = u + jnp.uint32(0x7FFF) + ((u >> 16) & jnp.uint32(1))
    hi = u & jnp.uint32(0xFFFF0000)
    return hi | (hi >> 16)


def _peer_select_kernel(x1_ref, gffn_ref, wpq_ref, k1_ref, k2_ref,
                        xn_ref, cnt_ref, p1n_ref, rank_ref, p2_ref,
                        q_scr, c1_scr, c2_scr, v1_scr, v2_scr, n_scr, *, tb):
    xn = _rms(x1_ref[...], gffn_ref[...]).astype(_BF16)
    xn_ref[...] = xn
    q_scr[...] = _dot(xn, wpq_ref[...])
    k1 = k1_ref[...].astype(_BF16)
    k2 = k2_ref[...].astype(_BF16)
    half = PEER_DQ // 2
    K = PEER_TOPK

    def lane_group(h, ls):
        c1 = c1_scr[:, ls]
        c2 = c2_scr[:, ls]
        _pop_top(_sorted_columns(c1), v1_scr)
        _pop_top(_sorted_columns(c2), v2_scr)
        v1 = v1_scr[...]
        lists = [v1_scr[0:8, :] + v2_scr[b:b + 1, :] for b in range(8)]
        ea = v1_scr[8:16, :] + v2_scr[0:1, :]
        eb = v1_scr[0:1, :] + v2_scr[8:16, :]
        cands = lists + [ea, eb]
        tau = None
        for r in range(K):
            tau = jnp.max(jnp.maximum(jnp.maximum(lists[0], ea), eb), axis=0, keepdims=True)
            left = K - 1 - r
            if left:
                eq = lists[0] == tau
                for k in range(min(left, 7)):
                    lists[k] = jnp.where(eq, lists[k + 1], lists[k])
                if left >= 8:
                    lists[7] = jnp.where(eq, _NEG_INF, lists[7])
                ea = jnp.where(ea == tau, _NEG_INF, ea)
                eb = jnp.where(eb == tau, _NEG_INF, eb)
        z = sum(jnp.sum(jnp.where(c >= tau, jnp.exp(c), 0.0), axis=0, keepdims=True) for c in cands)
        n = jnp.zeros(v1.shape, _F32)
        for b in range(K):
            n = jnp.where(v1 + v2_scr[b:b + 1, :] >= tau, float(b + 1), n)
        n_scr[...] = n
        cnt = jnp.zeros(c1.shape, _F32)
        rank2 = jnp.full(c2.shape, float(K), _F32)
        for a in range(K - 1, -1, -1):
            cnt = jnp.where(c1 >= v1_scr[a:a + 1, :], n_scr[a:a + 1, :], cnt)
            rank2 = jnp.where(c2 >= v2_scr[a:a + 1, :], float(a), rank2)
        cnt_ref[h, :, ls] = _bf16_pair(cnt, exact=True)
        p1n_ref[h, :, ls] = _bf16_pair(jnp.exp(c1) / z)
        rank_ref[h, :, ls] = rank2.astype(_BF16)
        p2_ref[h, :, ls] = jnp.exp(c2).astype(_BF16)

    def head(h, carry):
        q0 = pl.multiple_of(h * PEER_DQ, PEER_DQ)
        s1 = _dot_nt(k1, q_scr[:, pl.ds(q0, half)].astype(_BF16))
        s2 = _dot_nt(k2, q_scr[:, pl.ds(q0 + half, half)].astype(_BF16))
        c1_scr[...] = s1 - jnp.max(s1, axis=0, keepdims=True)
        c2_scr[...] = s2 - jnp.max(s2, axis=0, keepdims=True)

        def lane_groups(i, carry):
            for u in range(_SEL_UNROLL):
                lane_group(h, pl.ds(pl.multiple_of((i * _SEL_UNROLL + u) * 128, 128), 128))
            return carry

        lax.fori_loop(0, tb // (128 * _SEL_UNROLL), lane_groups, 0)
        return carry

    lax.fori_loop(0, PEER_HEADS, head, 0)


def _peer_select(x1, g_ffn, wpq_bf, k_sub1, k_sub2, *, tb):
    T, D = x1.shape
    assert T % tb == 0 and tb % (128 * _SEL_UNROLL) == 0
    sel = lambda dt: jax.ShapeDtypeStruct((PEER_HEADS, PEER_NKEYS, T), dt)
    sel_spec = pl.BlockSpec((PEER_HEADS, PEER_NKEYS, tb), lambda i: (0, 0, i))
    const2 = lambda i: (0, 0)
    return pl.pallas_call(
        functools.partial(_peer_select_kernel, tb=tb),
        out_shape=(jax.ShapeDtypeStruct((T, D), _BF16), sel(jnp.uint32), sel(jnp.uint32), sel(_BF16), sel(_BF16)),
        grid=(T // tb,),
        in_specs=[
            pl.BlockSpec((tb, D), lambda i: (i, 0)),
            pl.BlockSpec((1, D), const2),
            pl.BlockSpec((D, PEER_HEADS * PEER_DQ), const2),
            pl.BlockSpec((PEER_NKEYS, PEER_DQ // 2), const2),
            pl.BlockSpec((PEER_NKEYS, PEER_DQ // 2), const2),
        ],
        out_specs=(pl.BlockSpec((tb, D), lambda i: (i, 0)),) + (sel_spec,) * 4,
        scratch_shapes=[
            pltpu.VMEM((tb, PEER_HEADS * PEER_DQ), _F32),
            pltpu.VMEM((PEER_NKEYS, tb), _F32),
            pltpu.VMEM((PEER_NKEYS, tb), _F32),
            pltpu.VMEM((PEER_TOPK, 128), _F32),
            pltpu.VMEM((PEER_TOPK, 128), _F32),
            pltpu.VMEM((PEER_TOPK, 128), _F32),
        ],
        compiler_params=pltpu.CompilerParams(
            dimension_semantics=("arbitrary",), vmem_limit_bytes=_VMEM_LIMIT),
        name="peer_select",
    )(x1, g_ffn.reshape(1, D), wpq_bf, k_sub1, k_sub2)


def _peer_dense_kernel(xn_ref, u_ref, vtp_ref, vtl_ref, cnt_ref, p1n_ref, rank_ref, p2_ref, x1_ref,
                       gfin_ref, y_ref, acc_scr, st_scr, h_scr, *, tb, rows):
    j = pl.program_id(1)
    cur = lax.rem(j, 2)

    @pl.when(j == 0)
    def _():
        acc_scr[...] = jnp.zeros_like(acc_scr)
        h_scr[1] = jnp.zeros(h_scr.shape[1:], _BF16)

    tile = (PEER_NKEYS, 128)
    words = (PEER_NKEYS // 2, 128)
    rb = 2
    for b in range(rows // rb):
        bs = slice(b * rb * PEER_NKEYS, (b + 1) * rb * PEER_NKEYS)
        st_scr[bs, :] = jax.nn.gelu(_dot_nt(u_ref[bs, :], xn_ref[...]))
        for r in range(b * rb, (b + 1) * rb):
            es = slice(r * PEER_NKEYS, (r + 1) * PEER_NKEYS)
            for lg in range(tb // 128):
                ls = slice(lg * 128, (lg + 1) * 128)
                gate = jnp.zeros(tile, _BF16)
                for h in range(PEER_HEADS):
                    cnt = pltpu.bitcast(jnp.broadcast_to(cnt_ref[h, r:r + 1, ls], words), _BF16)
                    pn = pltpu.bitcast(jnp.broadcast_to(p1n_ref[h, r:r + 1, ls], words), _BF16)
                    sel = jnp.minimum(jnp.maximum(cnt - rank_ref[h, :, ls], 0), pn)
                    gate = gate + sel * p2_ref[h, :, ls]
                h_scr[cur, es, ls] = gate * st_scr[es, ls].astype(_BF16)
    acc_scr[...] += _dot(vtp_ref[0], h_scr[1 - cur])

    @pl.when(j == pl.num_programs(1) - 1)
    def _():
        acc = acc_scr[...] + _dot(vtl_ref[0], h_scr[cur])
        x2 = x1_ref[...] + acc.T
        y_ref[...] = _rms(x2, gfin_ref[...])


def _peer_dense(xn, u_bf, vt_bf, cnt, p1n, rank2, p2, x1, g_final, *, tb, ec):
    T, D = x1.shape
    rows = ec // PEER_NKEYS
    nchunks = PEER_NEXP // ec
    assert T % tb == 0 and PEER_NEXP % ec == 0 and rows % 8 == 0
    sel_all = pl.BlockSpec((PEER_HEADS, PEER_NKEYS, tb), lambda i, j: (0, 0, i))
    sel_rows = pl.BlockSpec((PEER_HEADS, rows, tb), lambda i, j: (0, j, i))
    tok = pl.BlockSpec((tb, D), lambda i, j: (i, 0))
    return pl.pallas_call(
        functools.partial(_peer_dense_kernel, tb=tb, rows=rows),
        out_shape=jax.ShapeDtypeStruct((T, D), _F32),
        grid=(T // tb, nchunks),
        in_specs=[
            tok,
            pl.BlockSpec((ec, D), lambda i, j: (j, 0)),
            pl.BlockSpec((1, D, ec), lambda i, j: (jnp.maximum(j - 1, 0), 0, 0)),
            pl.BlockSpec((1, D, ec), lambda i, j: (nchunks - 1, 0, 0), pipeline_mode=pl.Buffered(1)),
            sel_rows, sel_rows, sel_all, sel_all,
            tok,
            pl.BlockSpec((1, D), lambda i, j: (0, 0)),
        ],
        out_specs=tok,
        scratch_shapes=[
            pltpu.VMEM((D, tb), _F32),
            pltpu.VMEM((ec, tb), _F32),
            pltpu.VMEM((2, ec, tb), _BF16),
        ],
        compiler_params=pltpu.CompilerParams(
            dimension_semantics=("arbitrary", "arbitrary"), vmem_limit_bytes=_VMEM_LIMIT),
        name="peer_dense",
    )(xn, u_bf, vt_bf, vt_bf, cnt, p1n, rank2, p2, x1, g_final.reshape(1, D))


def _mixer_weights(g_mix, w_in, b_in, g_mlh, g_sgu, w_s, b_s, w_out):
    nq = ML_HEADS * ML_DQK
    nv = ML_HEADS * ML_DV
    o_q, o_k, o_v = 0, nq, 2 * nq
    o_ig = o_v + nv
    o_fg = o_ig + ML_HEADS
    o_og = o_fg + ML_HEADS
    o_su = o_og + nv
    o_sv = o_su + D_MODEL
    o_xq = o_sv + D_MODEL
    o_gt = o_xq + D_MODEL
    gate = lambda b: slice(o_gt + b * D_MODEL, o_gt + (b + 1) * D_MODEL)
    main = lambda a: jnp.concatenate(
        [a[..., o_q:o_ig],
         a[..., o_og:o_su], a[..., gate(0)],
         a[..., o_su:o_xq], a[..., gate(1)],
         a[..., o_xq:o_gt], a[..., gate(2)]], axis=-1)
    w_if = w_in[:, o_ig:o_og]
    b_if = b_in[o_ig:o_og]
    return (
        g_mix.reshape(1, D_MODEL),
        main(w_in).astype(_BF16),
        main(b_in).reshape(1, _P_COLS),
        w_if.astype(_BF16),
        w_if.T.astype(_BF16),
        b_if.reshape(1, 8),
        b_if.reshape(8, 1),
        g_mlh.reshape(1, D_MODEL),
        g_sgu.reshape(1, D_MODEL),
        w_s,
        b_s.T,
        w_out.astype(_BF16),
    )


def kernel(x_prompt, x_sample, mem_prompt, cache_mem_k, cache_mem_v, state_mlstm_C, state_mlstm_n,
           state_mlstm_m, g_mix, w_in, b_in, g_mlh, g_sgu, w_s, b_s, g_mem, w_mk, w_mv, w_out, g_ffn,
           w_pq, k_sub1, k_sub2, u_exp, v_exp, g_final):
    depth = g_mix.shape[0]
    assert depth == 1
    l = 0
    Bp, Sp, D = x_prompt.shape
    Bs, Ss, _ = x_sample.shape

    wts = _mixer_weights(g_mix[l], w_in[l], b_in[l], g_mlh[l], g_sgu[l], w_s[l], b_s[l], w_out[l])

    mk, mv = _mem_kv(mem_prompt, g_mem[l], w_mk[l], w_mv[l])
    zc = jnp.zeros((Bp, ML_HEADS, ML_DQK, ML_DV), _F32)
    zn = jnp.zeros((Bp, ML_HEADS, ML_DQK), _F32)
    zm = jnp.zeros((Bp, ML_HEADS), _F32)
    x1p, cp, np_, mp, _ = _mixer(x_prompt, mk, mv, zc, zn, zm, wts,
                                 tc=min(_MIXER_TC, Sp), want_vn=False)
    x1s, cs, ns, ms, vn = _mixer(
        x_sample, cache_mem_k[l].reshape(Bs, N_MEM, D), cache_mem_v[l].reshape(Bs, N_MEM, D),
        state_mlstm_C[l], state_mlstm_n[l], state_mlstm_m[l], wts,
        tc=min(_MIXER_TC, Ss), want_vn=True)

    wpq_bf = w_pq[l].astype(_BF16)
    u_bf = u_exp[l].astype(_BF16)
    vt_bf = v_exp[l].astype(_BF16).reshape(PEER_NEXP // _PEER_EC, _PEER_EC, D).transpose(0, 2, 1)

    def peer(x1):
        B, S, _ = x1.shape
        x1 = x1.reshape(B * S, D)
        xn, cnt, p1n, rank2, p2 = _peer_select(x1, g_ffn[l], wpq_bf, k_sub1[l], k_sub2[l], tb=_PEER_TB)
        y = _peer_dense(xn, u_bf, vt_bf, cnt, p1n, rank2, p2, x1, g_final, tb=_PEER_TB, ec=_PEER_EC)
        return y.reshape(B, S, D)

    hs = (XA_HEADS, XA_DH)
    return (peer(x1p), peer(x1s), cp[None], np_[None], mp[None],
            mk.reshape(1, Bp, N_MEM, *hs), mv.reshape(1, Bp, N_MEM, *hs),
            cs[None], ns[None], ms[None], vn[None])
```

```python
import functools

import jax
import jax.numpy as jnp
from jax import lax
from jax.experimental import pallas as pl
from jax.experimental.pallas import tpu as pltpu

D_MODEL = 1024
EPS = 1e-6
CHUNK = 64
N_MEM = 256
ML_HEADS = 4
ML_DQK = 128
ML_DV = D_MODEL // ML_HEADS
SGU_CHUNK = 128
SGU_GROUPS = 4
SGU_GDIM = D_MODEL // SGU_GROUPS
XA_HEADS = 4
XA_DH = D_MODEL // XA_HEADS
PEER_HEADS = 8
PEER_NKEYS = 128
PEER_DQ = 256
PEER_TOPK = 16
PEER_NEXP = PEER_NKEYS * PEER_NKEYS

_SEG_A = ML_HEADS * (2 * ML_DQK + ML_DV)
_SEG_B = 2 * D_MODEL
_SEG_C = 3 * D_MODEL
_SEG_D = 2 * D_MODEL
_SEG0 = (0, _SEG_A, _SEG_A + _SEG_B, _SEG_A + _SEG_B + _SEG_C)
_P_COLS = _SEG_A + _SEG_B + _SEG_C + _SEG_D
_K0 = ML_HEADS * ML_DQK
_V0 = 2 * ML_HEADS * ML_DQK

_VMEM_LIMIT = 56 * 1024 * 1024

_MIXER_TC = 256
_PEER_TB = 512
_PEER_EC = 2048

_BF16 = jnp.bfloat16
_F32 = jnp.float32
_NEG_INF = float("-inf")


def _rms(xf, g):
    return xf * lax.rsqrt(jnp.mean(xf * xf, axis=-1, keepdims=True) + EPS) * g


def _dot(a, b):
    return jnp.dot(a, b, preferred_element_type=_F32)


def _dot_nt(a, b):
    return lax.dot_general(a, b, (((1,), (1,)), ((), ())), preferred_element_type=_F32)


def _bmm(a, b, ca, cb):
    return lax.dot_general(a, b, (((ca,), (cb,)), ((0,), (0,))), preferred_element_type=_F32)


def _mem_kv_kernel(mem_ref, g_ref, wk_ref, wv_ref, k_ref, v_ref):
    mn = _rms(mem_ref[0], g_ref[...]).astype(_BF16)
    k_ref[0] = _dot(mn, wk_ref[...])
    v_ref[0] = _dot(mn, wv_ref[...])


def _mem_kv(mem, g_mem, w_mk, w_mv):
    B = mem.shape[0]
    full = lambda b: (0, 0)
    return pl.pallas_call(
        _mem_kv_kernel,
        out_shape=(jax.ShapeDtypeStruct((B, N_MEM, D_MODEL), _F32),) * 2,
        grid=(B,),
        in_specs=[
            pl.BlockSpec((1, N_MEM, D_MODEL), lambda b: (b, 0, 0)),
            pl.BlockSpec((1, D_MODEL), full),
            pl.BlockSpec((D_MODEL, D_MODEL), full),
            pl.BlockSpec((D_MODEL, D_MODEL), full),
        ],
        out_specs=(pl.BlockSpec((1, N_MEM, D_MODEL), lambda b: (b, 0, 0)),) * 2,
        compiler_params=pltpu.CompilerParams(
            dimension_semantics=("arbitrary",), vmem_limit_bytes=_VMEM_LIMIT),
        name="mem_kv",
    )(mem, g_mem.reshape(1, D_MODEL), w_mk.astype(_BF16), w_mv.astype(_BF16))


def _mixer_kernel(x_ref, mk_ref, mv_ref, c0_ref, n0_ref, m0_ref, gmix_ref, win_ref, bin_ref,
                  wif_ref, wift_ref, bif_ref, bift_ref, gmlh_ref, gsgu_ref, ws_ref, bst_ref,
                  wout_ref,
                  x1_ref, c_out_ref, n_out_ref, m_out_ref, vn_ref,
                  pa_scr, pb_scr, pc_scr, pd_scr, mrg_scr, c_scr, n_scr, m_scr, *, tc, sgu_len):
    ci = pl.program_id(1)
    nsub = tc // CHUNK

    @pl.when(ci == 0)
    def _():
        c_scr[...] = c0_ref[0]
        n_scr[...] = n0_ref[0]
        m_scr[...] = m0_ref[0]

    x = x_ref[0]
    xn = _rms(x, gmix_ref[...]).astype(_BF16)

    def project(seg, out_scr):
        cols = slice(_SEG0[seg], _SEG0[seg] + out_scr.shape[1])
        out_scr[...] = _dot(xn, win_ref[:, cols]) + bin_ref[:, cols]

    project(0, pa_scr)
    gif = _dot(xn, wif_ref[...]) + bif_ref[...]
    gift = _dot_nt(wift_ref[...], xn) + bift_ref[...]

    row = lax.broadcasted_iota(jnp.int32, (CHUNK, CHUNK), 0)
    col = lax.broadcasted_iota(jnp.int32, (CHUNK, CHUNK), 1)
    tril = row >= col
    tril_f = tril.astype(_F32)
    triu_f = (row <= col).astype(_F32)
    stack = lambda f: jnp.stack([f(h) for h in range(ML_HEADS)])
    for j in range(nsub):
        r0 = j * CHUNK
        rows = slice(r0, r0 + CHUNK)
        ig_c = gif[rows, 0:ML_HEADS]
        lf_c = jax.nn.log_sigmoid(gif[rows, ML_HEADS:2 * ML_HEADS])
        ig_r = gift[0:ML_HEADS, rows]
        lf_r = jax.nn.log_sigmoid(gift[ML_HEADS:2 * ML_HEADS, rows])
        b_c = jnp.dot(tril_f, lf_c, precision=lax.Precision.HIGHEST,
                      preferred_element_type=_F32)
        b_r = jnp.dot(lf_r, triu_f, precision=lax.Precision.HIGHEST,
                      preferred_element_type=_F32)
        q = stack(lambda h: pa_scr[rows, h * ML_DQK:(h + 1) * ML_DQK]).astype(_BF16)
        k = stack(lambda h: pa_scr[rows, _K0 + h * ML_DQK:_K0 + (h + 1) * ML_DQK]) * (ML_DQK ** -0.5)
        v = stack(lambda h: pa_scr[rows, _V0 + h * ML_DV:_V0 + (h + 1) * ML_DV]).astype(_BF16)
        c_st = c_scr[...]
        n_st = stack(lambda h: n_scr[h:h + 1, :])
        m_st = stack(lambda h: m_scr[h:h + 1, 0:1])
        bc = stack(lambda h: b_c[:, h:h + 1])
        br = stack(lambda h: b_r[h:h + 1, :])
        ig_row = stack(lambda h: ig_r[h:h + 1, :])
        ig_col = stack(lambda h: ig_c[:, h:h + 1])
        logd = jnp.where(tril[None], bc - br + ig_row, _NEG_INF)
        log_prev = bc + m_st
        m_t = jnp.maximum(log_prev, jnp.max(logd, axis=2, keepdims=True))
        a = _bmm(q, k.astype(_BF16), 2, 2) * jnp.exp(logd - m_t)
        wp = jnp.exp(log_prev - m_t)
        num = _bmm(a.astype(_BF16), v, 2, 1) + wp * _bmm(q, c_st.astype(_BF16), 2, 1)
        qn = jnp.sum(q.astype(_F32) * n_st, axis=2, keepdims=True)
        den = jnp.sum(a, axis=2, keepdims=True) + wp * qn
        hh = num / jnp.maximum(jnp.abs(den), jnp.exp(-m_t))
        hn = hh * lax.rsqrt(jnp.mean(hh * hh, axis=2, keepdims=True) + EPS)
        for h in range(ML_HEADS):
            mrg_scr[rows, h * ML_DV:(h + 1) * ML_DV] = hn[h]
        b_last = bc[:, CHUNK - 1:CHUNK, :]
        log_in = b_last - bc + ig_col
        m_new = jnp.maximum(b_last + m_st, jnp.max(log_in, axis=1, keepdims=True))
        wi = jnp.exp(log_in - m_new)
        wc = jnp.exp(b_last + m_st - m_new)
        kw = wi * k
        c_scr[...] = wc * c_st + _bmm(kw.astype(_BF16), v, 1, 1)
        n_new = wc * n_st + jnp.sum(kw, axis=1, keepdims=True)
        for h in range(ML_HEADS):
            n_scr[h:h + 1, :] = n_new[h]
            m_scr[h:h + 1, :] = jnp.broadcast_to(m_new[h], (1, 128))

    project(1, pb_scr)
    h_a = jax.nn.sigmoid(pb_scr[:, 0:D_MODEL]) * mrg_scr[...] * gmlh_ref[...]
    mrg_scr[...] = jax.nn.sigmoid(pb_scr[:, D_MODEL:2 * D_MODEL]) * h_a

    project(2, pc_scr)
    sv = jax.nn.gelu(pc_scr[:, D_MODEL:2 * D_MODEL])
    vn = _rms(sv, gsgu_ref[...])
    if vn_ref is not None:
        vn_ref[0] = vn
    vnb = vn.astype(_BF16)
    rs = lax.broadcasted_iota(jnp.int32, (sgu_len, sgu_len), 0)
    cs = lax.broadcasted_iota(jnp.int32, (sgu_len, sgu_len), 1)
    for g in range(SGU_GROUPS):
        wsg = jnp.where(rs >= cs, ws_ref[g, 0:sgu_len, 0:sgu_len], 0.0).astype(_BF16)
        bsg = bst_ref[0:sgu_len, g:g + 1]
        gl = slice(g * SGU_GDIM, (g + 1) * SGU_GDIM)
        for c in range(tc // sgu_len):
            rows = slice(c * sgu_len, (c + 1) * sgu_len)
            mix = _dot(wsg, vnb[rows, gl]) + bsg
            u = jax.nn.gelu(pc_scr[rows, g * SGU_GDIM:(g + 1) * SGU_GDIM])
            g1 = jax.nn.sigmoid(pc_scr[rows, 2 * D_MODEL + g * SGU_GDIM:2 * D_MODEL + (g + 1) * SGU_GDIM])
            mrg_scr[rows, gl] += g1 * (u * mix)

    project(3, pd_scr)
    for h in range(XA_HEADS):
        hl = slice(h * XA_DH, (h + 1) * XA_DH)
        xq = pd_scr[:, h * XA_DH:(h + 1) * XA_DH].astype(_BF16)
        sc = _dot_nt(xq, mk_ref[0, :, hl].astype(_BF16)) * (XA_DH ** -0.5)
        sc = sc - jnp.max(sc, axis=1, keepdims=True)
        e = jnp.exp(sc)
        att = e / jnp.sum(e, axis=1, keepdims=True)
        h_c = _dot(att.astype(_BF16), mv_ref[0, :, hl].astype(_BF16))
        g2 = jax.nn.sigmoid(pd_scr[:, D_MODEL + h * XA_DH:D_MODEL + (h + 1) * XA_DH])
        mrg_scr[:, hl] += g2 * h_c

    x1_ref[0] = x + _dot(mrg_scr[...].astype(_BF16), wout_ref[...])

    @pl.when(ci == pl.num_programs(1) - 1)
    def _():
        c_out_ref[0] = c_scr[...]
        n_out_ref[0] = n_scr[...]
        m_out_ref[0] = m_scr[...]


def _mixer(x, mem_k, mem_v, c0, n0, m0, wts, *, tc, want_vn):
    B, S, D = x.shape
    sgu_len = min(S, SGU_CHUNK)
    assert S % tc == 0 and tc % CHUNK == 0 and tc % sgu_len == 0
    nchunks = S // tc
    m0p = jnp.broadcast_to(m0[:, :, None], (B, ML_HEADS, 128))
    m0p = jnp.concatenate([m0p, jnp.zeros((B, 8 - ML_HEADS, 128), _F32)], axis=1)

    def body(*refs):
        ins, rest = refs[:18], refs[18:]
        if want_vn:
            outs, scr = rest[:5], rest[5:]
        else:
            outs, scr = rest[:4] + (None,), rest[4:]
        _mixer_kernel(*ins, *outs, *scr, tc=tc, sgu_len=sgu_len)

    const2 = lambda b, c: (0, 0)
    const3 = lambda b, c: (0, 0, 0)
    per_b3 = lambda b, c: (b, 0, 0)
    per_b4 = lambda b, c: (b, 0, 0, 0)
    once = dict(pipeline_mode=pl.Buffered(1))
    in_specs = [
        pl.BlockSpec((1, tc, D), lambda b, c: (b, c, 0)),
        pl.BlockSpec((1, N_MEM, D), per_b3),
        pl.BlockSpec((1, N_MEM, D), per_b3),
        pl.BlockSpec((1, ML_HEADS, ML_DQK, ML_DV), per_b4),
        pl.BlockSpec((1, ML_HEADS, ML_DQK), per_b3),
        pl.BlockSpec((1, 8, 128), per_b3),
        pl.BlockSpec((1, D), const2, **once),
        pl.BlockSpec((D, _P_COLS), const2, **once),
        pl.BlockSpec((1, _P_COLS), const2, **once),
        pl.BlockSpec((D, 8), const2, **once),
        pl.BlockSpec((8, D), const2, **once),
        pl.BlockSpec((1, 8), const2, **once),
        pl.BlockSpec((8, 1), const2, **once),
        pl.BlockSpec((1, D), const2, **once),
        pl.BlockSpec((1, D), const2, **once),
        pl.BlockSpec((SGU_GROUPS, SGU_CHUNK, SGU_CHUNK), const3, **once),
        pl.BlockSpec((SGU_CHUNK, SGU_GROUPS), const2, **once),
        pl.BlockSpec((D, D), const2, **once),
    ]
    out_shape = [
        jax.ShapeDtypeStruct((B, S, D), _F32),
        jax.ShapeDtypeStruct((B, ML_HEADS, ML_DQK, ML_DV), _F32),
        jax.ShapeDtypeStruct((B, ML_HEADS, ML_DQK), _F32),
        jax.ShapeDtypeStruct((B, 8, 128), _F32),
    ]
    out_specs = [
        pl.BlockSpec((1, tc, D), lambda b, c: (b, c, 0)),
        pl.BlockSpec((1, ML_HEADS, ML_DQK, ML_DV), per_b4),
        pl.BlockSpec((1, ML_HEADS, ML_DQK), per_b3),
        pl.BlockSpec((1, 8, 128), per_b3),
    ]
    if want_vn:
        out_shape.append(jax.ShapeDtypeStruct((B, S, D), _F32))
        out_specs.append(pl.BlockSpec((1, tc, D), lambda b, c: (b, c, 0)))
    outs = pl.pallas_call(
        body,
        out_shape=tuple(out_shape),
        grid=(B, nchunks),
        in_specs=in_specs,
        out_specs=tuple(out_specs),
        scratch_shapes=[
            pltpu.VMEM((tc, _SEG_A), _F32),
            pltpu.VMEM((tc, _SEG_B), _F32),
            pltpu.VMEM((tc, _SEG_C), _F32),
            pltpu.VMEM((tc, _SEG_D), _F32),
            pltpu.VMEM((tc, D), _F32),
            pltpu.VMEM((ML_HEADS, ML_DQK, ML_DV), _F32),
            pltpu.VMEM((ML_HEADS, ML_DQK), _F32),
            pltpu.VMEM((8, 128), _F32),
        ],
        compiler_params=pltpu.CompilerParams(
            dimension_semantics=("arbitrary", "arbitrary"), vmem_limit_bytes=_VMEM_LIMIT),
        name="mixer",
    )(x, mem_k, mem_v, c0, n0, m0p, *wts)
    x1, c1, n1, m1p = outs[:4]
    vn = outs[4] if want_vn else None
    return x1, c1, n1, m1p[:, :ML_HEADS, 0], vn


def _batcher_pairs(n):
    pairs = []
    p = 1
    while p < n:
        k = p
        while k >= 1:
            for j in range(k % p, n - k, 2 * k):
                for i in range(min(k, n - j - k)):
                    if (i + j) // (2 * p) == (i + j + k) // (2 * p):
                        pairs.append((i + j, i + j + k))
            k //= 2
        p *= 2
    return pairs


_SORT16 = _batcher_pairs(PEER_NKEYS // 8)
_SEL_UNROLL = 2


def _sorted_columns(c):
    cols = [c[8 * k:8 * k + 8, :] for k in range(PEER_NKEYS // 8)]
    for i, j in _SORT16:
        cols[i], cols[j] = jnp.maximum(cols[i], cols[j]), jnp.minimum(cols[i], cols[j])
    return cols


def _pop_top(cols, out_scr):
    for r in range(PEER_TOPK):
        mx = jnp.max(cols[0], axis=0, keepdims=True)
        out_scr[r:r + 1, :] = mx
        left = PEER_TOPK - 1 - r
        if left:
            eq = cols[0] == mx
            for k in range(left):
                cols[k] = jnp.where(eq, cols[k + 1], cols[k])


def _bf16_pair(x, exact=False):
    u = pltpu.bitcast(x, jnp.uint32)
    if not exact:
        u = u + jnp.uint32(0x7FFF) + ((u >> 16) & jnp.uint32(1))
    hi = u & jnp.uint32(0xFFFF0000)
    return hi | (hi >> 16)


def _peer_select_kernel(x1_ref, gffn_ref, wpq_ref, k1_ref, k2_ref,
                        xn_ref, cnt_ref, p1n_ref, rank_ref, p2_ref,
                        q_scr, c1_scr, c2_scr, v1_scr, v2_scr, n_scr, *, tb):
    xn = _rms(x1_ref[...], gffn_ref[...]).astype(_BF16)
    xn_ref[...] = xn
    q_scr[...] = _dot(xn, wpq_ref[...])
    k1 = k1_ref[...].astype(_BF16)
    k2 = k2_ref[...].astype(_BF16)
    half = PEER_DQ // 2
    K = PEER_TOPK

    def lane_group(h, ls):
        c1 = c1_scr[:, ls]
        c2 = c2_scr[:, ls]
        _pop_top(_sorted_columns(c1), v1_scr)
        _pop_top(_sorted_columns(c2), v2_scr)
        v1 = v1_scr[...]
        lists = [v1_scr[0:8, :] + v2_scr[b:b + 1, :] for b in range(8)]
        ea = v1_scr[8:16, :] + v2_scr[0:1, :]
        eb = v1_scr[0:1, :] + v2_scr[8:16, :]
        cands = lists + [ea, eb]
        tau = None
        for r in range(K):
            tau = jnp.max(jnp.maximum(jnp.maximum(lists[0], ea), eb), axis=0, keepdims=True)
            left = K - 1 - r
            if left:
                eq = lists[0] == tau
                for k in range(min(left, 7)):
                    lists[k] = jnp.where(eq, lists[k + 1], lists[k])
                if left >= 8:
                    lists[7] = jnp.where(eq, _NEG_INF, lists[7])
                ea = jnp.where(ea == tau, _NEG_INF, ea)
                eb = jnp.where(eb == tau, _NEG_INF, eb)
        z = sum(jnp.sum(jnp.where(c >= tau, jnp.exp(c), 0.0), axis=0, keepdims=True) for c in cands)
        n = jnp.zeros(v1.shape, _F32)
        for b in range(K):
            n = jnp.where(v1 + v2_scr[b:b + 1, :] >= tau, float(b + 1), n)
        n_scr[...] = n
        cnt = jnp.zeros(c1.shape, _F32)
        rank2 = jnp.full(c2.shape, float(K), _F32)
        for a in range(K - 1, -1, -1):
            cnt = jnp.where(c1 >= v1_scr[a:a + 1, :], n_scr[a:a + 1, :], cnt)
            rank2 = jnp.where(c2 >= v2_scr[a:a + 1, :], float(a), rank2)
        cnt_ref[h, :, ls] = _bf16_pair(cnt, exact=True)
        p1n_ref[h, :, ls] = _bf16_pair(jnp.exp(c1) / z)
        rank_ref[h, :, ls] = rank2.astype(_BF16)
        p2_ref[h, :, ls] = jnp.exp(c2).astype(_BF16)

    def head(h, carry):
        q0 = pl.multiple_of(h * PEER_DQ, PEER_DQ)
        s1 = _dot_nt(k1, q_scr[:, pl.ds(q0, half)].astype(_BF16))
        s2 = _dot_nt(k2, q_scr[:, pl.ds(q0 + half, half)].astype(_BF16))
        c1_scr[...] = s1 - jnp.max(s1, axis=0, keepdims=True)
        c2_scr[...] = s2 - jnp.max(s2, axis=0, keepdims=True)

        def lane_groups(i, carry):
            for u in range(_SEL_UNROLL):
                lane_group(h, pl.ds(pl.multiple_of((i * _SEL_UNROLL + u) * 128, 128), 128))
            return carry

        lax.fori_loop(0, tb // (128 * _SEL_UNROLL), lane_groups, 0)
        return carry

    lax.fori_loop(0, PEER_HEADS, head, 0)


def _peer_select(x1, g_ffn, wpq_bf, k_sub1, k_sub2, *, tb):
    T, D = x1.shape
    assert T % tb == 0 and tb % (128 * _SEL_UNROLL) == 0
    sel = lambda dt: jax.ShapeDtypeStruct((PEER_HEADS, PEER_NKEYS, T), dt)
    sel_spec = pl.BlockSpec((PEER_HEADS, PEER_NKEYS, tb), lambda i: (0, 0, i))
    const2 = lambda i: (0, 0)
    return pl.pallas_call(
        functools.partial(_peer_select_kernel, tb=tb),
        out_shape=(jax.ShapeDtypeStruct((T, D), _BF16), sel(jnp.uint32), sel(jnp.uint32), sel(_BF16), sel(_BF16)),
        grid=(T // tb,),
        in_specs=[
            pl.BlockSpec((tb, D), lambda i: (i, 0)),
            pl.BlockSpec((1, D), const2),
            pl.BlockSpec((D, PEER_HEADS * PEER_DQ), const2),
            pl.BlockSpec((PEER_NKEYS, PEER_DQ // 2), const2),
            pl.BlockSpec((PEER_NKEYS, PEER_DQ // 2), const2),
        ],
        out_specs=(pl.BlockSpec((tb, D), lambda i: (i, 0)),) + (sel_spec,) * 4,
        scratch_shapes=[
            pltpu.VMEM((tb, PEER_HEADS * PEER_DQ), _F32),
            pltpu.VMEM((PEER_NKEYS, tb), _F32),
            pltpu.VMEM((PEER_NKEYS, tb), _F32),
            pltpu.VMEM((PEER_TOPK, 128), _F32),
            pltpu.VMEM((PEER_TOPK, 128), _F32),
            pltpu.VMEM((PEER_TOPK, 128), _F32),
        ],
        compiler_params=pltpu.CompilerParams(
            dimension_semantics=("arbitrary",), vmem_limit_bytes=_VMEM_LIMIT),
        name="peer_select",
    )(x1, g_ffn.reshape(1, D), wpq_bf, k_sub1, k_sub2)


def _peer_dense_kernel(xn_ref, u_ref, vtp_ref, vtl_ref, cnt_ref, p1n_ref, rank_ref, p2_ref, x1_ref,
                       gfin_ref, y_ref, acc_scr, mix_scr, st_scr, h_scr, *, tb, rows):
    j = pl.program_id(1)
    cur = lax.rem(j, 2)

    @pl.when(j == 0)
    def _():
        acc_scr[...] = jnp.zeros_like(acc_scr)
        mix_scr[...] = jnp.zeros_like(mix_scr)
        h_scr[1] = jnp.zeros(h_scr.shape[1:], _BF16)

    acc_scr[...] += mix_scr[...]

    tile = (PEER_NKEYS, 128)
    words = (PEER_NKEYS // 2, 128)
    rb = 2
    for b in range(rows // rb):
        bs = slice(b * rb * PEER_NKEYS, (b + 1) * rb * PEER_NKEYS)
        st_scr[bs, :] = _dot_nt(u_ref[bs, :], xn_ref[...])
        for r in range(b * rb, (b + 1) * rb):
            es = slice(r * PEER_NKEYS, (r + 1) * PEER_NKEYS)
            for lg in range(tb // 128):
                ls = slice(lg * 128, (lg + 1) * 128)
                gate = jnp.zeros(tile, _BF16)
                for h in range(PEER_HEADS):
                    cnt = pltpu.bitcast(jnp.broadcast_to(cnt_ref[h, r:r + 1, ls], words), _BF16)
                    pn = pltpu.bitcast(jnp.broadcast_to(p1n_ref[h, r:r + 1, ls], words), _BF16)
                    sel = jnp.minimum(jnp.maximum(cnt - rank_ref[h, :, ls], 0), pn)
                    gate = gate + sel * p2_ref[h, :, ls]
                act = jax.nn.gelu(st_scr[es, ls].astype(_BF16))
                h_scr[cur, es, ls] = gate * act
    mix_scr[...] = _dot(vtp_ref[0], h_scr[1 - cur])

    @pl.when(j == pl.num_programs(1) - 1)
    def _():
        acc = acc_scr[...] + mix_scr[...] + _dot(vtl_ref[0], h_scr[cur])
        x2 = x1_ref[...] + acc.T
        y_ref[...] = _rms(x2, gfin_ref[...])


def _peer_dense(xn, u_bf, vt_bf, cnt, p1n, rank2, p2, x1, g_final, *, tb, ec):
    T, D = x1.shape
    rows = ec // PEER_NKEYS
    nchunks = PEER_NEXP // ec
    assert T % tb == 0 and PEER_NEXP % ec == 0 and rows % 8 == 0
    sel_all = pl.BlockSpec((PEER_HEADS, PEER_NKEYS, tb), lambda i, j: (0, 0, i))
    sel_rows = pl.BlockSpec((PEER_HEADS, rows, tb), lambda i, j: (0, j, i))
    tok = pl.BlockSpec((tb, D), lambda i, j: (i, 0))
    return pl.pallas_call(
        functools.partial(_peer_dense_kernel, tb=tb, rows=rows),
        out_shape=jax.ShapeDtypeStruct((T, D), _F32),
        grid=(T // tb, nchunks),
        in_specs=[
            tok,
            pl.BlockSpec((ec, D), lambda i, j: (j, 0)),
            pl.BlockSpec((1, D, ec), lambda i, j: (jnp.maximum(j - 1, 0), 0, 0)),
            pl.BlockSpec((1, D, ec), lambda i, j: (nchunks - 1, 0, 0), pipeline_mode=pl.Buffered(1)),
            sel_rows, sel_rows, sel_all, sel_all,
            tok,
            pl.BlockSpec((1, D), lambda i, j: (0, 0)),
        ],
        out_specs=tok,
        scratch_shapes=[
            pltpu.VMEM((D, tb), _F32),
            pltpu.VMEM((D, tb), _F32),
            pltpu.VMEM((ec, tb), _F32),
            pltpu.VMEM((2, ec, tb), _BF16),
        ],
        compiler_params=pltpu.CompilerParams(
            dimension_semantics=("arbitrary", "arbitrary"), vmem_limit_bytes=_VMEM_LIMIT),
        name="peer_dense",
    )(xn, u_bf, vt_bf, vt_bf, cnt, p1n, rank2, p2, x1, g_final.reshape(1, D))


def _mixer_weights(g_mix, w_in, b_in, g_mlh, g_sgu, w_s, b_s, w_out):
    nq = ML_HEADS * ML_DQK
    nv = ML_HEADS * ML_DV
    o_q, o_k, o_v = 0, nq, 2 * nq
    o_ig = o_v + nv
    o_fg = o_ig + ML_HEADS
    o_og = o_fg + ML_HEADS
    o_su = o_og + nv
    o_sv = o_su + D_MODEL
    o_xq = o_sv + D_MODEL
    o_gt = o_xq + D_MODEL
    gate = lambda b: slice(o_gt + b * D_MODEL, o_gt + (b + 1) * D_MODEL)
    main = lambda a: jnp.concatenate(
        [a[..., o_q:o_ig],
         a[..., o_og:o_su], a[..., gate(0)],
         a[..., o_su:o_xq], a[..., gate(1)],
         a[..., o_xq:o_gt], a[..., gate(2)]], axis=-1)
    w_if = w_in[:, o_ig:o_og]
    b_if = b_in[o_ig:o_og]
    return (
        g_mix.reshape(1, D_MODEL),
        main(w_in).astype(_BF16),
        main(b_in).reshape(1, _P_COLS),
        w_if.astype(_BF16),
        w_if.T.astype(_BF16),
        b_if.reshape(1, 8),
        b_if.reshape(8, 1),
        g_mlh.reshape(1, D_MODEL),
        g_sgu.reshape(1, D_MODEL),
        w_s,
        b_s.T,
        w_out.astype(_BF16),
    )


def kernel(x_prompt, x_sample, mem_prompt, cache_mem_k, cache_mem_v, state_mlstm_C, state_mlstm_n,
           state_mlstm_m, g_mix, w_in, b_in, g_mlh, g_sgu, w_s, b_s, g_mem, w_mk, w_mv, w_out, g_ffn,
           w_pq, k_sub1, k_sub2, u_exp, v_exp, g_final):
    depth = g_mix.shape[0]
    assert depth == 1
    l = 0
    Bp, Sp, D = x_prompt.shape
    Bs, Ss, _ = x_sample.shape

    wts = _mixer_weights(g_mix[l], w_in[l], b_in[l], g_mlh[l], g_sgu[l], w_s[l], b_s[l], w_out[l])

    mk, mv = _mem_kv(mem_prompt, g_mem[l], w_mk[l], w_mv[l])
    zc = jnp.zeros((Bp, ML_HEADS, ML_DQK, ML_DV), _F32)
    zn = jnp.zeros((Bp, ML_HEADS, ML_DQK), _F32)
    zm = jnp.zeros((Bp, ML_HEADS), _F32)
    x1p, cp, np_, mp, _ = _mixer(x_prompt, mk, mv, zc, zn, zm, wts,
                                 tc=min(_MIXER_TC, Sp), want_vn=False)
    x1s, cs, ns, ms, vn = _mixer(
        x_sample, cache_mem_k[l].reshape(Bs, N_MEM, D), cache_mem_v[l].reshape(Bs, N_MEM, D),
        state_mlstm_C[l], state_mlstm_n[l], state_mlstm_m[l], wts,
        tc=min(_MIXER_TC, Ss), want_vn=True)

    wpq_bf = w_pq[l].astype(_BF16)
    u_bf = u_exp[l].astype(_BF16)
    vt_bf = v_exp[l].astype(_BF16).reshape(PEER_NEXP // _PEER_EC, _PEER_EC, D).transpose(0, 2, 1)

    def peer(x1):
        B, S, _ = x1.shape
        x1 = x1.reshape(B * S, D)
        xn, cnt, p1n, rank2, p2 = _peer_select(x1, g_ffn[l], wpq_bf, k_sub1[l], k_sub2[l], tb=_PEER_TB)
        y = _peer_dense(xn, u_bf, vt_bf, cnt, p1n, rank2, p2, x1, g_final, tb=_PEER_TB, ec=_PEER_EC)
        return y.reshape(B, S, D)

    hs = (XA_HEADS, XA_DH)
    return (peer(x1p), peer(x1s), cp[None], np_[None], mp[None],
            mk.reshape(1, Bp, N_MEM, *hs), mv.reshape(1, Bp, N_MEM, *hs),
            cs[None], ns[None], ms[None], vn[None])
```

```python
import functools

import jax
import jax.numpy as jnp
from jax import lax
from jax.experimental import pallas as pl
from jax.experimental.pallas import tpu as pltpu

D_MODEL = 1024
EPS = 1e-6
CHUNK = 64
N_MEM = 256
ML_HEADS = 4
ML_DQK = 128
ML_DV = D_MODEL // ML_HEADS
SGU_CHUNK = 128
SGU_GROUPS = 4
SGU_GDIM = D_MODEL // SGU_GROUPS
XA_HEADS = 4
XA_DH = D_MODEL // XA_HEADS
PEER_HEADS = 8
PEER_NKEYS = 128
PEER_DQ = 256
PEER_TOPK = 16
PEER_NEXP = PEER_NKEYS * PEER_NKEYS

_SEG_A = ML_HEADS * (2 * ML_DQK + ML_DV)
_SEG_B = 2 * D_MODEL
_SEG_C = 3 * D_MODEL
_SEG_D = 2 * D_MODEL
_SEG0 = (0, _SEG_A, _SEG_A + _SEG_B, _SEG_A + _SEG_B + _SEG_C)
_P_COLS = _SEG_A + _SEG_B + _SEG_C + _SEG_D
_K0 = ML_HEADS * ML_DQK
_V0 = 2 * ML_HEADS * ML_DQK

_VMEM_LIMIT = 56 * 1024 * 1024

_MIXER_TC = 256
_PEER_TB = 512
_PEER_EC = 2048

_BF16 = jnp.bfloat16
_F32 = jnp.float32
_NEG_INF = float("-inf")


def _rms(xf, g):
    return xf * lax.rsqrt(jnp.mean(xf * xf, axis=-1, keepdims=True) + EPS) * g


def _dot(a, b):
    return jnp.dot(a, b, preferred_element_type=_F32)


def _dot_nt(a, b):
    return lax.dot_general(a, b, (((1,), (1,)), ((), ())), preferred_element_type=_F32)


def _bmm(a, b, ca, cb):
    return lax.dot_general(a, b, (((ca,), (cb,)), ((0,), (0,))), preferred_element_type=_F32)


def _mem_kv_kernel(mem_ref, g_ref, wk_ref, wv_ref, k_ref, v_ref):
    mn = _rms(mem_ref[0], g_ref[...]).astype(_BF16)
    k_ref[0] = _dot(mn, wk_ref[...])
    v_ref[0] = _dot(mn, wv_ref[...])


def _mem_kv(mem, g_mem, w_mk, w_mv):
    B = mem.shape[0]
    full = lambda b: (0, 0)
    return pl.pallas_call(
        _mem_kv_kernel,
        out_shape=(jax.ShapeDtypeStruct((B, N_MEM, D_MODEL), _F32),) * 2,
        grid=(B,),
        in_specs=[
            pl.BlockSpec((1, N_MEM, D_MODEL), lambda b: (b, 0, 0)),
            pl.BlockSpec((1, D_MODEL), full),
            pl.BlockSpec((D_MODEL, D_MODEL), full),
            pl.BlockSpec((D_MODEL, D_MODEL), full),
        ],
        out_specs=(pl.BlockSpec((1, N_MEM, D_MODEL), lambda b: (b, 0, 0)),) * 2,
        compiler_params=pltpu.CompilerParams(
            dimension_semantics=("arbitrary",), vmem_limit_bytes=_VMEM_LIMIT),
        name="mem_kv",
    )(mem, g_mem.reshape(1, D_MODEL), w_mk.astype(_BF16), w_mv.astype(_BF16))


def _mixer_kernel(x_ref, mk_ref, mv_ref, c0_ref, n0_ref, m0_ref, gmix_ref, win_ref, bin_ref,
                  wif_ref, wift_ref, bif_ref, bift_ref, gmlh_ref, gsgu_ref, ws_ref, bst_ref,
                  wout_ref,
                  x1_ref, c_out_ref, n_out_ref, m_out_ref, vn_ref,
                  pa_scr, pb_scr, pc_scr, pd_scr, mrg_scr, c_scr, n_scr, m_scr, *, tc, sgu_len):
    ci = pl.program_id(1)
    nsub = tc // CHUNK

    @pl.when(ci == 0)
    def _():
        c_scr[...] = c0_ref[0]
        n_scr[...] = n0_ref[0]
        m_scr[...] = m0_ref[0]

    x = x_ref[0]
    xn = _rms(x, gmix_ref[...]).astype(_BF16)

    def project(seg, out_scr):
        cols = slice(_SEG0[seg], _SEG0[seg] + out_scr.shape[1])
        out_scr[...] = _dot(xn, win_ref[:, cols]) + bin_ref[:, cols]

    project(0, pa_scr)
    gif = _dot(xn, wif_ref[...]) + bif_ref[...]
    gift = _dot_nt(wift_ref[...], xn) + bift_ref[...]

    row = lax.broadcasted_iota(jnp.int32, (CHUNK, CHUNK), 0)
    col = lax.broadcasted_iota(jnp.int32, (CHUNK, CHUNK), 1)
    tril = row >= col
    tril_f = tril.astype(_F32)
    triu_f = (row <= col).astype(_F32)
    stack = lambda f: jnp.stack([f(h) for h in range(ML_HEADS)])
    for j in range(nsub):
        r0 = j * CHUNK
        rows = slice(r0, r0 + CHUNK)
        ig_c = gif[rows, 0:ML_HEADS]
        lf_c = jax.nn.log_sigmoid(gif[rows, ML_HEADS:2 * ML_HEADS])
        ig_r = gift[0:ML_HEADS, rows]
        lf_r = jax.nn.log_sigmoid(gift[ML_HEADS:2 * ML_HEADS, rows])
        b_c = jnp.dot(tril_f, lf_c, precision=lax.Precision.HIGHEST,
                      preferred_element_type=_F32)
        b_r = jnp.dot(lf_r, triu_f, precision=lax.Precision.HIGHEST,
                      preferred_element_type=_F32)
        q = stack(lambda h: pa_scr[rows, h * ML_DQK:(h + 1) * ML_DQK]).astype(_BF16)
        k = stack(lambda h: pa_scr[rows, _K0 + h * ML_DQK:_K0 + (h + 1) * ML_DQK]) * (ML_DQK ** -0.5)
        v = stack(lambda h: pa_scr[rows, _V0 + h * ML_DV:_V0 + (h + 1) * ML_DV]).astype(_BF16)
        c_st = c_scr[...]
        n_st = stack(lambda h: n_scr[h:h + 1, :])
        m_st = stack(lambda h: m_scr[h:h + 1, 0:1])
        bc = stack(lambda h: b_c[:, h:h + 1])
        br = stack(lambda h: b_r[h:h + 1, :])
        ig_row = stack(lambda h: ig_r[h:h + 1, :])
        ig_col = stack(lambda h: ig_c[:, h:h + 1])
        logd = jnp.where(tril[None], bc - br + ig_row, _NEG_INF)
        log_prev = bc + m_st
        m_t = jnp.maximum(log_prev, jnp.max(logd, axis=2, keepdims=True))
        a = _bmm(q, k.astype(_BF16), 2, 2) * jnp.exp(logd - m_t)
        wp = jnp.exp(log_prev - m_t)
        num = _bmm(a.astype(_BF16), v, 2, 1) + wp * _bmm(q, c_st.astype(_BF16), 2, 1)
        qn = jnp.sum(q.astype(_F32) * n_st, axis=2, keepdims=True)
        den = jnp.sum(a, axis=2, keepdims=True) + wp * qn
        hh = num / jnp.maximum(jnp.abs(den), jnp.exp(-m_t))
        hn = hh * lax.rsqrt(jnp.mean(hh * hh, axis=2, keepdims=True) + EPS)
        for h in range(ML_HEADS):
            mrg_scr[rows, h * ML_DV:(h + 1) * ML_DV] = hn[h]
        b_last = bc[:, CHUNK - 1:CHUNK, :]
        log_in = b_last - bc + ig_col
        m_new = jnp.maximum(b_last + m_st, jnp.max(log_in, axis=1, keepdims=True))
        wi = jnp.exp(log_in - m_new)
        wc = jnp.exp(b_last + m_st - m_new)
        kw = wi * k
        c_scr[...] = wc * c_st + _bmm(kw.astype(_BF16), v, 1, 1)
        n_new = wc * n_st + jnp.sum(kw, axis=1, keepdims=True)
        for h in range(ML_HEADS):
            n_scr[h:h + 1, :] = n_new[h]
            m_scr[h:h + 1, :] = jnp.broadcast_to(m_new[h], (1, 128))

    project(1, pb_scr)
    h_a = jax.nn.sigmoid(pb_scr[:, 0:D_MODEL]) * mrg_scr[...] * gmlh_ref[...]
    mrg_scr[...] = jax.nn.sigmoid(pb_scr[:, D_MODEL:2 * D_MODEL]) * h_a

    project(2, pc_scr)
    sv = jax.nn.gelu(pc_scr[:, D_MODEL:2 * D_MODEL])
    vn = _rms(sv, gsgu_ref[...])
    if vn_ref is not None:
        vn_ref[0] = vn
    vnb = vn.astype(_BF16)
    rs = lax.broadcasted_iota(jnp.int32, (sgu_len, sgu_len), 0)
    cs = lax.broadcasted_iota(jnp.int32, (sgu_len, sgu_len), 1)
    for g in range(SGU_GROUPS):
        wsg = jnp.where(rs >= cs, ws_ref[g, 0:sgu_len, 0:sgu_len], 0.0).astype(_BF16)
        bsg = bst_ref[0:sgu_len, g:g + 1]
        gl = slice(g * SGU_GDIM, (g + 1) * SGU_GDIM)
        for c in range(tc // sgu_len):
            rows = slice(c * sgu_len, (c + 1) * sgu_len)
            mix = _dot(wsg, vnb[rows, gl]) + bsg
            u = jax.nn.gelu(pc_scr[rows, g * SGU_GDIM:(g + 1) * SGU_GDIM])
            g1 = jax.nn.sigmoid(pc_scr[rows, 2 * D_MODEL + g * SGU_GDIM:2 * D_MODEL + (g + 1) * SGU_GDIM])
            mrg_scr[rows, gl] += g1 * (u * mix)

    project(3, pd_scr)
    for h in range(XA_HEADS):
        hl = slice(h * XA_DH, (h + 1) * XA_DH)
        xq = pd_scr[:, h * XA_DH:(h + 1) * XA_DH].astype(_BF16)
        sc = _dot_nt(xq, mk_ref[0, :, hl].astype(_BF16)) * (XA_DH ** -0.5)
        sc = sc - jnp.max(sc, axis=1, keepdims=True)
        e = jnp.exp(sc)
        att = e / jnp.sum(e, axis=1, keepdims=True)
        h_c = _dot(att.astype(_BF16), mv_ref[0, :, hl].astype(_BF16))
        g2 = jax.nn.sigmoid(pd_scr[:, D_MODEL + h * XA_DH:D_MODEL + (h + 1) * XA_DH])
        mrg_scr[:, hl] += g2 * h_c

    x1_ref[0] = x + _dot(mrg_scr[...].astype(_BF16), wout_ref[...])

    @pl.when(ci == pl.num_programs(1) - 1)
    def _():
        c_out_ref[0] = c_scr[...]
        n_out_ref[0] = n_scr[...]
        m_out_ref[0] = m_scr[...]


def _mixer(x, mem_k, mem_v, c0, n0, m0, wts, *, tc, want_vn):
    B, S, D = x.shape
    sgu_len = min(S, SGU_CHUNK)
    assert S % tc == 0 and tc % CHUNK == 0 and tc % sgu_len == 0
    nchunks = S // tc
    m0p = jnp.broadcast_to(m0[:, :, None], (B, ML_HEADS, 128))
    m0p = jnp.concatenate([m0p, jnp.zeros((B, 8 - ML_HEADS, 128), _F32)], axis=1)

    def body(*refs):
        ins, rest = refs[:18], refs[18:]
        if want_vn:
            outs, scr = rest[:5], rest[5:]
        else:
            outs, scr = rest[:4] + (None,), rest[4:]
        _mixer_kernel(*ins, *outs, *scr, tc=tc, sgu_len=sgu_len)

    const2 = lambda b, c: (0, 0)
    const3 = lambda b, c: (0, 0, 0)
    per_b3 = lambda b, c: (b, 0, 0)
    per_b4 = lambda b, c: (b, 0, 0, 0)
    once = dict(pipeline_mode=pl.Buffered(1))
    in_specs = [
        pl.BlockSpec((1, tc, D), lambda b, c: (b, c, 0)),
        pl.BlockSpec((1, N_MEM, D), per_b3),
        pl.BlockSpec((1, N_MEM, D), per_b3),
        pl.BlockSpec((1, ML_HEADS, ML_DQK, ML_DV), per_b4),
        pl.BlockSpec((1, ML_HEADS, ML_DQK), per_b3),
        pl.BlockSpec((1, 8, 128), per_b3),
        pl.BlockSpec((1, D), const2, **once),
        pl.BlockSpec((D, _P_COLS), const2, **once),
        pl.BlockSpec((1, _P_COLS), const2, **once),
        pl.BlockSpec((D, 8), const2, **once),
        pl.BlockSpec((8, D), const2, **once),
        pl.BlockSpec((1, 8), const2, **once),
        pl.BlockSpec((8, 1), const2, **once),
        pl.BlockSpec((1, D), const2, **once),
        pl.BlockSpec((1, D), const2, **once),
        pl.BlockSpec((SGU_GROUPS, SGU_CHUNK, SGU_CHUNK), const3, **once),
        pl.BlockSpec((SGU_CHUNK, SGU_GROUPS), const2, **once),
        pl.BlockSpec((D, D), const2, **once),
    ]
    out_shape = [
        jax.ShapeDtypeStruct((B, S, D), _F32),
        jax.ShapeDtypeStruct((B, ML_HEADS, ML_DQK, ML_DV), _F32),
        jax.ShapeDtypeStruct((B, ML_HEADS, ML_DQK), _F32),
        jax.ShapeDtypeStruct((B, 8, 128), _F32),
    ]
    out_specs = [
        pl.BlockSpec((1, tc, D), lambda b, c: (b, c, 0)),
        pl.BlockSpec((1, ML_HEADS, ML_DQK, ML_DV), per_b4),
        pl.BlockSpec((1, ML_HEADS, ML_DQK), per_b3),
        pl.BlockSpec((1, 8, 128), per_b3),
    ]
    if want_vn:
        out_shape.append(jax.ShapeDtypeStruct((B, S, D), _F32))
        out_specs.append(pl.BlockSpec((1, tc, D), lambda b, c: (b, c, 0)))
    outs = pl.pallas_call(
        body,
        out_shape=tuple(out_shape),
        grid=(B, nchunks),
        in_specs=in_specs,
        out_specs=tuple(out_specs),
        scratch_shapes=[
            pltpu.VMEM((tc, _SEG_A), _F32),
            pltpu.VMEM((tc, _SEG_B), _F32),
            pltpu.VMEM((tc, _SEG_C), _F32),
            pltpu.VMEM((tc, _SEG_D), _F32),
            pltpu.VMEM((tc, D), _F32),
            pltpu.VMEM((ML_HEADS, ML_DQK, ML_DV), _F32),
            pltpu.VMEM((ML_HEADS, ML_DQK), _F32),
            pltpu.VMEM((8, 128), _F32),
        ],
        compiler_params=pltpu.CompilerParams(
            dimension_semantics=("arbitrary", "arbitrary"), vmem_limit_bytes=_VMEM_LIMIT),
        name="mixer",
    )(x, mem_k, mem_v, c0, n0, m0p, *wts)
    x1, c1, n1, m1p = outs[:4]
    vn = outs[4] if want_vn else None
    return x1, c1, n1, m1p[:, :ML_HEADS, 0], vn


def _batcher_pairs(n):
    pairs = []
    p = 1
    while p < n:
        k = p
        while k >= 1:
            for j in range(k % p, n - k, 2 * k):
                for i in range(min(k, n - j - k)):
                    if (i + j) // (2 * p) == (i + j + k) // (2 * p):
                        pairs.append((i + j, i + j + k))
            k //= 2
        p *= 2
    return pairs


_SORT16 = _batcher_pairs(PEER_NKEYS // 8)
_SEL_UNROLL = 2


def _sorted_columns(c):
    cols = [c[8 * k:8 * k + 8, :] for k in range(PEER_NKEYS // 8)]
    for i, j in _SORT16:
        cols[i], cols[j] = jnp.maximum(cols[i], cols[j]), jnp.minimum(cols[i], cols[j])
    return cols


def _pop_top(cols, out_scr):
    for r in range(PEER_TOPK):
        mx = jnp.max(cols[0], axis=0, keepdims=True)
        out_scr[r:r + 1, :] = mx
        left = PEER_TOPK - 1 - r
        if left:
            eq = cols[0] == mx
            for k in range(left):
                cols[k] = jnp.where(eq, cols[k + 1], cols[k])


def _bf16_pair(x, exact=False):
    u = pltpu.bitcast(x, jnp.uint32)
    if not exact:
        u = u + jnp.uint32(0x7FFF) + ((u >> 16) & jnp.uint32(1))
    hi = u & jnp.uint32(0xFFFF0000)
    return hi | (hi >> 16)


def _peer_select_kernel(x1_ref, gffn_ref, wpq_ref, k1_ref, k2_ref,
                        xn_ref, cnt_ref, p1n_ref, rank_ref, p2_ref,
                        q_scr, c1_scr, c2_scr, v1_scr, v2_scr, n_scr, *, tb):
    xn = _rms(x1_ref[...], gffn_ref[...]).astype(_BF16)
    xn_ref[...] = xn
    q_scr[...] = _dot(xn, wpq_ref[...])
    k1 = k1_ref[...].astype(_BF16)
    k2 = k2_ref[...].astype(_BF16)
    half = PEER_DQ // 2
    K = PEER_TOPK

    def lane_group(h, ls):
        c1 = c1_scr[:, ls]
        c2 = c2_scr[:, ls]
        _pop_top(_sorted_columns(c1), v1_scr)
        _pop_top(_sorted_columns(c2), v2_scr)
        v1 = v1_scr[...]
        lists = [v1_scr[0:8, :] + v2_scr[b:b + 1, :] for b in range(8)]
        ea = v1_scr[8:16, :] + v2_scr[0:1, :]
        eb = v1_scr[0:1, :] + v2_scr[8:16, :]
        cands = lists + [ea, eb]
        tau = None
        for r in range(K):
            tau = jnp.max(jnp.maximum(jnp.maximum(lists[0], ea), eb), axis=0, keepdims=True)
            left = K - 1 - r
            if left:
                eq = lists[0] == tau
                for k in range(min(left, 7)):
                    lists[k] = jnp.where(eq, lists[k + 1], lists[k])
                if left >= 8:
                    lists[7] = jnp.where(eq, _NEG_INF, lists[7])
                ea = jnp.where(ea == tau, _NEG_INF, ea)
                eb = jnp.where(eb == tau, _NEG_INF, eb)
        z = sum(jnp.sum(jnp.where(c >= tau, jnp.exp(c), 0.0), axis=0, keepdims=True) for c in cands)
        n = jnp.zeros(v1.shape, _F32)
        for b in range(K):
            n = jnp.where(v1 + v2_scr[b:b + 1, :] >= tau, float(b + 1), n)
        n_scr[...] = n
        cnt = jnp.zeros(c1.shape, _F32)
        rank2 = jnp.full(c2.shape, float(K), _F32)
        for a in range(K - 1, -1, -1):
            cnt = jnp.where(c1 >= v1_scr[a:a + 1, :], n_scr[a:a + 1, :], cnt)
            rank2 = jnp.where(c2 >= v2_scr[a:a + 1, :], float(a), rank2)
        cnt_ref[h, :, ls] = _bf16_pair(cnt, exact=True)
        p1n_ref[h, :, ls] = _bf16_pair(jnp.exp(c1) / z)
        rank_ref[h, :, ls] = rank2.astype(_BF16)
        p2_ref[h, :, ls] = jnp.exp(c2).astype(_BF16)

    def head(h, carry):
        q0 = pl.multiple_of(h * PEER_DQ, PEER_DQ)
        s1 = _dot_nt(k1, q_scr[:, pl.ds(q0, half)].astype(_BF16))
        s2 = _dot_nt(k2, q_scr[:, pl.ds(q0 + half, half)].astype(_BF16))
        c1_scr[...] = s1 - jnp.max(s1, axis=0, keepdims=True)
        c2_scr[...] = s2 - jnp.max(s2, axis=0, keepdims=True)

        def lane_groups(i, carry):
            for u in range(_SEL_UNROLL):
                lane_group(h, pl.ds(pl.multiple_of((i * _SEL_UNROLL + u) * 128, 128), 128))
            return carry

        lax.fori_loop(0, tb // (128 * _SEL_UNROLL), lane_groups, 0)
        return carry

    lax.fori_loop(0, PEER_HEADS, head, 0)


def _peer_select(x1, g_ffn, wpq_bf, k_sub1, k_sub2, *, tb):
    T, D = x1.shape
    assert T % tb == 0 and tb % (128 * _SEL_UNROLL) == 0
    sel = lambda dt: jax.ShapeDtypeStruct((PEER_HEADS, PEER_NKEYS, T), dt)
    sel_spec = pl.BlockSpec((PEER_HEADS, PEER_NKEYS, tb), lambda i: (0, 0, i))
    const2 = lambda i: (0, 0)
    return pl.pallas_call(
        functools.partial(_peer_select_kernel, tb=tb),
        out_shape=(jax.ShapeDtypeStruct((T, D), _BF16), sel(jnp.uint32), sel(jnp.uint32), sel(_BF16), sel(_BF16)),
        grid=(T // tb,),
        in_specs=[
            pl.BlockSpec((tb, D), lambda i: (i, 0)),
            pl.BlockSpec((1, D), const2),
            pl.BlockSpec((D, PEER_HEADS * PEER_DQ), const2),
            pl.BlockSpec((PEER_NKEYS, PEER_DQ // 2), const2),
            pl.BlockSpec((PEER_NKEYS, PEER_DQ // 2), const2),
        ],
        out_specs=(pl.BlockSpec((tb, D), lambda i: (i, 0)),) + (sel_spec,) * 4,
        scratch_shapes=[
            pltpu.VMEM((tb, PEER_HEADS * PEER_DQ), _F32),
            pltpu.VMEM((PEER_NKEYS, tb), _F32),
            pltpu.VMEM((PEER_NKEYS, tb), _F32),
            pltpu.VMEM((PEER_TOPK, 128), _F32),
            pltpu.VMEM((PEER_TOPK, 128), _F32),
            pltpu.VMEM((PEER_TOPK, 128), _F32),
        ],
        compiler_params=pltpu.CompilerParams(
            dimension_semantics=("arbitrary",), vmem_limit_bytes=_VMEM_LIMIT),
        name="peer_select",
    )(x1, g_ffn.reshape(1, D), wpq_bf, k_sub1, k_sub2)


def _peer_dense_kernel(xn_ref, u_ref, vtp_ref, vtl_ref, cnt_ref, p1n_ref, rank_ref, p2_ref, x1_ref,
                       gfin_ref, y_ref, acc_scr, st_scr, h_scr, *, tb, rows):
    j = pl.program_id(1)
    cur = lax.rem(j, 2)

    @pl.when(j == 0)
    def _():
        acc_scr[...] = jnp.zeros_like(acc_scr)
        h_scr[1] = jnp.zeros(h_scr.shape[1:], _BF16)

    tile = (PEER_NKEYS, 128)
    words = (PEER_NKEYS // 2, 128)
    rb = 4
    for b in range(rows // rb):
        bs = slice(b * rb * PEER_NKEYS, (b + 1) * rb * PEER_NKEYS)
        st_scr[bs, :] = _dot_nt(u_ref[bs, :], xn_ref[...])
        for r in range(b * rb, (b + 1) * rb):
            es = slice(r * PEER_NKEYS, (r + 1) * PEER_NKEYS)
            for lg in range(tb // 128):
                ls = slice(lg * 128, (lg + 1) * 128)
                gate = jnp.zeros(tile, _BF16)
                for h in range(PEER_HEADS):
                    cnt = pltpu.bitcast(jnp.broadcast_to(cnt_ref[h, r:r + 1, ls], words), _BF16)
                    pn = pltpu.bitcast(jnp.broadcast_to(p1n_ref[h, r:r + 1, ls], words), _BF16)
                    sel = jnp.minimum(jnp.maximum(cnt - rank_ref[h, :, ls], 0), pn)
                    gate = gate + sel * p2_ref[h, :, ls]
                act = jax.nn.gelu(st_scr[es, ls].astype(_BF16))
                h_scr[cur, es, ls] = gate * act
    acc_scr[...] += _dot(vtp_ref[0], h_scr[1 - cur])

    @pl.when(j == pl.num_programs(1) - 1)
    def _():
        acc = acc_scr[...] + _dot(vtl_ref[0], h_scr[cur])
        x2 = x1_ref[...] + acc.T
        y_ref[...] = _rms(x2, gfin_ref[...])


def _peer_dense(xn, u_bf, vt_bf, cnt, p1n, rank2, p2, x1, g_final, *, tb, ec):
    T, D = x1.shape
    rows = ec // PEER_NKEYS
    nchunks = PEER_NEXP // ec
    assert T % tb == 0 and PEER_NEXP % ec == 0 and rows % 8 == 0
    sel_all = pl.BlockSpec((PEER_HEADS, PEER_NKEYS, tb), lambda i, j: (0, 0, i))
    sel_rows = pl.BlockSpec((PEER_HEADS, rows, tb), lambda i, j: (0, j, i))
    tok = pl.BlockSpec((tb, D), lambda i, j: (i, 0))
    return pl.pallas_call(
        functools.partial(_peer_dense_kernel, tb=tb, rows=rows),
        out_shape=jax.ShapeDtypeStruct((T, D), _F32),
        grid=(T // tb, nchunks),
        in_specs=[
            tok,
            pl.BlockSpec((ec, D), lambda i, j: (j, 0)),
            pl.BlockSpec((1, D, ec), lambda i, j: (jnp.maximum(j - 1, 0), 0, 0)),
            pl.BlockSpec((1, D, ec), lambda i, j: (nchunks - 1, 0, 0), pipeline_mode=pl.Buffered(1)),
            sel_rows, sel_rows, sel_all, sel_all,
            tok,
            pl.BlockSpec((1, D), lambda i, j: (0, 0)),
        ],
        out_specs=tok,
        scratch_shapes=[
            pltpu.VMEM((D, tb), _F32),
            pltpu.VMEM((ec, tb), _F32),
            pltpu.VMEM((2, ec, tb), _BF16),
        ],
        compiler_params=pltpu.CompilerParams(
            dimension_semantics=("arbitrary", "arbitrary"), vmem_limit_bytes=_VMEM_LIMIT),
        name="peer_dense",
    )(xn, u_bf, vt_bf, vt_bf, cnt, p1n, rank2, p2, x1, g_final.reshape(1, D))


def _mixer_weights(g_mix, w_in, b_in, g_mlh, g_sgu, w_s, b_s, w_out):
    nq = ML_HEADS * ML_DQK
    nv = ML_HEADS * ML_DV
    o_q, o_k, o_v = 0, nq, 2 * nq
    o_ig = o_v + nv
    o_fg = o_ig + ML_HEADS
    o_og = o_fg + ML_HEADS
    o_su = o_og + nv
    o_sv = o_su + D_MODEL
    o_xq = o_sv + D_MODEL
    o_gt = o_xq + D_MODEL
    gate = lambda b: slice(o_gt + b * D_MODEL, o_gt + (b + 1) * D_MODEL)
    main = lambda a: jnp.concatenate(
        [a[..., o_q:o_ig],
         a[..., o_og:o_su], a[..., gate(0)],
         a[..., o_su:o_xq], a[..., gate(1)],
         a[..., o_xq:o_gt], a[..., gate(2)]], axis=-1)
    w_if = w_in[:, o_ig:o_og]
    b_if = b_in[o_ig:o_og]
    return (
        g_mix.reshape(1, D_MODEL),
        main(w_in).astype(_BF16),
        main(b_in).reshape(1, _P_COLS),
        w_if.astype(_BF16),
        w_if.T.astype(_BF16),
        b_if.reshape(1, 8),
        b_if.reshape(8, 1),
        g_mlh.reshape(1, D_MODEL),
        g_sgu.reshape(1, D_MODEL),
        w_s,
        b_s.T,
        w_out.astype(_BF16),
    )


def kernel(x_prompt, x_sample, mem_prompt, cache_mem_k, cache_mem_v, state_mlstm_C, state_mlstm_n,
           state_mlstm_m, g_mix, w_in, b_in, g_mlh, g_sgu, w_s, b_s, g_mem, w_mk, w_mv, w_out, g_ffn,
           w_pq, k_sub1, k_sub2, u_exp, v_exp, g_final):
    depth = g_mix.shape[0]
    assert depth == 1
    l = 0
    Bp, Sp, D = x_prompt.shape
    Bs, Ss, _ = x_sample.shape

    wts = _mixer_weights(g_mix[l], w_in[l], b_in[l], g_mlh[l], g_sgu[l], w_s[l], b_s[l], w_out[l])

    mk, mv = _mem_kv(mem_prompt, g_mem[l], w_mk[l], w_mv[l])
    zc = jnp.zeros((Bp, ML_HEADS, ML_DQK, ML_DV), _F32)
    zn = jnp.zeros((Bp, ML_HEADS, ML_DQK), _F32)
    zm = jnp.zeros((Bp, ML_HEADS), _F32)
    x1p, cp, np_, mp, _ = _mixer(x_prompt, mk, mv, zc, zn, zm, wts,
                                 tc=min(_MIXER_TC, Sp), want_vn=False)
    x1s, cs, ns, ms, vn = _mixer(
        x_sample, cache_mem_k[l].reshape(Bs, N_MEM, D), cache_mem_v[l].reshape(Bs, N_MEM, D),
        state_mlstm_C[l], state_mlstm_n[l], state_mlstm_m[l], wts,
        tc=min(_MIXER_TC, Ss), want_vn=True)

    wpq_bf = w_pq[l].astype(_BF16)
    u_bf = u_exp[l].astype(_BF16)
    vt_bf = v_exp[l].astype(_BF16).reshape(PEER_NEXP // _PEER_EC, _PEER_EC, D).transpose(0, 2, 1)

    def peer(x1):
        B, S, _ = x1.shape
        x1 = x1.reshape(B * S, D)
        xn, cnt, p1n, rank2, p2 = _peer_select(x1, g_ffn[l], wpq_bf, k_sub1[l], k_sub2[l], tb=_PEER_TB)
        y = _peer_dense(xn, u_bf, vt_bf, cnt, p1n, rank2, p2, x1, g_final, tb=_PEER_TB, ec=_PEER_EC)
        return y.reshape(B, S, D)

    hs = (XA_HEADS, XA_DH)
    return (peer(x1p), peer(x1s), cp[None], np_[None], mp[None],
            mk.reshape(1, Bp, N_MEM, *hs), mv.reshape(1, Bp, N_MEM, *hs),
            cs[None], ns[None], ms[None], vn[None])
```

```python
import functools

import jax
import jax.numpy as jnp
from jax import lax
from jax.experimental import pallas as pl
from jax.experimental.pallas import tpu as pltpu

D_MODEL = 1024
EPS = 1e-6
CHUNK = 64
N_MEM = 256
ML_HEADS = 4
ML_DQK = 128
ML_DV = D_MODEL // ML_HEADS
SGU_CHUNK = 128
SGU_GROUPS = 4
SGU_GDIM = D_MODEL // SGU_GROUPS
XA_HEADS = 4
XA_DH = D_MODEL // XA_HEADS
PEER_HEADS = 8
PEER_NKEYS = 128
PEER_DQ = 256
PEER_TOPK = 16
PEER_NEXP = PEER_NKEYS * PEER_NKEYS

_SEG_A = ML_HEADS * (2 * ML_DQK + ML_DV)
_SEG_B = 2 * D_MODEL
_SEG_C = 3 * D_MODEL
_SEG_D = 2 * D_MODEL
_SEG0 = (0, _SEG_A, _SEG_A + _SEG_B, _SEG_A + _SEG_B + _SEG_C)
_P_COLS = _SEG_A + _SEG_B + _SEG_C + _SEG_D
_K0 = ML_HEADS * ML_DQK
_V0 = 2 * ML_HEADS * ML_DQK

_VMEM_LIMIT = 56 * 1024 * 1024

_MIXER_TC = 256
_PEER_TB = 512
_PEER_EC = 2048

_BF16 = jnp.bfloat16
_F32 = jnp.float32
_NEG_INF = float("-inf")


def _rms(xf, g):
    return xf * lax.rsqrt(jnp.mean(xf * xf, axis=-1, keepdims=True) + EPS) * g


def _dot(a, b):
    return jnp.dot(a, b, preferred_element_type=_F32)


def _dot_nt(a, b):
    return lax.dot_general(a, b, (((1,), (1,)), ((), ())), preferred_element_type=_F32)


def _bmm(a, b, ca, cb):
    return lax.dot_general(a, b, (((ca,), (cb,)), ((0,), (0,))), preferred_element_type=_F32)


def _mem_kv_kernel(mem_ref, g_ref, wk_ref, wv_ref, k_ref, v_ref):
    mn = _rms(mem_ref[0], g_ref[...]).astype(_BF16)
    k_ref[0] = _dot(mn, wk_ref[...])
    v_ref[0] = _dot(mn, wv_ref[...])


def _mem_kv(mem, g_mem, w_mk, w_mv):
    B = mem.shape[0]
    full = lambda b: (0, 0)
    return pl.pallas_call(
        _mem_kv_kernel,
        out_shape=(jax.ShapeDtypeStruct((B, N_MEM, D_MODEL), _F32),) * 2,
        grid=(B,),
        in_specs=[
            pl.BlockSpec((1, N_MEM, D_MODEL), lambda b: (b, 0, 0)),
            pl.BlockSpec((1, D_MODEL), full),
            pl.BlockSpec((D_MODEL, D_MODEL), full),
            pl.BlockSpec((D_MODEL, D_MODEL), full),
        ],
        out_specs=(pl.BlockSpec((1, N_MEM, D_MODEL), lambda b: (b, 0, 0)),) * 2,
        compiler_params=pltpu.CompilerParams(
            dimension_semantics=("arbitrary",), vmem_limit_bytes=_VMEM_LIMIT),
        name="mem_kv",
    )(mem, g_mem.reshape(1, D_MODEL), w_mk.astype(_BF16), w_mv.astype(_BF16))


def _mixer_kernel(x_ref, mk_ref, mv_ref, c0_ref, n0_ref, m0_ref, gmix_ref, win_ref, bin_ref,
                  wif_ref, wift_ref, bif_ref, bift_ref, gmlh_ref, gsgu_ref, ws_ref, bst_ref,
                  wout_ref,
                  x1_ref, c_out_ref, n_out_ref, m_out_ref, vn_ref,
                  pa_scr, pb_scr, pc_scr, pd_scr, mrg_scr, c_scr, n_scr, m_scr, *, tc, sgu_len):
    ci = pl.program_id(1)
    nsub = tc // CHUNK

    @pl.when(ci == 0)
    def _():
        c_scr[...] = c0_ref[0]
        n_scr[...] = n0_ref[0]
        m_scr[...] = m0_ref[0]

    x = x_ref[0]
    xn = _rms(x, gmix_ref[...]).astype(_BF16)

    def project(seg, out_scr):
        cols = slice(_SEG0[seg], _SEG0[seg] + out_scr.shape[1])
        out_scr[...] = _dot(xn, win_ref[:, cols]) + bin_ref[:, cols]

    project(0, pa_scr)
    gif = _dot(xn, wif_ref[...]) + bif_ref[...]
    gift = _dot_nt(wift_ref[...], xn) + bift_ref[...]

    row = lax.broadcasted_iota(jnp.int32, (CHUNK, CHUNK), 0)
    col = lax.broadcasted_iota(jnp.int32, (CHUNK, CHUNK), 1)
    tril = row >= col
    tril_f = tril.astype(_F32)
    triu_f = (row <= col).astype(_F32)
    stack = lambda f: jnp.stack([f(h) for h in range(ML_HEADS)])
    for j in range(nsub):
        r0 = j * CHUNK
        rows = slice(r0, r0 + CHUNK)
        ig_c = gif[rows, 0:ML_HEADS]
        lf_c = jax.nn.log_sigmoid(gif[rows, ML_HEADS:2 * ML_HEADS])
        ig_r = gift[0:ML_HEADS, rows]
        lf_r = jax.nn.log_sigmoid(gift[ML_HEADS:2 * ML_HEADS, rows])
        b_c = jnp.dot(tril_f, lf_c, precision=lax.Precision.HIGHEST,
                      preferred_element_type=_F32)
        b_r = jnp.dot(lf_r, triu_f, precision=lax.Precision.HIGHEST,
                      preferred_element_type=_F32)
        q = stack(lambda h: pa_scr[rows, h * ML_DQK:(h + 1) * ML_DQK]).astype(_BF16)
        k = stack(lambda h: pa_scr[rows, _K0 + h * ML_DQK:_K0 + (h + 1) * ML_DQK]) * (ML_DQK ** -0.5)
        v = stack(lambda h: pa_scr[rows, _V0 + h * ML_DV:_V0 + (h + 1) * ML_DV]).astype(_BF16)
        c_st = c_scr[...]
        n_st = stack(lambda h: n_scr[h:h + 1, :])
        m_st = stack(lambda h: m_scr[h:h + 1, 0:1])
        bc = stack(lambda h: b_c[:, h:h + 1])
        br = stack(lambda h: b_r[h:h + 1, :])
        ig_row = stack(lambda h: ig_r[h:h + 1, :])
        ig_col = stack(lambda h: ig_c[:, h:h + 1])
        logd = jnp.where(tril[None], bc - br + ig_row, _NEG_INF)
        log_prev = bc + m_st
        m_t = jnp.maximum(log_prev, jnp.max(logd, axis=2, keepdims=True))
        a = _bmm(q, k.astype(_BF16), 2, 2) * jnp.exp(logd - m_t)
        wp = jnp.exp(log_prev - m_t)
        num = _bmm(a.astype(_BF16), v, 2, 1) + wp * _bmm(q, c_st.astype(_BF16), 2, 1)
        qn = jnp.sum(q.astype(_F32) * n_st, axis=2, keepdims=True)
        den = jnp.sum(a, axis=2, keepdims=True) + wp * qn
        hh = num / jnp.maximum(jnp.abs(den), jnp.exp(-m_t))
        hn = hh * lax.rsqrt(jnp.mean(hh * hh, axis=2, keepdims=True) + EPS)
        for h in range(ML_HEADS):
            mrg_scr[rows, h * ML_DV:(h + 1) * ML_DV] = hn[h]
        b_last = bc[:, CHUNK - 1:CHUNK, :]
        log_in = b_last - bc + ig_col
        m_new = jnp.maximum(b_last + m_st, jnp.max(log_in, axis=1, keepdims=True))
        wi = jnp.exp(log_in - m_new)
        wc = jnp.exp(b_last + m_st - m_new)
        kw = wi * k
        c_scr[...] = wc * c_st + _bmm(kw.astype(_BF16), v, 1, 1)
        n_new = wc * n_st + jnp.sum(kw, axis=1, keepdims=True)
        for h in range(ML_HEADS):
            n_scr[h:h + 1, :] = n_new[h]
            m_scr[h:h + 1, :] = jnp.broadcast_to(m_new[h], (1, 128))

    project(1, pb_scr)
    h_a = jax.nn.sigmoid(pb_scr[:, 0:D_MODEL]) * mrg_scr[...] * gmlh_ref[...]
    mrg_scr[...] = jax.nn.sigmoid(pb_scr[:, D_MODEL:2 * D_MODEL]) * h_a

    project(2, pc_scr)
    sv = jax.nn.gelu(pc_scr[:, D_MODEL:2 * D_MODEL])
    vn = _rms(sv, gsgu_ref[...])
    if vn_ref is not None:
        vn_ref[0] = vn
    vnb = vn.astype(_BF16)
    rs = lax.broadcasted_iota(jnp.int32, (sgu_len, sgu_len), 0)
    cs = lax.broadcasted_iota(jnp.int32, (sgu_len, sgu_len), 1)
    for g in range(SGU_GROUPS):
        wsg = jnp.where(rs >= cs, ws_ref[g, 0:sgu_len, 0:sgu_len], 0.0).astype(_BF16)
        bsg = bst_ref[0:sgu_len, g:g + 1]
        gl = slice(g * SGU_GDIM, (g + 1) * SGU_GDIM)
        for c in range(tc // sgu_len):
            rows = slice(c * sgu_len, (c + 1) * sgu_len)
            mix = _dot(wsg, vnb[rows, gl]) + bsg
            u = jax.nn.gelu(pc_scr[rows, g * SGU_GDIM:(g + 1) * SGU_GDIM])
            g1 = jax.nn.sigmoid(pc_scr[rows, 2 * D_MODEL + g * SGU_GDIM:2 * D_MODEL + (g + 1) * SGU_GDIM])
            mrg_scr[rows, gl] += g1 * (u * mix)

    project(3, pd_scr)
    for h in range(XA_HEADS):
        hl = slice(h * XA_DH, (h + 1) * XA_DH)
        xq = pd_scr[:, h * XA_DH:(h + 1) * XA_DH].astype(_BF16)
        sc = _dot_nt(xq, mk_ref[0, :, hl].astype(_BF16)) * (XA_DH ** -0.5)
        sc = sc - jnp.max(sc, axis=1, keepdims=True)
        e = jnp.exp(sc)
        att = e / jnp.sum(e, axis=1, keepdims=True)
        h_c = _dot(att.astype(_BF16), mv_ref[0, :, hl].astype(_BF16))
        g2 = jax.nn.sigmoid(pd_scr[:, D_MODEL + h * XA_DH:D_MODEL + (h + 1) * XA_DH])
        mrg_scr[:, hl] += g2 * h_c

    x1_ref[0] = x + _dot(mrg_scr[...].astype(_BF16), wout_ref[...])

    @pl.when(ci == pl.num_programs(1) - 1)
    def _():
        c_out_ref[0] = c_scr[...]
        n_out_ref[0] = n_scr[...]
        m_out_ref[0] = m_scr[...]


def _mixer(x, mem_k, mem_v, c0, n0, m0, wts, *, tc, want_vn):
    B, S, D = x.shape
    sgu_len = min(S, SGU_CHUNK)
    assert S % tc == 0 and tc % CHUNK == 0 and tc % sgu_len == 0
    nchunks = S // tc
    m0p = jnp.broadcast_to(m0[:, :, None], (B, ML_HEADS, 128))
    m0p = jnp.concatenate([m0p, jnp.zeros((B, 8 - ML_HEADS, 128), _F32)], axis=1)

    def body(*refs):
        ins, rest = refs[:18], refs[18:]
        if want_vn:
            outs, scr = rest[:5], rest[5:]
        else:
            outs, scr = rest[:4] + (None,), rest[4:]
        _mixer_kernel(*ins, *outs, *scr, tc=tc, sgu_len=sgu_len)

    const2 = lambda b, c: (0, 0)
    const3 = lambda b, c: (0, 0, 0)
    per_b3 = lambda b, c: (b, 0, 0)
    per_b4 = lambda b, c: (b, 0, 0, 0)
    once = dict(pipeline_mode=pl.Buffered(1))
    in_specs = [
        pl.BlockSpec((1, tc, D), lambda b, c: (b, c, 0)),
        pl.BlockSpec((1, N_MEM, D), per_b3),
        pl.BlockSpec((1, N_MEM, D), per_b3),
        pl.BlockSpec((1, ML_HEADS, ML_DQK, ML_DV), per_b4),
        pl.BlockSpec((1, ML_HEADS, ML_DQK), per_b3),
        pl.BlockSpec((1, 8, 128), per_b3),
        pl.BlockSpec((1, D), const2, **once),
        pl.BlockSpec((D, _P_COLS), const2, **once),
        pl.BlockSpec((1, _P_COLS), const2, **once),
        pl.BlockSpec((D, 8), const2, **once),
        pl.BlockSpec((8, D), const2, **once),
        pl.BlockSpec((1, 8), const2, **once),
        pl.BlockSpec((8, 1), const2, **once),
        pl.BlockSpec((1, D), const2, **once),
        pl.BlockSpec((1, D), const2, **once),
        pl.BlockSpec((SGU_GROUPS, SGU_CHUNK, SGU_CHUNK), const3, **once),
        pl.BlockSpec((SGU_CHUNK, SGU_GROUPS), const2, **once),
        pl.BlockSpec((D, D), const2, **once),
    ]
    out_shape = [
        jax.ShapeDtypeStruct((B, S, D), _F32),
        jax.ShapeDtypeStruct((B, ML_HEADS, ML_DQK, ML_DV), _F32),
        jax.ShapeDtypeStruct((B, ML_HEADS, ML_DQK), _F32),
        jax.ShapeDtypeStruct((B, 8, 128), _F32),
    ]
    out_specs = [
        pl.BlockSpec((1, tc, D), lambda b, c: (b, c, 0)),
        pl.BlockSpec((1, ML_HEADS, ML_DQK, ML_DV), per_b4),
        pl.BlockSpec((1, ML_HEADS, ML_DQK), per_b3),
        pl.BlockSpec((1, 8, 128), per_b3),
    ]
    if want_vn:
        out_shape.append(jax.ShapeDtypeStruct((B, S, D), _F32))
        out_specs.append(pl.BlockSpec((1, tc, D), lambda b, c: (b, c, 0)))
    outs = pl.pallas_call(
        body,
        out_shape=tuple(out_shape),
        grid=(B, nchunks),
        in_specs=in_specs,
        out_specs=tuple(out_specs),
        scratch_shapes=[
            pltpu.VMEM((tc, _SEG_A), _F32),
            pltpu.VMEM((tc, _SEG_B), _F32),
            pltpu.VMEM((tc, _SEG_C), _F32),
            pltpu.VMEM((tc, _SEG_D), _F32),
            pltpu.VMEM((tc, D), _F32),
            pltpu.VMEM((ML_HEADS, ML_DQK, ML_DV), _F32),
            pltpu.VMEM((ML_HEADS, ML_DQK), _F32),
            pltpu.VMEM((8, 128), _F32),
        ],
        compiler_params=pltpu.CompilerParams(
            dimension_semantics=("arbitrary", "arbitrary"), vmem_limit_bytes=_VMEM_LIMIT),
        name="mixer",
    )(x, mem_k, mem_v, c0, n0, m0p, *wts)
    x1, c1, n1, m1p = outs[:4]
    vn = outs[4] if want_vn else None
    return x1, c1, n1, m1p[:, :ML_HEADS, 0], vn


def _batcher_pairs(n):
    pairs = []
    p = 1
    while p < n:
        k = p
        while k >= 1:
            for j in range(k % p, n - k, 2 * k):
                for i in range(min(k, n - j - k)):
                    if (i + j) // (2 * p) == (i + j + k) // (2 * p):
                        pairs.append((i + j, i + j + k))
            k //= 2
        p *= 2
    return pairs


_SORT16 = _batcher_pairs(PEER_NKEYS // 8)
_SEL_UNROLL = 2


def _sorted_columns(c):
    cols = [c[8 * k:8 * k + 8, :] for k in range(PEER_NKEYS // 8)]
    for i, j in _SORT16:
        cols[i], cols[j] = jnp.maximum(cols[i], cols[j]), jnp.minimum(cols[i], cols[j])
    return cols


def _pop_top(cols, out_scr):
    for r in range(PEER_TOPK):
        mx = jnp.max(cols[0], axis=0, keepdims=True)
        out_scr[r:r + 1, :] = mx
        left = PEER_TOPK - 1 - r
        if left:
            eq = cols[0] == mx
            for k in range(left):
                cols[k] = jnp.where(eq, cols[k + 1], cols[k])


def _bf16_pair(x, exact=False):
    u = pltpu.bitcast(x, jnp.uint32)
    if not exact:
        u = u + jnp.uint32(0x7FFF) + ((u >> 16) & jnp.uint32(1))
    hi = u & jnp.uint32(0xFFFF0000)
    return hi | (hi >> 16)


def _peer_select_kernel(x1_ref, gffn_ref, wpq_ref, k1_ref, k2_ref,
                        xn_ref, cnt_ref, p1n_ref, rank_ref, p2_ref,
                        q_scr, c1_scr, c2_scr, v1_scr, v2_scr, n_scr, *, tb):
    xn = _rms(x1_ref[...], gffn_ref[...]).astype(_BF16)
    xn_ref[...] = xn
    q_scr[...] = _dot(xn, wpq_ref[...])
    k1 = k1_ref[...].astype(_BF16)
    k2 = k2_ref[...].astype(_BF16)
    half = PEER_DQ // 2
    K = PEER_TOPK

    def lane_group(h, ls):
        c1 = c1_scr[:, ls]
        c2 = c2_scr[:, ls]
        _pop_top(_sorted_columns(c1), v1_scr)
        _pop_top(_sorted_columns(c2), v2_scr)
        v1 = v1_scr[...]
        lists = [v1_scr[0:8, :] + v2_scr[b:b + 1, :] for b in range(8)]
        ea = v1_scr[8:16, :] + v2_scr[0:1, :]
        eb = v1_scr[0:1, :] + v2_scr[8:16, :]
        cands = lists + [ea, eb]
        tau = None
        for r in range(K):
            tau = jnp.max(jnp.maximum(jnp.maximum(lists[0], ea), eb), axis=0, keepdims=True)
            left = K - 1 - r
            if left:
                eq = lists[0] == tau
                for k in range(min(left, 7)):
                    lists[k] = jnp.where(eq, lists[k + 1], lists[k])
                if left >= 8:
                    lists[7] = jnp.where(eq, _NEG_INF, lists[7])
                ea = jnp.where(ea == tau, _NEG_INF, ea)
                eb = jnp.where(eb == tau, _NEG_INF, eb)
        z = sum(jnp.sum(jnp.where(c >= tau, jnp.exp(c), 0.0), axis=0, keepdims=True) for c in cands)
        n = jnp.zeros(v1.shape, _F32)
        for b in range(K):
            n = jnp.where(v1 + v2_scr[b:b + 1, :] >= tau, float(b + 1), n)
        n_scr[...] = n
        cnt = jnp.zeros(c1.shape, _F32)
        rank2 = jnp.full(c2.shape, float(K), _F32)
        for a in range(K - 1, -1, -1):
            cnt = jnp.where(c1 >= v1_scr[a:a + 1, :], n_scr[a:a + 1, :], cnt)
            rank2 = jnp.where(c2 >= v2_scr[a:a + 1, :], float(a), rank2)
        cnt_ref[h, :, ls] = _bf16_pair(cnt, exact=True)
        p1n_ref[h, :, ls] = _bf16_pair(jnp.exp(c1) / z)
        rank_ref[h, :, ls] = rank2.astype(_BF16)
        p2_ref[h, :, ls] = jnp.exp(c2).astype(_BF16)

    def head(h, carry):
        q0 = pl.multiple_of(h * PEER_DQ, PEER_DQ)
        s1 = _dot_nt(k1, q_scr[:, pl.ds(q0, half)].astype(_BF16))
        s2 = _dot_nt(k2, q_scr[:, pl.ds(q0 + half, half)].astype(_BF16))
        c1_scr[...] = s1 - jnp.max(s1, axis=0, keepdims=True)
        c2_scr[...] = s2 - jnp.max(s2, axis=0, keepdims=True)

        def lane_groups(i, carry):
            for u in range(_SEL_UNROLL):
                lane_group(h, pl.ds(pl.multiple_of((i * _SEL_UNROLL + u) * 128, 128), 128))
            return carry

        lax.fori_loop(0, tb // (128 * _SEL_UNROLL), lane_groups, 0)
        return carry

    lax.fori_loop(0, PEER_HEADS, head, 0)


def _peer_select(x1, g_ffn, wpq_bf, k_sub1, k_sub2, *, tb):
    T, D = x1.shape
    assert T % tb == 0 and tb % (128 * _SEL_UNROLL) == 0
    sel = lambda dt: jax.ShapeDtypeStruct((PEER_HEADS, PEER_NKEYS, T), dt)
    sel_spec = pl.BlockSpec((PEER_HEADS, PEER_NKEYS, tb), lambda i: (0, 0, i))
    const2 = lambda i: (0, 0)
    return pl.pallas_call(
        functools.partial(_peer_select_kernel, tb=tb),
        out_shape=(jax.ShapeDtypeStruct((T, D), _BF16), sel(jnp.uint32), sel(jnp.uint32), sel(_BF16), sel(_BF16)),
        grid=(T // tb,),
        in_specs=[
            pl.BlockSpec((tb, D), lambda i: (i, 0)),
            pl.BlockSpec((1, D), const2),
            pl.BlockSpec((D, PEER_HEADS * PEER_DQ), const2),
            pl.BlockSpec((PEER_NKEYS, PEER_DQ // 2), const2),
            pl.BlockSpec((PEER_NKEYS, PEER_DQ // 2), const2),
        ],
        out_specs=(pl.BlockSpec((tb, D), lambda i: (i, 0)),) + (sel_spec,) * 4,
        scratch_shapes=[
            pltpu.VMEM((tb, PEER_HEADS * PEER_DQ), _F32),
            pltpu.VMEM((PEER_NKEYS, tb), _F32),
            pltpu.VMEM((PEER_NKEYS, tb), _F32),
            pltpu.VMEM((PEER_TOPK, 128), _F32),
            pltpu.VMEM((PEER_TOPK, 128), _F32),
            pltpu.VMEM((PEER_TOPK, 128), _F32),
        ],
        compiler_params=pltpu.CompilerParams(
            dimension_semantics=("arbitrary",), vmem_limit_bytes=_VMEM_LIMIT),
        name="peer_select",
    )(x1, g_ffn.reshape(1, D), wpq_bf, k_sub1, k_sub2)


def _peer_dense_kernel(xn_ref, u_ref, vtp_ref, vtl_ref, cnt_ref, p1n_ref, rank_ref, p2_ref, x1_ref,
                       gfin_ref, y_ref, acc_scr, st_scr, h_scr, *, tb, rows):
    j = pl.program_id(1)
    cur = lax.rem(j, 2)

    @pl.when(j == 0)
    def _():
        acc_scr[...] = jnp.zeros_like(acc_scr)
        h_scr[1] = jnp.zeros(h_scr.shape[1:], _BF16)

    tile = (PEER_NKEYS, 128)
    words = (PEER_NKEYS // 2, 128)
    rb = 1
    for b in range(rows // rb):
        bs = slice(b * rb * PEER_NKEYS, (b + 1) * rb * PEER_NKEYS)
        st_scr[bs, :] = _dot_nt(u_ref[bs, :], xn_ref[...])
        for r in range(b * rb, (b + 1) * rb):
            es = slice(r * PEER_NKEYS, (r + 1) * PEER_NKEYS)
            for lg in range(tb // 128):
                ls = slice(lg * 128, (lg + 1) * 128)
                gate = jnp.zeros(tile, _BF16)
                for h in range(PEER_HEADS):
                    cnt = pltpu.bitcast(jnp.broadcast_to(cnt_ref[h, r:r + 1, ls], words), _BF16)
                    pn = pltpu.bitcast(jnp.broadcast_to(p1n_ref[h, r:r + 1, ls], words), _BF16)
                    sel = jnp.minimum(jnp.maximum(cnt - rank_ref[h, :, ls], 0), pn)
                    gate = gate + sel * p2_ref[h, :, ls]
                act = jax.nn.gelu(st_scr[es, ls].astype(_BF16))
                h_scr[cur, es, ls] = gate * act
    acc_scr[...] += _dot(vtp_ref[0], h_scr[1 - cur])

    @pl.when(j == pl.num_programs(1) - 1)
    def _():
        acc = acc_scr[...] + _dot(vtl_ref[0], h_scr[cur])
        x2 = x1_ref[...] + acc.T
        y_ref[...] = _rms(x2, gfin_ref[...])


def _peer_dense(xn, u_bf, vt_bf, cnt, p1n, rank2, p2, x1, g_final, *, tb, ec):
    T, D = x1.shape
    rows = ec // PEER_NKEYS
    nchunks = PEER_NEXP // ec
    assert T % tb == 0 and PEER_NEXP % ec == 0 and rows % 8 == 0
    sel_all = pl.BlockSpec((PEER_HEADS, PEER_NKEYS, tb), lambda i, j: (0, 0, i))
    sel_rows = pl.BlockSpec((PEER_HEADS, rows, tb), lambda i, j: (0, j, i))
    tok = pl.BlockSpec((tb, D), lambda i, j: (i, 0))
    return pl.pallas_call(
        functools.partial(_peer_dense_kernel, tb=tb, rows=rows),
        out_shape=jax.ShapeDtypeStruct((T, D), _F32),
        grid=(T // tb, nchunks),
        in_specs=[
            tok,
            pl.BlockSpec((ec, D), lambda i, j: (j, 0)),
            pl.BlockSpec((1, D, ec), lambda i, j: (jnp.maximum(j - 1, 0), 0, 0)),
            pl.BlockSpec((1, D, ec), lambda i, j: (nchunks - 1, 0, 0), pipeline_mode=pl.Buffered(1)),
            sel_rows, sel_rows, sel_all, sel_all,
            tok,
            pl.BlockSpec((1, D), lambda i, j: (0, 0)),
        ],
        out_specs=tok,
        scratch_shapes=[
            pltpu.VMEM((D, tb), _F32),
            pltpu.VMEM((ec, tb), _F32),
            pltpu.VMEM((2, ec, tb), _BF16),
        ],
        compiler_params=pltpu.CompilerParams(
            dimension_semantics=("arbitrary", "arbitrary"), vmem_limit_bytes=_VMEM_LIMIT),
        name="peer_dense",
    )(xn, u_bf, vt_bf, vt_bf, cnt, p1n, rank2, p2, x1, g_final.reshape(1, D))


def _mixer_weights(g_mix, w_in, b_in, g_mlh, g_sgu, w_s, b_s, w_out):
    nq = ML_HEADS * ML_DQK
    nv = ML_HEADS * ML_DV
    o_q, o_k, o_v = 0, nq, 2 * nq
    o_ig = o_v + nv
    o_fg = o_ig + ML_HEADS
    o_og = o_fg + ML_HEADS
    o_su = o_og + nv
    o_sv = o_su + D_MODEL
    o_xq = o_sv + D_MODEL
    o_gt = o_xq + D_MODEL
    gate = lambda b: slice(o_gt + b * D_MODEL, o_gt + (b + 1) * D_MODEL)
    main = lambda a: jnp.concatenate(
        [a[..., o_q:o_ig],
         a[..., o_og:o_su], a[..., gate(0)],
         a[..., o_su:o_xq], a[..., gate(1)],
         a[..., o_xq:o_gt], a[..., gate(2)]], axis=-1)
    w_if = w_in[:, o_ig:o_og]
    b_if = b_in[o_ig:o_og]
    return (
        g_mix.reshape(1, D_MODEL),
        main(w_in).astype(_BF16),
        main(b_in).reshape(1, _P_COLS),
        w_if.astype(_BF16),
        w_if.T.astype(_BF16),
        b_if.reshape(1, 8),
        b_if.reshape(8, 1),
        g_mlh.reshape(1, D_MODEL),
        g_sgu.reshape(1, D_MODEL),
        w_s,
        b_s.T,
        w_out.astype(_BF16),
    )


def kernel(x_prompt, x_sample, mem_prompt, cache_mem_k, cache_mem_v, state_mlstm_C, state_mlstm_n,
           state_mlstm_m, g_mix, w_in, b_in, g_mlh, g_sgu, w_s, b_s, g_mem, w_mk, w_mv, w_out, g_ffn,
           w_pq, k_sub1, k_sub2, u_exp, v_exp, g_final):
    depth = g_mix.shape[0]
    assert depth == 1
    l = 0
    Bp, Sp, D = x_prompt.shape
    Bs, Ss, _ = x_sample.shape

    wts = _mixer_weights(g_mix[l], w_in[l], b_in[l], g_mlh[l], g_sgu[l], w_s[l], b_s[l], w_out[l])

    mk, mv = _mem_kv(mem_prompt, g_mem[l], w_mk[l], w_mv[l])
    zc = jnp.zeros((Bp, ML_HEADS, ML_DQK, ML_DV), _F32)
    zn = jnp.zeros((Bp, ML_HEADS, ML_DQK), _F32)
    zm = jnp.zeros((Bp, ML_HEADS), _F32)
    x1p, cp, np_, mp, _ = _mixer(x_prompt, mk, mv, zc, zn, zm, wts,
                                 tc=min(_MIXER_TC, Sp), want_vn=False)
    x1s, cs, ns, ms, vn = _mixer(
        x_sample, cache_mem_k[l].reshape(Bs, N_MEM, D), cache_mem_v[l].reshape(Bs, N_MEM, D),
        state_mlstm_C[l], state_mlstm_n[l], state_mlstm_m[l], wts,
        tc=min(_MIXER_TC, Ss), want_vn=True)

    wpq_bf = w_pq[l].astype(_BF16)
    u_bf = u_exp[l].astype(_BF16)
    vt_bf = v_exp[l].astype(_BF16).reshape(PEER_NEXP // _PEER_EC, _PEER_EC, D).transpose(0, 2, 1)

    def peer(x1):
        B, S, _ = x1.shape
        x1 = x1.reshape(B * S, D)
        xn, cnt, p1n, rank2, p2 = _peer_select(x1, g_ffn[l], wpq_bf, k_sub1[l], k_sub2[l], tb=_PEER_TB)
        y = _peer_dense(xn, u_bf, vt_bf, cnt, p1n, rank2, p2, x1, g_final, tb=_PEER_TB, ec=_PEER_EC)
        return y.reshape(B, S, D)

    hs = (XA_HEADS, XA_DH)
    return (peer(x1p), peer(x1s), cp[None], np_[None], mp[None],
            mk.reshape(1, Bp, N_MEM, *hs), mv.reshape(1, Bp, N_MEM, *hs),
            cs[None], ns[None], ms[None], vn[None])
```

```python
import functools

import jax
import jax.numpy as jnp
from jax import lax
from jax.experimental import pallas as pl
from jax.experimental.pallas import tpu as pltpu

D_MODEL = 1024
EPS = 1e-6
CHUNK = 64
N_MEM = 256
ML_HEADS = 4
ML_DQK = 128
ML_DV = D_MODEL // ML_HEADS
SGU_CHUNK = 128
SGU_GROUPS = 4
SGU_GDIM = D_MODEL // SGU_GROUPS
XA_HEADS = 4
XA_DH = D_MODEL // XA_HEADS
PEER_HEADS = 8
PEER_NKEYS = 128
PEER_DQ = 256
PEER_TOPK = 16
PEER_NEXP = PEER_NKEYS * PEER_NKEYS

_SEG_A = ML_HEADS * (2 * ML_DQK + ML_DV)
_SEG_B = 2 * D_MODEL
_SEG_C = 3 * D_MODEL
_SEG_D = 2 * D_MODEL
_SEG0 = (0, _SEG_A, _SEG_A + _SEG_B, _SEG_A + _SEG_B + _SEG_C)
_P_COLS = _SEG_A + _SEG_B + _SEG_C + _SEG_D
_K0 = ML_HEADS * ML_DQK
_V0 = 2 * ML_HEADS * ML_DQK

_VMEM_LIMIT = 56 * 1024 * 1024

_MIXER_TC = 256
_PEER_TB = 512
_PEER_EC = 2048

_BF16 = jnp.bfloat16
_F32 = jnp.float32
_NEG_INF = float("-inf")


def _rms(xf, g):
    return xf * lax.rsqrt(jnp.mean(xf * xf, axis=-1, keepdims=True) + EPS) * g


def _dot(a, b):
    return jnp.dot(a, b, preferred_element_type=_F32)


def _dot_nt(a, b):
    return lax.dot_general(a, b, (((1,), (1,)), ((), ())), preferred_element_type=_F32)


def _bmm(a, b, ca, cb):
    return lax.dot_general(a, b, (((ca,), (cb,)), ((0,), (0,))), preferred_element_type=_F32)


def _mem_kv_kernel(mem_ref, g_ref, wk_ref, wv_ref, k_ref, v_ref):
    mn = _rms(mem_ref[0], g_ref[...]).astype(_BF16)
    k_ref[0] = _dot(mn, wk_ref[...])
    v_ref[0] = _dot(mn, wv_ref[...])


def _mem_kv(mem, g_mem, w_mk, w_mv):
    B = mem.shape[0]
    full = lambda b: (0, 0)
    return pl.pallas_call(
        _mem_kv_kernel,
        out_shape=(jax.ShapeDtypeStruct((B, N_MEM, D_MODEL), _F32),) * 2,
        grid=(B,),
        in_specs=[
            pl.BlockSpec((1, N_MEM, D_MODEL), lambda b: (b, 0, 0)),
            pl.BlockSpec((1, D_MODEL), full),
            pl.BlockSpec((D_MODEL, D_MODEL), full),
            pl.BlockSpec((D_MODEL, D_MODEL), full),
        ],
        out_specs=(pl.BlockSpec((1, N_MEM, D_MODEL), lambda b: (b, 0, 0)),) * 2,
        compiler_params=pltpu.CompilerParams(
            dimension_semantics=("arbitrary",), vmem_limit_bytes=_VMEM_LIMIT),
        name="mem_kv",
    )(mem, g_mem.reshape(1, D_MODEL), w_mk.astype(_BF16), w_mv.astype(_BF16))


def _mixer_kernel(x_ref, mk_ref, mv_ref, c0_ref, n0_ref, m0_ref, gmix_ref, win_ref, bin_ref,
                  wif_ref, wift_ref, bif_ref, bift_ref, gmlh_ref, gsgu_ref, ws_ref, bst_ref,
                  wout_ref,
                  x1_ref, c_out_ref, n_out_ref, m_out_ref, vn_ref,
                  pa_scr, pb_scr, pc_scr, pd_scr, mrg_scr, c_scr, n_scr, m_scr, *, tc, sgu_len):
    ci = pl.program_id(1)
    nsub = tc // CHUNK

    @pl.when(ci == 0)
    def _():
        c_scr[...] = c0_ref[0]
        n_scr[...] = n0_ref[0]
        m_scr[...] = m0_ref[0]

    x = x_ref[0]
    xn = _rms(x, gmix_ref[...]).astype(_BF16)

    def project(seg, out_scr):
        cols = slice(_SEG0[seg], _SEG0[seg] + out_scr.shape[1])
        out_scr[...] = _dot(xn, win_ref[:, cols]) + bin_ref[:, cols]

    project(0, pa_scr)
    gif = _dot(xn, wif_ref[...]) + bif_ref[...]
    gift = _dot_nt(wift_ref[...], xn) + bift_ref[...]

    row = lax.broadcasted_iota(jnp.int32, (CHUNK, CHUNK), 0)
    col = lax.broadcasted_iota(jnp.int32, (CHUNK, CHUNK), 1)
    tril = row >= col
    tril_f = tril.astype(_F32)
    triu_f = (row <= col).astype(_F32)
    stack = lambda f: jnp.stack([f(h) for h in range(ML_HEADS)])
    for j in range(nsub):
        r0 = j * CHUNK
        rows = slice(r0, r0 + CHUNK)
        ig_c = gif[rows, 0:ML_HEADS]
        lf_c = jax.nn.log_sigmoid(gif[rows, ML_HEADS:2 * ML_HEADS])
        ig_r = gift[0:ML_HEADS, rows]
        lf_r = jax.nn.log_sigmoid(gift[ML_HEADS:2 * ML_HEADS, rows])
        b_c = jnp.dot(tril_f, lf_c, precision=lax.Precision.HIGHEST,
                      preferred_element_type=_F32)
        b_r = jnp.dot(lf_r, triu_f, precision=lax.Precision.HIGHEST,
                      preferred_element_type=_F32)
        q = stack(lambda h: pa_scr[rows, h * ML_DQK:(h + 1) * ML_DQK]).astype(_BF16)
        k = stack(lambda h: pa_scr[rows, _K0 + h * ML_DQK:_K0 + (h + 1) * ML_DQK]) * (ML_DQK ** -0.5)
        v = stack(lambda h: pa_scr[rows, _V0 + h * ML_DV:_V0 + (h + 1) * ML_DV]).astype(_BF16)
        c_st = c_scr[...]
        n_st = stack(lambda h: n_scr[h:h + 1, :])
        m_st = stack(lambda h: m_scr[h:h + 1, 0:1])
        bc = stack(lambda h: b_c[:, h:h + 1])
        br = stack(lambda h: b_r[h:h + 1, :])
        ig_row = stack(lambda h: ig_r[h:h + 1, :])
        ig_col = stack(lambda h: ig_c[:, h:h + 1])
        logd = jnp.where(tril[None], bc - br + ig_row, _NEG_INF)
        log_prev = bc + m_st
        m_t = jnp.maximum(log_prev, jnp.max(logd, axis=2, keepdims=True))
        a = _bmm(q, k.astype(_BF16), 2, 2) * jnp.exp(logd - m_t)
        wp = jnp.exp(log_prev - m_t)
        num = _bmm(a.astype(_BF16), v, 2, 1) + wp * _bmm(q, c_st.astype(_BF16), 2, 1)
        qn = jnp.sum(q.astype(_F32) * n_st, axis=2, keepdims=True)
        den = jnp.sum(a, axis=2, keepdims=True) + wp * qn
        hh = num / jnp.maximum(jnp.abs(den), jnp.exp(-m_t))
        hn = hh * lax.rsqrt(jnp.mean(hh * hh, axis=2, keepdims=True) + EPS)
        for h in range(ML_HEADS):
            mrg_scr[rows, h * ML_DV:(h + 1) * ML_DV] = hn[h]
        b_last = bc[:, CHUNK - 1:CHUNK, :]
        log_in = b_last - bc + ig_col
        m_new = jnp.maximum(b_last + m_st, jnp.max(log_in, axis=1, keepdims=True))
        wi = jnp.exp(log_in - m_new)
        wc = jnp.exp(b_last + m_st - m_new)
        kw = wi * k
        c_scr[...] = wc * c_st + _bmm(kw.astype(_BF16), v, 1, 1)
        n_new = wc * n_st + jnp.sum(kw, axis=1, keepdims=True)
        for h in range(ML_HEADS):
            n_scr[h:h + 1, :] = n_new[h]
            m_scr[h:h + 1, :] = jnp.broadcast_to(m_new[h], (1, 128))

    project(1, pb_scr)
    h_a = jax.nn.sigmoid(pb_scr[:, 0:D_MODEL]) * mrg_scr[...] * gmlh_ref[...]
    mrg_scr[...] = jax.nn.sigmoid(pb_scr[:, D_MODEL:2 * D_MODEL]) * h_a

    project(2, pc_scr)
    sv = jax.nn.gelu(pc_scr[:, D_MODEL:2 * D_MODEL])
    vn = _rms(sv, gsgu_ref[...])
    if vn_ref is not None:
        vn_ref[0] = vn
    vnb = vn.astype(_BF16)
    rs = lax.broadcasted_iota(jnp.int32, (sgu_len, sgu_len), 0)
    cs = lax.broadcasted_iota(jnp.int32, (sgu_len, sgu_len), 1)
    for g in range(SGU_GROUPS):
        wsg = jnp.where(rs >= cs, ws_ref[g, 0:sgu_len, 0:sgu_len], 0.0).astype(_BF16)
        bsg = bst_ref[0:sgu_len, g:g + 1]
        gl = slice(g * SGU_GDIM, (g + 1) * SGU_GDIM)
        for c in range(tc // sgu_len):
            rows = slice(c * sgu_len, (c + 1) * sgu_len)
            mix = _dot(wsg, vnb[rows, gl]) + bsg
            u = jax.nn.gelu(pc_scr[rows, g * SGU_GDIM:(g + 1) * SGU_GDIM])
            g1 = jax.nn.sigmoid(pc_scr[rows, 2 * D_MODEL + g * SGU_GDIM:2 * D_MODEL + (g + 1) * SGU_GDIM])
            mrg_scr[rows, gl] += g1 * (u * mix)

    project(3, pd_scr)
    for h in range(XA_HEADS):
        hl = slice(h * XA_DH, (h + 1) * XA_DH)
        xq = pd_scr[:, h * XA_DH:(h + 1) * XA_DH].astype(_BF16)
        sc = _dot_nt(xq, mk_ref[0, :, hl].astype(_BF16)) * (XA_DH ** -0.5)
        sc = sc - jnp.max(sc, axis=1, keepdims=True)
        e = jnp.exp(sc)
        att = e / jnp.sum(e, axis=1, keepdims=True)
        h_c = _dot(att.astype(_BF16), mv_ref[0, :, hl].astype(_BF16))
        g2 = jax.nn.sigmoid(pd_scr[:, D_MODEL + h * XA_DH:D_MODEL + (h + 1) * XA_DH])
        mrg_scr[:, hl] += g2 * h_c

    x1_ref[0] = x + _dot(mrg_scr[...].astype(_BF16), wout_ref[...])

    @pl.when(ci == pl.num_programs(1) - 1)
    def _():
        c_out_ref[0] = c_scr[...]
        n_out_ref[0] = n_scr[...]
        m_out_ref[0] = m_scr[...]


def _mixer(x, mem_k, mem_v, c0, n0, m0, wts, *, tc, want_vn):
    B, S, D = x.shape
    sgu_len = min(S, SGU_CHUNK)
    assert S % tc == 0 and tc % CHUNK == 0 and tc % sgu_len == 0
    nchunks = S // tc
    m0p = jnp.broadcast_to(m0[:, :, None], (B, ML_HEADS, 128))
    m0p = jnp.concatenate([m0p, jnp.zeros((B, 8 - ML_HEADS, 128), _F32)], axis=1)

    def body(*refs):
        ins, rest = refs[:18], refs[18:]
        if want_vn:
            outs, scr = rest[:5], rest[5:]
        else:
            outs, scr = rest[:4] + (None,), rest[4:]
        _mixer_kernel(*ins, *outs, *scr, tc=tc, sgu_len=sgu_len)

    const2 = lambda b, c: (0, 0)
    const3 = lambda b, c: (0, 0, 0)
    per_b3 = lambda b, c: (b, 0, 0)
    per_b4 = lambda b, c: (b, 0, 0, 0)
    once = dict(pipeline_mode=pl.Buffered(1))
    in_specs = [
        pl.BlockSpec((1, tc, D), lambda b, c: (b, c, 0)),
        pl.BlockSpec((1, N_MEM, D), per_b3),
        pl.BlockSpec((1, N_MEM, D), per_b3),
        pl.BlockSpec((1, ML_HEADS, ML_DQK, ML_DV), per_b4),
        pl.BlockSpec((1, ML_HEADS, ML_DQK), per_b3),
        pl.BlockSpec((1, 8, 128), per_b3),
        pl.BlockSpec((1, D), const2, **once),
        pl.BlockSpec((D, _P_COLS), const2, **once),
        pl.BlockSpec((1, _P_COLS), const2, **once),
        pl.BlockSpec((D, 8), const2, **once),
        pl.BlockSpec((8, D), const2, **once),
        pl.BlockSpec((1, 8), const2, **once),
        pl.BlockSpec((8, 1), const2, **once),
        pl.BlockSpec((1, D), const2, **once),
        pl.BlockSpec((1, D), const2, **once),
        pl.BlockSpec((SGU_GROUPS, SGU_CHUNK, SGU_CHUNK), const3, **once),
        pl.BlockSpec((SGU_CHUNK, SGU_GROUPS), const2, **once),
        pl.BlockSpec((D, D), const2, **once),
    ]
    out_shape = [
        jax.ShapeDtypeStruct((B, S, D), _F32),
        jax.ShapeDtypeStruct((B, ML_HEADS, ML_DQK, ML_DV), _F32),
        jax.ShapeDtypeStruct((B, ML_HEADS, ML_DQK), _F32),
        jax.ShapeDtypeStruct((B, 8, 128), _F32),
    ]
    out_specs = [
        pl.BlockSpec((1, tc, D), lambda b, c: (b, c, 0)),
        pl.BlockSpec((1, ML_HEADS, ML_DQK, ML_DV), per_b4),
        pl.BlockSpec((1, ML_HEADS, ML_DQK), per_b3),
        pl.BlockSpec((1, 8, 128), per_b3),
    ]
    if want_vn:
        out_shape.append(jax.ShapeDtypeStruct((B, S, D), _F32))
        out_specs.append(pl.BlockSpec((1, tc, D), lambda b, c: (b, c, 0)))
    outs = pl.pallas_call(
        body,
        out_shape=tuple(out_shape),
        grid=(B, nchunks),
        in_specs=in_specs,
        out_specs=tuple(out_specs),
        scratch_shapes=[
            pltpu.VMEM((tc, _SEG_A), _F32),
            pltpu.VMEM((tc, _SEG_B), _F32),
            pltpu.VMEM((tc, _SEG_C), _F32),
            pltpu.VMEM((tc, _SEG_D), _F32),
            pltpu.VMEM((tc, D), _F32),
            pltpu.VMEM((ML_HEADS, ML_DQK, ML_DV), _F32),
            pltpu.VMEM((ML_HEADS, ML_DQK), _F32),
            pltpu.VMEM((8, 128), _F32),
        ],
        compiler_params=pltpu.CompilerParams(
            dimension_semantics=("arbitrary", "arbitrary"), vmem_limit_bytes=_VMEM_LIMIT),
        name="mixer",
    )(x, mem_k, mem_v, c0, n0, m0p, *wts)
    x1, c1, n1, m1p = outs[:4]
    vn = outs[4] if want_vn else None
    return x1, c1, n1, m1p[:, :ML_HEADS, 0], vn


def _batcher_pairs(n):
    pairs = []
    p = 1
    while p < n:
        k = p
        while k >= 1:
            for j in range(k % p, n - k, 2 * k):
                for i in range(min(k, n - j - k)):
                    if (i + j) // (2 * p) == (i + j + k) // (2 * p):
                        pairs.append((i + j, i + j + k))
            k //= 2
        p *= 2
    return pairs


_SORT16 = _batcher_pairs(PEER_NKEYS // 8)
_SEL_UNROLL = 2


def _sorted_columns(c):
    cols = [c[8 * k:8 * k + 8, :] for k in range(PEER_NKEYS // 8)]
    for i, j in _SORT16:
        cols[i], cols[j] = jnp.maximum(cols[i], cols[j]), jnp.minimum(cols[i], cols[j])
    return cols


def _pop_top(cols, out_scr):
    for r in range(PEER_TOPK):
        mx = jnp.max(cols[0], axis=0, keepdims=True)
        out_scr[r:r + 1, :] = mx
        left = PEER_TOPK - 1 - r
        if left:
            eq = cols[0] == mx
            for k in range(left):
                cols[k] = jnp.where(eq, cols[k + 1], cols[k])


def _bf16_pair(x, exact=False):
    u = pltpu.bitcast(x, jnp.uint32)
    if not exact:
        u = u + jnp.uint32(0x7FFF) + ((u >> 16) & jnp.uint32(1))
    hi = u & jnp.uint32(0xFFFF0000)
    return hi | (hi >> 16)


def _peer_select_kernel(x1_ref, gffn_ref, wpq_ref, k1_ref, k2_ref,
                        xn_ref, cnt_ref, p1n_ref, rank_ref, p2_ref,
                        q_scr, c1_scr, c2_scr, v1_scr, v2_scr, n_scr, *, tb):
    xn = _rms(x1_ref[...], gffn_ref[...]).astype(_BF16)
    xn_ref[...] = xn
    q_scr[...] = _dot(xn, wpq_ref[...])
    k1 = k1_ref[...].astype(_BF16)
    k2 = k2_ref[...].astype(_BF16)
    half = PEER_DQ // 2
    K = PEER_TOPK

    def lane_group(h, ls):
        c1 = c1_scr[:, ls]
        c2 = c2_scr[:, ls]
        _pop_top(_sorted_columns(c1), v1_scr)
        _pop_top(_sorted_columns(c2), v2_scr)
        v1 = v1_scr[...]
        lists = [v1_scr[0:8, :] + v2_scr[b:b + 1, :] for b in range(8)]
        ea = v1_scr[8:16, :] + v2_scr[0:1, :]
        eb = v1_scr[0:1, :] + v2_scr[8:16, :]
        cands = lists + [ea, eb]
        tau = None
        for r in range(K):
            tau = jnp.max(jnp.maximum(jnp.maximum(lists[0], ea), eb), axis=0, keepdims=True)
            left = K - 1 - r
            if left:
                eq = lists[0] == tau
                for k in range(min(left, 7)):
                    lists[k] = jnp.where(eq, lists[k + 1], lists[k])
                if left >= 8:
                    lists[7] = jnp.where(eq, _NEG_INF, lists[7])
                ea = jnp.where(ea == tau, _NEG_INF, ea)
                eb = jnp.where(eb == tau, _NEG_INF, eb)
        z = sum(jnp.sum(jnp.where(c >= tau, jnp.exp(c), 0.0), axis=0, keepdims=True) for c in cands)
        n = jnp.zeros(v1.shape, _F32)
        for b in range(K):
            n = jnp.where(v1 + v2_scr[b:b + 1, :] >= tau, float(b + 1), n)
        n_scr[...] = n
        cnt = jnp.zeros(c1.shape, _F32)
        rank2 = jnp.full(c2.shape, float(K), _F32)
        for a in range(K - 1, -1, -1):
            cnt = jnp.where(c1 >= v1_scr[a:a + 1, :], n_scr[a:a + 1, :], cnt)
            rank2 = jnp.where(c2 >= v2_scr[a:a + 1, :], float(a), rank2)
        cnt_ref[h, :, ls] = _bf16_pair(cnt, exact=True)
        p1n_ref[h, :, ls] = _bf16_pair(jnp.exp(c1) / z)
        rank_ref[h, :, ls] = rank2.astype(_BF16)
        p2_ref[h, :, ls] = jnp.exp(c2).astype(_BF16)

    def head(h, carry):
        q0 = pl.multiple_of(h * PEER_DQ, PEER_DQ)
        s1 = _dot_nt(k1, q_scr[:, pl.ds(q0, half)].astype(_BF16))
        s2 = _dot_nt(k2, q_scr[:, pl.ds(q0 + half, half)].astype(_BF16))
        c1_scr[...] = s1 - jnp.max(s1, axis=0, keepdims=True)
        c2_scr[...] = s2 - jnp.max(s2, axis=0, keepdims=True)

        def lane_groups(i, carry):
            for u in range(_SEL_UNROLL):
                lane_group(h, pl.ds(pl.multiple_of((i * _SEL_UNROLL + u) * 128, 128), 128))
            return carry

        lax.fori_loop(0, tb // (128 * _SEL_UNROLL), lane_groups, 0)
        return carry

    lax.fori_loop(0, PEER_HEADS, head, 0)


def _peer_select(x1, g_ffn, wpq_bf, k_sub1, k_sub2, *, tb):
    T, D = x1.shape
    assert T % tb == 0 and tb % (128 * _SEL_UNROLL) == 0
    sel = lambda dt: jax.ShapeDtypeStruct((PEER_HEADS, PEER_NKEYS, T), dt)
    sel_spec = pl.BlockSpec((PEER_HEADS, PEER_NKEYS, tb), lambda i: (0, 0, i))
    const2 = lambda i: (0, 0)
    return pl.pallas_call(
        functools.partial(_peer_select_kernel, tb=tb),
        out_shape=(jax.ShapeDtypeStruct((T, D), _BF16), sel(jnp.uint32), sel(jnp.uint32), sel(_BF16), sel(_BF16)),
        grid=(T // tb,),
        in_specs=[
            pl.BlockSpec((tb, D), lambda i: (i, 0)),
            pl.BlockSpec((1, D), const2),
            pl.BlockSpec((D, PEER_HEADS * PEER_DQ), const2),
            pl.BlockSpec((PEER_NKEYS, PEER_DQ // 2), const2),
            pl.BlockSpec((PEER_NKEYS, PEER_DQ // 2), const2),
        ],
        out_specs=(pl.BlockSpec((tb, D), lambda i: (i, 0)),) + (sel_spec,) * 4,
        scratch_shapes=[
            pltpu.VMEM((tb, PEER_HEADS * PEER_DQ), _F32),
            pltpu.VMEM((PEER_NKEYS, tb), _F32),
            pltpu.VMEM((PEER_NKEYS, tb), _F32),
            pltpu.VMEM((PEER_TOPK, 128), _F32),
            pltpu.VMEM((PEER_TOPK, 128), _F32),
            pltpu.VMEM((PEER_TOPK, 128), _F32),
        ],
        compiler_params=pltpu.CompilerParams(
            dimension_semantics=("arbitrary",), vmem_limit_bytes=_VMEM_LIMIT),
        name="peer_select",
    )(x1, g_ffn.reshape(1, D), wpq_bf, k_sub1, k_sub2)


def _peer_dense_kernel(xn_ref, u_ref, vtp_ref, vtl_ref, cnt_ref, p1n_ref, rank_ref, p2_ref, x1_ref,
                       gfin_ref, y_ref, acc_scr, st_scr, h_scr, *, tb, rows):
    j = pl.program_id(1)
    cur = lax.rem(j, 2)

    @pl.when(j == 0)
    def _():
        acc_scr[...] = jnp.zeros_like(acc_scr)
        h_scr[1] = jnp.zeros(h_scr.shape[1:], _BF16)

    tile = (PEER_NKEYS, 128)
    words = (PEER_NKEYS // 2, 128)
    rb = 2
    for b in range(rows // rb):
        bs = slice(b * rb * PEER_NKEYS, (b + 1) * rb * PEER_NKEYS)
        st_scr[bs, :] = _dot_nt(u_ref[bs, :], xn_ref[...])
        for r in range(b * rb, (b + 1) * rb):
            es = slice(r * PEER_NKEYS, (r + 1) * PEER_NKEYS)
            for lg in range(tb // 128):
                ls = slice(lg * 128, (lg + 1) * 128)
                gate = jnp.zeros(tile, _BF16)
                for h in range(PEER_HEADS):
                    cnt = pltpu.bitcast(jnp.broadcast_to(cnt_ref[h, r:r + 1, ls], words), _BF16)
                    pn = pltpu.bitcast(jnp.broadcast_to(p1n_ref[h, r:r + 1, ls], words), _BF16)
                    sel = jnp.minimum(jnp.maximum(cnt - rank_ref[h, :, ls], 0), pn)
                    gate = gate + sel * p2_ref[h, :, ls]
                act = jax.nn.gelu(st_scr[es, ls].astype(_BF16))
                h_scr[cur, es, ls] = gate * act
    for th in range(2):
        ts = slice(th * (tb // 2), (th + 1) * (tb // 2))
        acc_scr[:, ts] += _dot(vtp_ref[0], h_scr[1 - cur, :, ts])

    @pl.when(j == pl.num_programs(1) - 1)
    def _():
        acc = acc_scr[...] + _dot(vtl_ref[0], h_scr[cur])
        x2 = x1_ref[...] + acc.T
        y_ref[...] = _rms(x2, gfin_ref[...])


def _peer_dense(xn, u_bf, vt_bf, cnt, p1n, rank2, p2, x1, g_final, *, tb, ec):
    T, D = x1.shape
    rows = ec // PEER_NKEYS
    nchunks = PEER_NEXP // ec
    assert T % tb == 0 and PEER_NEXP % ec == 0 and rows % 8 == 0
    sel_all = pl.BlockSpec((PEER_HEADS, PEER_NKEYS, tb), lambda i, j: (0, 0, i))
    sel_rows = pl.BlockSpec((PEER_HEADS, rows, tb), lambda i, j: (0, j, i))
    tok = pl.BlockSpec((tb, D), lambda i, j: (i, 0))
    return pl.pallas_call(
        functools.partial(_peer_dense_kernel, tb=tb, rows=rows),
        out_shape=jax.ShapeDtypeStruct((T, D), _F32),
        grid=(T // tb, nchunks),
        in_specs=[
            tok,
            pl.BlockSpec((ec, D), lambda i, j: (j, 0)),
            pl.BlockSpec((1, D, ec), lambda i, j: (jnp.maximum(j - 1, 0), 0, 0)),
            pl.BlockSpec((1, D, ec), lambda i, j: (nchunks - 1, 0, 0), pipeline_mode=pl.Buffered(1)),
            sel_rows, sel_rows, sel_all, sel_all,
            tok,
            pl.BlockSpec((1, D), lambda i, j: (0, 0)),
        ],
        out_specs=tok,
        scratch_shapes=[
            pltpu.VMEM((D, tb), _F32),
            pltpu.VMEM((ec, tb), _F32),
            pltpu.VMEM((2, ec, tb), _BF16),
        ],
        compiler_params=pltpu.CompilerParams(
            dimension_semantics=("arbitrary", "arbitrary"), vmem_limit_bytes=_VMEM_LIMIT),
        name="peer_dense",
    )(xn, u_bf, vt_bf, vt_bf, cnt, p1n, rank2, p2, x1, g_final.reshape(1, D))


def _mixer_weights(g_mix, w_in, b_in, g_mlh, g_sgu, w_s, b_s, w_out):
    nq = ML_HEADS * ML_DQK
    nv = ML_HEADS * ML_DV
    o_q, o_k, o_v = 0, nq, 2 * nq
    o_ig = o_v + nv
    o_fg = o_ig + ML_HEADS
    o_og = o_fg + ML_HEADS
    o_su = o_og + nv
    o_sv = o_su + D_MODEL
    o_xq = o_sv + D_MODEL
    o_gt = o_xq + D_MODEL
    gate = lambda b: slice(o_gt + b * D_MODEL, o_gt + (b + 1) * D_MODEL)
    main = lambda a: jnp.concatenate(
        [a[..., o_q:o_ig],
         a[..., o_og:o_su], a[..., gate(0)],
         a[..., o_su:o_xq], a[..., gate(1)],
         a[..., o_xq:o_gt], a[..., gate(2)]], axis=-1)
    w_if = w_in[:, o_ig:o_og]
    b_if = b_in[o_ig:o_og]
    return (
        g_mix.reshape(1, D_MODEL),
        main(w_in).astype(_BF16),
        main(b_in).reshape(1, _P_COLS),
        w_if.astype(_BF16),
        w_if.T.astype(_BF16),
        b_if.reshape(1, 8),
        b_if.reshape(8, 1),
        g_mlh.reshape(1, D_MODEL),
        g_sgu.reshape(1, D_MODEL),
        w_s,
        b_s.T,
        w_out.astype(_BF16),
    )


def kernel(x_prompt, x_sample, mem_prompt, cache_mem_k, cache_mem_v, state_mlstm_C, state_mlstm_n,
           state_mlstm_m, g_mix, w_in, b_in, g_mlh, g_sgu, w_s, b_s, g_mem, w_mk, w_mv, w_out, g_ffn,
           w_pq, k_sub1, k_sub2, u_exp, v_exp, g_final):
    depth = g_mix.shape[0]
    assert depth == 1
    l = 0
    Bp, Sp, D = x_prompt.shape
    Bs, Ss, _ = x_sample.shape

    wts = _mixer_weights(g_mix[l], w_in[l], b_in[l], g_mlh[l], g_sgu[l], w_s[l], b_s[l], w_out[l])

    mk, mv = _mem_kv(mem_prompt, g_mem[l], w_mk[l], w_mv[l])
    zc = jnp.zeros((Bp, ML_HEADS, ML_DQK, ML_DV), _F32)
    zn = jnp.zeros((Bp, ML_HEADS, ML_DQK), _F32)
    zm = jnp.zeros((Bp, ML_HEADS), _F32)
    x1p, cp, np_, mp, _ = _mixer(x_prompt, mk, mv, zc, zn, zm, wts,
                                 tc=min(_MIXER_TC, Sp), want_vn=False)
    x1s, cs, ns, ms, vn = _mixer(
        x_sample, cache_mem_k[l].reshape(Bs, N_MEM, D), cache_mem_v[l].reshape(Bs, N_MEM, D),
        state_mlstm_C[l], state_mlstm_n[l], state_mlstm_m[l], wts,
        tc=min(_MIXER_TC, Ss), want_vn=True)

    wpq_bf = w_pq[l].astype(_BF16)
    u_bf = u_exp[l].astype(_BF16)
    vt_bf = v_exp[l].astype(_BF16).reshape(PEER_NEXP // _PEER_EC, _PEER_EC, D).transpose(0, 2, 1)

    def peer(x1):
        B, S, _ = x1.shape
        x1 = x1.reshape(B * S, D)
        xn, cnt, p1n, rank2, p2 = _peer_select(x1, g_ffn[l], wpq_bf, k_sub1[l], k_sub2[l], tb=_PEER_TB)
        y = _peer_dense(xn, u_bf, vt_bf, cnt, p1n, rank2, p2, x1, g_final, tb=_PEER_TB, ec=_PEER_EC)
        return y.reshape(B, S, D)

    hs = (XA_HEADS, XA_DH)
    return (peer(x1p), peer(x1s), cp[None], np_[None], mp[None],
            mk.reshape(1, Bp, N_MEM, *hs), mv.reshape(1, Bp, N_MEM, *hs),
            cs[None], ns[None], ms[None], vn[None])
```

```python
import functools

import jax
import jax.numpy as jnp
from jax import lax
from jax.experimental import pallas as pl
from jax.experimental.pallas import tpu as pltpu

D_MODEL = 1024
EPS = 1e-6
CHUNK = 64
N_MEM = 256
ML_HEADS = 4
ML_DQK = 128
ML_DV = D_MODEL // ML_HEADS
SGU_CHUNK = 128
SGU_GROUPS = 4
SGU_GDIM = D_MODEL // SGU_GROUPS
XA_HEADS = 4
XA_DH = D_MODEL // XA_HEADS
PEER_HEADS = 8
PEER_NKEYS = 128
PEER_DQ = 256
PEER_TOPK = 16
PEER_NEXP = PEER_NKEYS * PEER_NKEYS

_SEG_A = ML_HEADS * (2 * ML_DQK + ML_DV)
_SEG_B = 2 * D_MODEL
_SEG_C = 3 * D_MODEL
_SEG_D = 2 * D_MODEL
_SEG0 = (0, _SEG_A, _SEG_A + _SEG_B, _SEG_A + _SEG_B + _SEG_C)
_P_COLS = _SEG_A + _SEG_B + _SEG_C + _SEG_D
_K0 = ML_HEADS * ML_DQK
_V0 = 2 * ML_HEADS * ML_DQK

_VMEM_LIMIT = 56 * 1024 * 1024

_MIXER_TC = 256
_PEER_TB = 512
_PEER_EC = 2048

_BF16 = jnp.bfloat16
_F32 = jnp.float32
_NEG_INF = float("-inf")


def _rms(xf, g):
    return xf * lax.rsqrt(jnp.mean(xf * xf, axis=-1, keepdims=True) + EPS) * g


def _dot(a, b):
    return jnp.dot(a, b, preferred_element_type=_F32)


def _dot_nt(a, b):
    return lax.dot_general(a, b, (((1,), (1,)), ((), ())), preferred_element_type=_F32)


def _bmm(a, b, ca, cb):
    return lax.dot_general(a, b, (((ca,), (cb,)), ((0,), (0,))), preferred_element_type=_F32)


def _mem_kv_kernel(mem_ref, g_ref, wk_ref, wv_ref, k_ref, v_ref):
    mn = _rms(mem_ref[0], g_ref[...]).astype(_BF16)
    k_ref[0] = _dot(mn, wk_ref[...])
    v_ref[0] = _dot(mn, wv_ref[...])


def _mem_kv(mem, g_mem, w_mk, w_mv):
    B = mem.shape[0]
    full = lambda b: (0, 0)
    return pl.pallas_call(
        _mem_kv_kernel,
        out_shape=(jax.ShapeDtypeStruct((B, N_MEM, D_MODEL), _F32),) * 2,
        grid=(B,),
        in_specs=[
            pl.BlockSpec((1, N_MEM, D_MODEL), lambda b: (b, 0, 0)),
            pl.BlockSpec((1, D_MODEL), full),
            pl.BlockSpec((D_MODEL, D_MODEL), full),
            pl.BlockSpec((D_MODEL, D_MODEL), full),
        ],
        out_specs=(pl.BlockSpec((1, N_MEM, D_MODEL), lambda b: (b, 0, 0)),) * 2,
        compiler_params=pltpu.CompilerParams(
            dimension_semantics=("arbitrary",), vmem_limit_bytes=_VMEM_LIMIT),
        name="mem_kv",
    )(mem, g_mem.reshape(1, D_MODEL), w_mk.astype(_BF16), w_mv.astype(_BF16))


def _mixer_kernel(x_ref, mk_ref, mv_ref, c0_ref, n0_ref, m0_ref, gmix_ref, win_ref, bin_ref,
                  wif_ref, wift_ref, bif_ref, bift_ref, gmlh_ref, gsgu_ref, ws_ref, bst_ref,
                  wout_ref,
                  x1_ref, c_out_ref, n_out_ref, m_out_ref, vn_ref,
                  pa_scr, pb_scr, pc_scr, pd_scr, mrg_scr, c_scr, n_scr, m_scr, *, tc, sgu_len):
    ci = pl.program_id(1)
    nsub = tc // CHUNK

    @pl.when(ci == 0)
    def _():
        c_scr[...] = c0_ref[0]
        n_scr[...] = n0_ref[0]
        m_scr[...] = m0_ref[0]

    x = x_ref[0]
    xn = _rms(x, gmix_ref[...]).astype(_BF16)

    def project(seg, out_scr):
        cols = slice(_SEG0[seg], _SEG0[seg] + out_scr.shape[1])
        out_scr[...] = _dot(xn, win_ref[:, cols]) + bin_ref[:, cols]

    project(0, pa_scr)
    gif = _dot(xn, wif_ref[...]) + bif_ref[...]
    gift = _dot_nt(wift_ref[...], xn) + bift_ref[...]

    row = lax.broadcasted_iota(jnp.int32, (CHUNK, CHUNK), 0)
    col = lax.broadcasted_iota(jnp.int32, (CHUNK, CHUNK), 1)
    tril = row >= col
    tril_f = tril.astype(_F32)
    triu_f = (row <= col).astype(_F32)
    stack = lambda f: jnp.stack([f(h) for h in range(ML_HEADS)])
    for j in range(nsub):
        r0 = j * CHUNK
        rows = slice(r0, r0 + CHUNK)
        ig_c = gif[rows, 0:ML_HEADS]
        lf_c = jax.nn.log_sigmoid(gif[rows, ML_HEADS:2 * ML_HEADS])
        ig_r = gift[0:ML_HEADS, rows]
        lf_r = jax.nn.log_sigmoid(gift[ML_HEADS:2 * ML_HEADS, rows])
        b_c = jnp.dot(tril_f, lf_c, precision=lax.Precision.HIGHEST,
                      preferred_element_type=_F32)
        b_r = jnp.dot(lf_r, triu_f, precision=lax.Precision.HIGHEST,
                      preferred_element_type=_F32)
        q = stack(lambda h: pa_scr[rows, h * ML_DQK:(h + 1) * ML_DQK]).astype(_BF16)
        k = stack(lambda h: pa_scr[rows, _K0 + h * ML_DQK:_K0 + (h + 1) * ML_DQK]) * (ML_DQK ** -0.5)
        v = stack(lambda h: pa_scr[rows, _V0 + h * ML_DV:_V0 + (h + 1) * ML_DV]).astype(_BF16)
        c_st = c_scr[...]
        n_st = stack(lambda h: n_scr[h:h + 1, :])
        m_st = stack(lambda h: m_scr[h:h + 1, 0:1])
        bc = stack(lambda h: b_c[:, h:h + 1])
        br = stack(lambda h: b_r[h:h + 1, :])
        ig_row = stack(lambda h: ig_r[h:h + 1, :])
        ig_col = stack(lambda h: ig_c[:, h:h + 1])
        logd = jnp.where(tril[None], bc - br + ig_row, _NEG_INF)
        log_prev = bc + m_st
        m_t = jnp.maximum(log_prev, jnp.max(logd, axis=2, keepdims=True))
        a = _bmm(q, k.astype(_BF16), 2, 2) * jnp.exp(logd - m_t)
        wp = jnp.exp(log_prev - m_t)
        num = _bmm(a.astype(_BF16), v, 2, 1) + wp * _bmm(q, c_st.astype(_BF16), 2, 1)
        qn = jnp.sum(q.astype(_F32) * n_st, axis=2, keepdims=True)
        den = jnp.sum(a, axis=2, keepdims=True) + wp * qn
        hh = num / jnp.maximum(jnp.abs(den), jnp.exp(-m_t))
        hn = hh * lax.rsqrt(jnp.mean(hh * hh, axis=2, keepdims=True) + EPS)
        for h in range(ML_HEADS):
            mrg_scr[rows, h * ML_DV:(h + 1) * ML_DV] = hn[h]
        b_last = bc[:, CHUNK - 1:CHUNK, :]
        log_in = b_last - bc + ig_col
        m_new = jnp.maximum(b_last + m_st, jnp.max(log_in, axis=1, keepdims=True))
        wi = jnp.exp(log_in - m_new)
        wc = jnp.exp(b_last + m_st - m_new)
        kw = wi * k
        c_scr[...] = wc * c_st + _bmm(kw.astype(_BF16), v, 1, 1)
        n_new = wc * n_st + jnp.sum(kw, axis=1, keepdims=True)
        for h in range(ML_HEADS):
            n_scr[h:h + 1, :] = n_new[h]
            m_scr[h:h + 1, :] = jnp.broadcast_to(m_new[h], (1, 128))

    project(1, pb_scr)
    h_a = jax.nn.sigmoid(pb_scr[:, 0:D_MODEL]) * mrg_scr[...] * gmlh_ref[...]
    mrg_scr[...] = jax.nn.sigmoid(pb_scr[:, D_MODEL:2 * D_MODEL]) * h_a

    project(2, pc_scr)
    sv = jax.nn.gelu(pc_scr[:, D_MODEL:2 * D_MODEL])
    vn = _rms(sv, gsgu_ref[...])
    if vn_ref is not None:
        vn_ref[0] = vn
    vnb = vn.astype(_BF16)
    rs = lax.broadcasted_iota(jnp.int32, (sgu_len, sgu_len), 0)
    cs = lax.broadcasted_iota(jnp.int32, (sgu_len, sgu_len), 1)
    for g in range(SGU_GROUPS):
        wsg = jnp.where(rs >= cs, ws_ref[g, 0:sgu_len, 0:sgu_len], 0.0).astype(_BF16)
        bsg = bst_ref[0:sgu_len, g:g + 1]
        gl = slice(g * SGU_GDIM, (g + 1) * SGU_GDIM)
        for c in range(tc // sgu_len):
            rows = slice(c * sgu_len, (c + 1) * sgu_len)
            mix = _dot(wsg, vnb[rows, gl]) + bsg
            u = jax.nn.gelu(pc_scr[rows, g * SGU_GDIM:(g + 1) * SGU_GDIM])
            g1 = jax.nn.sigmoid(pc_scr[rows, 2 * D_MODEL + g * SGU_GDIM:2 * D_MODEL + (g + 1) * SGU_GDIM])
            mrg_scr[rows, gl] += g1 * (u * mix)

    project(3, pd_scr)
    for h in range(XA_HEADS):
        hl = slice(h * XA_DH, (h + 1) * XA_DH)
        xq = pd_scr[:, h * XA_DH:(h + 1) * XA_DH].astype(_BF16)
        sc = _dot_nt(xq, mk_ref[0, :, hl].astype(_BF16)) * (XA_DH ** -0.5)
        sc = sc - jnp.max(sc, axis=1, keepdims=True)
        e = jnp.exp(sc)
        att = e / jnp.sum(e, axis=1, keepdims=True)
        h_c = _dot(att.astype(_BF16), mv_ref[0, :, hl].astype(_BF16))
        g2 = jax.nn.sigmoid(pd_scr[:, D_MODEL + h * XA_DH:D_MODEL + (h + 1) * XA_DH])
        mrg_scr[:, hl] += g2 * h_c

    x1_ref[0] = x + _dot(mrg_scr[...].astype(_BF16), wout_ref[...])

    @pl.when(ci == pl.num_programs(1) - 1)
    def _():
        c_out_ref[0] = c_scr[...]
        n_out_ref[0] = n_scr[...]
        m_out_ref[0] = m_scr[...]


def _mixer(x, mem_k, mem_v, c0, n0, m0, wts, *, tc, want_vn):
    B, S, D = x.shape
    sgu_len = min(S, SGU_CHUNK)
    assert S % tc == 0 and tc % CHUNK == 0 and tc % sgu_len == 0
    nchunks = S // tc
    m0p = jnp.broadcast_to(m0[:, :, None], (B, ML_HEADS, 128))
    m0p = jnp.concatenate([m0p, jnp.zeros((B, 8 - ML_HEADS, 128), _F32)], axis=1)

    def body(*refs):
        ins, rest = refs[:18], refs[18:]
        if want_vn:
            outs, scr = rest[:5], rest[5:]
        else:
            outs, scr = rest[:4] + (None,), rest[4:]
        _mixer_kernel(*ins, *outs, *scr, tc=tc, sgu_len=sgu_len)

    const2 = lambda b, c: (0, 0)
    const3 = lambda b, c: (0, 0, 0)
    per_b3 = lambda b, c: (b, 0, 0)
    per_b4 = lambda b, c: (b, 0, 0, 0)
    once = dict(pipeline_mode=pl.Buffered(1))
    in_specs = [
        pl.BlockSpec((1, tc, D), lambda b, c: (b, c, 0)),
        pl.BlockSpec((1, N_MEM, D), per_b3),
        pl.BlockSpec((1, N_MEM, D), per_b3),
        pl.BlockSpec((1, ML_HEADS, ML_DQK, ML_DV), per_b4),
        pl.BlockSpec((1, ML_HEADS, ML_DQK), per_b3),
        pl.BlockSpec((1, 8, 128), per_b3),
        pl.BlockSpec((1, D), const2, **once),
        pl.BlockSpec((D, _P_COLS), const2, **once),
        pl.BlockSpec((1, _P_COLS), const2, **once),
        pl.BlockSpec((D, 8), const2, **once),
        pl.BlockSpec((8, D), const2, **once),
        pl.BlockSpec((1, 8), const2, **once),
        pl.BlockSpec((8, 1), const2, **once),
        pl.BlockSpec((1, D), const2, **once),
        pl.BlockSpec((1, D), const2, **once),
        pl.BlockSpec((SGU_GROUPS, SGU_CHUNK, SGU_CHUNK), const3, **once),
        pl.BlockSpec((SGU_CHUNK, SGU_GROUPS), const2, **once),
        pl.BlockSpec((D, D), const2, **once),
    ]
    out_shape = [
        jax.ShapeDtypeStruct((B, S, D), _F32),
        jax.ShapeDtypeStruct((B, ML_HEADS, ML_DQK, ML_DV), _F32),
        jax.ShapeDtypeStruct((B, ML_HEADS, ML_DQK), _F32),
        jax.ShapeDtypeStruct((B, 8, 128), _F32),
    ]
    out_specs = [
        pl.BlockSpec((1, tc, D), lambda b, c: (b, c, 0)),
        pl.BlockSpec((1, ML_HEADS, ML_DQK, ML_DV), per_b4),
        pl.BlockSpec((1, ML_HEADS, ML_DQK), per_b3),
        pl.BlockSpec((1, 8, 128), per_b3),
    ]
    if want_vn:
        out_shape.append(jax.ShapeDtypeStruct((B, S, D), _F32))
        out_specs.append(pl.BlockSpec((1, tc, D), lambda b, c: (b, c, 0)))
    outs = pl.pallas_call(
        body,
        out_shape=tuple(out_shape),
        grid=(B, nchunks),
        in_specs=in_specs,
        out_specs=tuple(out_specs),
        scratch_shapes=[
            pltpu.VMEM((tc, _SEG_A), _F32),
            pltpu.VMEM((tc, _SEG_B), _F32),
            pltpu.VMEM((tc, _SEG_C), _F32),
            pltpu.VMEM((tc, _SEG_D), _F32),
            pltpu.VMEM((tc, D), _F32),
            pltpu.VMEM((ML_HEADS, ML_DQK, ML_DV), _F32),
            pltpu.VMEM((ML_HEADS, ML_DQK), _F32),
            pltpu.VMEM((8, 128), _F32),
        ],
        compiler_params=pltpu.CompilerParams(
            dimension_semantics=("arbitrary", "arbitrary"), vmem_limit_bytes=_VMEM_LIMIT),
        name="mixer",
    )(x, mem_k, mem_v, c0, n0, m0p, *wts)
    x1, c1, n1, m1p = outs[:4]
    vn = outs[4] if want_vn else None
    return x1, c1, n1, m1p[:, :ML_HEADS, 0], vn


def _batcher_pairs(n):
    pairs = []
    p = 1
    while p < n:
        k = p
        while k >= 1:
            for j in range(k % p, n - k, 2 * k):
                for i in range(min(k, n - j - k)):
                    if (i + j) // (2 * p) == (i + j + k) // (2 * p):
                        pairs.append((i + j, i + j + k))
            k //= 2
        p *= 2
    return pairs


_SORT16 = _batcher_pairs(PEER_NKEYS // 8)
_SEL_UNROLL = 2


def _sorted_columns(c):
    cols = [c[8 * k:8 * k + 8, :] for k in range(PEER_NKEYS // 8)]
    for i, j in _SORT16:
        cols[i], cols[j] = jnp.maximum(cols[i], cols[j]), jnp.minimum(cols[i], cols[j])
    return cols


def _pop_top(cols, out_scr):
    for r in range(PEER_TOPK):
        mx = jnp.max(cols[0], axis=0, keepdims=True)
        out_scr[r:r + 1, :] = mx
        left = PEER_TOPK - 1 - r
        if left:
            eq = cols[0] == mx
            for k in range(left):
                cols[k] = jnp.where(eq, cols[k + 1], cols[k])


def _bf16_pair(x, exact=False):
    u = pltpu.bitcast(x, jnp.uint32)
    if not exact:
        u = u + jnp.uint32(0x7FFF) + ((u >> 16) & jnp.uint32(1))
    hi = u & jnp.uint32(0xFFFF0000)
    return hi | (hi >> 16)


def _peer_select_kernel(x1_ref, gffn_ref, wpq_ref, k1_ref, k2_ref,
                        xn_ref, cnt_ref, p1n_ref, rank_ref, p2_ref,
                        q_scr, c1_scr, c2_scr, v1_scr, v2_scr, n_scr, *, tb):
    xn = _rms(x1_ref[...], gffn_ref[...]).astype(_BF16)
    xn_ref[...] = xn
    q_scr[...] = _dot(xn, wpq_ref[...])
    k1 = k1_ref[...].astype(_BF16)
    k2 = k2_ref[...].astype(_BF16)
    half = PEER_DQ // 2
    K = PEER_TOPK

    def lane_group(h, ls):
        c1 = c1_scr[:, ls]
        c2 = c2_scr[:, ls]
        _pop_top(_sorted_columns(c1), v1_scr)
        _pop_top(_sorted_columns(c2), v2_scr)
        v1 = v1_scr[...]
        lists = [v1_scr[0:8, :] + v2_scr[b:b + 1, :] for b in range(8)]
        ea = v1_scr[8:16, :] + v2_scr[0:1, :]
        eb = v1_scr[0:1, :] + v2_scr[8:16, :]
        cands = lists + [ea, eb]
        tau = None
        for r in range(K):
            tau = jnp.max(jnp.maximum(jnp.maximum(lists[0], ea), eb), axis=0, keepdims=True)
            left = K - 1 - r
            if left:
                eq = lists[0] == tau
                for k in range(min(left, 7)):
                    lists[k] = jnp.where(eq, lists[k + 1], lists[k])
                if left >= 8:
                    lists[7] = jnp.where(eq, _NEG_INF, lists[7])
                ea = jnp.where(ea == tau, _NEG_INF, ea)
                eb = jnp.where(eb == tau, _NEG_INF, eb)
        z = sum(jnp.sum(jnp.where(c >= tau, jnp.exp(c), 0.0), axis=0, keepdims=True) for c in cands)
        n = jnp.zeros(v1.shape, _F32)
        for b in range(K):
            n = jnp.where(v1 + v2_scr[b:b + 1, :] >= tau, float(b + 1), n)
        n_scr[...] = n
        cnt = jnp.zeros(c1.shape, _F32)
        rank2 = jnp.full(c2.shape, float(K), _F32)
        for a in range(K - 1, -1, -1):
            cnt = jnp.where(c1 >= v1_scr[a:a + 1, :], n_scr[a:a + 1, :], cnt)
            rank2 = jnp.where(c2 >= v2_scr[a:a + 1, :], float(a), rank2)
        cnt_ref[h, :, ls] = _bf16_pair(cnt, exact=True)
        p1n_ref[h, :, ls] = _bf16_pair(jnp.exp(c1) / z)
        rank_ref[h, :, ls] = rank2.astype(_BF16)
        p2_ref[h, :, ls] = jnp.exp(c2).astype(_BF16)

    def head(h, carry):
        q0 = pl.multiple_of(h * PEER_DQ, PEER_DQ)
        s1 = _dot_nt(k1, q_scr[:, pl.ds(q0, half)].astype(_BF16))
        s2 = _dot_nt(k2, q_scr[:, pl.ds(q0 + half, half)].astype(_BF16))
        c1_scr[...] = s1 - jnp.max(s1, axis=0, keepdims=True)
        c2_scr[...] = s2 - jnp.max(s2, axis=0, keepdims=True)

        def lane_groups(i, carry):
            for u in range(_SEL_UNROLL):
                lane_group(h, pl.ds(pl.multiple_of((i * _SEL_UNROLL + u) * 128, 128), 128))
            return carry

        lax.fori_loop(0, tb // (128 * _SEL_UNROLL), lane_groups, 0)
        return carry

    lax.fori_loop(0, PEER_HEADS, head, 0)


def _peer_select(x1, g_ffn, wpq_bf, k_sub1, k_sub2, *, tb):
    T, D = x1.shape
    assert T % tb == 0 and tb % (128 * _SEL_UNROLL) == 0
    sel = lambda dt: jax.ShapeDtypeStruct((PEER_HEADS, PEER_NKEYS, T), dt)
    sel_spec = pl.BlockSpec((PEER_HEADS, PEER_NKEYS, tb), lambda i: (0, 0, i))
    const2 = lambda i: (0, 0)
    return pl.pallas_call(
        functools.partial(_peer_select_kernel, tb=tb),
        out_shape=(jax.ShapeDtypeStruct((T, D), _BF16), sel(jnp.uint32), sel(jnp.uint32), sel(_BF16), sel(_BF16)),
        grid=(T // tb,),
        in_specs=[
            pl.BlockSpec((tb, D), lambda i: (i, 0)),
            pl.BlockSpec((1, D), const2),
            pl.BlockSpec((D, PEER_HEADS * PEER_DQ), const2),
            pl.BlockSpec((PEER_NKEYS, PEER_DQ // 2), const2),
            pl.BlockSpec((PEER_NKEYS, PEER_DQ // 2), const2),
        ],
        out_specs=(pl.BlockSpec((tb, D), lambda i: (i, 0)),) + (sel_spec,) * 4,
        scratch_shapes=[
            pltpu.VMEM((tb, PEER_HEADS * PEER_DQ), _F32),
            pltpu.VMEM((PEER_NKEYS, tb), _F32),
            pltpu.VMEM((PEER_NKEYS, tb), _F32),
            pltpu.VMEM((PEER_TOPK, 128), _F32),
            pltpu.VMEM((PEER_TOPK, 128), _F32),
            pltpu.VMEM((PEER_TOPK, 128), _F32),
        ],
        compiler_params=pltpu.CompilerParams(
            dimension_semantics=("arbitrary",), vmem_limit_bytes=_VMEM_LIMIT),
        name="peer_select",
    )(x1, g_ffn.reshape(1, D), wpq_bf, k_sub1, k_sub2)


def _peer_dense_kernel(xn_ref, u_ref, vtp_ref, vtl_ref, cnt_ref, p1n_ref, rank_ref, p2_ref, x1_ref,
                       gfin_ref, y_ref, acc_scr, st_scr, h_scr, *, tb, rows):
    j = pl.program_id(1)
    cur = lax.rem(j, 2)

    @pl.when(j == 0)
    def _():
        acc_scr[...] = jnp.zeros_like(acc_scr)
        h_scr[1] = jnp.zeros(h_scr.shape[1:], _BF16)

    tile = (PEER_NKEYS, 128)
    words = (PEER_NKEYS // 2, 128)
    rb = 2
    for b in range(rows // rb):
        bs = slice(b * rb * PEER_NKEYS, (b + 1) * rb * PEER_NKEYS)
        st_scr[bs, :] = _dot_nt(u_ref[bs, :], xn_ref[...])
        for lg in range(tb // 128):
            ls = slice(lg * 128, (lg + 1) * 128)
            for r in range(b * rb, (b + 1) * rb):
                es = slice(r * PEER_NKEYS, (r + 1) * PEER_NKEYS)
                gate = jnp.zeros(tile, _BF16)
                for h in range(PEER_HEADS):
                    cnt = pltpu.bitcast(jnp.broadcast_to(cnt_ref[h, r:r + 1, ls], words), _BF16)
                    pn = pltpu.bitcast(jnp.broadcast_to(p1n_ref[h, r:r + 1, ls], words), _BF16)
                    sel = jnp.minimum(jnp.maximum(cnt - rank_ref[h, :, ls], 0), pn)
                    gate = gate + sel * p2_ref[h, :, ls]
                act = jax.nn.gelu(st_scr[es, ls].astype(_BF16))
                h_scr[cur, es, ls] = gate * act
    acc_scr[...] += _dot(vtp_ref[0], h_scr[1 - cur])

    @pl.when(j == pl.num_programs(1) - 1)
    def _():
        acc = acc_scr[...] + _dot(vtl_ref[0], h_scr[cur])
        x2 = x1_ref[...] + acc.T
        y_ref[...] = _rms(x2, gfin_ref[...])


def _peer_dense(xn, u_bf, vt_bf, cnt, p1n, rank2, p2, x1, g_final, *, tb, ec):
    T, D = x1.shape
    rows = ec // PEER_NKEYS
    nchunks = PEER_NEXP // ec
    assert T % tb == 0 and PEER_NEXP % ec == 0 and rows % 8 == 0
    sel_all = pl.BlockSpec((PEER_HEADS, PEER_NKEYS, tb), lambda i, j: (0, 0, i))
    sel_rows = pl.BlockSpec((PEER_HEADS, rows, tb), lambda i, j: (0, j, i))
    tok = pl.BlockSpec((tb, D), lambda i, j: (i, 0))
    return pl.pallas_call(
        functools.partial(_peer_dense_kernel, tb=tb, rows=rows),
        out_shape=jax.ShapeDtypeStruct((T, D), _F32),
        grid=(T // tb, nchunks),
        in_specs=[
            tok,
            pl.BlockSpec((ec, D), lambda i, j: (j, 0)),
            pl.BlockSpec((1, D, ec), lambda i, j: (jnp.maximum(j - 1, 0), 0, 0)),
            pl.BlockSpec((1, D, ec), lambda i, j: (nchunks - 1, 0, 0), pipeline_mode=pl.Buffered(1)),
            sel_rows, sel_rows, sel_all, sel_all,
            tok,
            pl.BlockSpec((1, D), lambda i, j: (0, 0)),
        ],
        out_specs=tok,
        scratch_shapes=[
            pltpu.VMEM((D, tb), _F32),
            pltpu.VMEM((ec, tb), _F32),
            pltpu.VMEM((2, ec, tb), _BF16),
        ],
        compiler_params=pltpu.CompilerParams(
            dimension_semantics=("arbitrary", "arbitrary"), vmem_limit_bytes=_VMEM_LIMIT),
        name="peer_dense",
    )(xn, u_bf, vt_bf, vt_bf, cnt, p1n, rank2, p2, x1, g_final.reshape(1, D))


def _mixer_weights(g_mix, w_in, b_in, g_mlh, g_sgu, w_s, b_s, w_out):
    nq = ML_HEADS * ML_DQK
    nv = ML_HEADS * ML_DV
    o_q, o_k, o_v = 0, nq, 2 * nq
    o_ig = o_v + nv
    o_fg = o_ig + ML_HEADS
    o_og = o_fg + ML_HEADS
    o_su = o_og + nv
    o_sv = o_su + D_MODEL
    o_xq = o_sv + D_MODEL
    o_gt = o_xq + D_MODEL
    gate = lambda b: slice(o_gt + b * D_MODEL, o_gt + (b + 1) * D_MODEL)
    main = lambda a: jnp.concatenate(
        [a[..., o_q:o_ig],
         a[..., o_og:o_su], a[..., gate(0)],
         a[..., o_su:o_xq], a[..., gate(1)],
         a[..., o_xq:o_gt], a[..., gate(2)]], axis=-1)
    w_if = w_in[:, o_ig:o_og]
    b_if = b_in[o_ig:o_og]
    return (
        g_mix.reshape(1, D_MODEL),
        main(w_in).astype(_BF16),
        main(b_in).reshape(1, _P_COLS),
        w_if.astype(_BF16),
        w_if.T.astype(_BF16),
        b_if.reshape(1, 8),
        b_if.reshape(8, 1),
        g_mlh.reshape(1, D_MODEL),
        g_sgu.reshape(1, D_MODEL),
        w_s,
        b_s.T,
        w_out.astype(_BF16),
    )


def kernel(x_prompt, x_sample, mem_prompt, cache_mem_k, cache_mem_v, state_mlstm_C, state_mlstm_n,
           state_mlstm_m, g_mix, w_in, b_in, g_mlh, g_sgu, w_s, b_s, g_mem, w_mk, w_mv, w_out, g_ffn,
           w_pq, k_sub1, k_sub2, u_exp, v_exp, g_final):
    depth = g_mix.shape[0]
    assert depth == 1
    l = 0
    Bp, Sp, D = x_prompt.shape
    Bs, Ss, _ = x_sample.shape

    wts = _mixer_weights(g_mix[l], w_in[l], b_in[l], g_mlh[l], g_sgu[l], w_s[l], b_s[l], w_out[l])

    mk, mv = _mem_kv(mem_prompt, g_mem[l], w_mk[l], w_mv[l])
    zc = jnp.zeros((Bp, ML_HEADS, ML_DQK, ML_DV), _F32)
    zn = jnp.zeros((Bp, ML_HEADS, ML_DQK), _F32)
    zm = jnp.zeros((Bp, ML_HEADS), _F32)
    x1p, cp, np_, mp, _ = _mixer(x_prompt, mk, mv, zc, zn, zm, wts,
                                 tc=min(_MIXER_TC, Sp), want_vn=False)
    x1s, cs, ns, ms, vn = _mixer(
        x_sample, cache_mem_k[l].reshape(Bs, N_MEM, D), cache_mem_v[l].reshape(Bs, N_MEM, D),
        state_mlstm_C[l], state_mlstm_n[l], state_mlstm_m[l], wts,
        tc=min(_MIXER_TC, Ss), want_vn=True)

    wpq_bf = w_pq[l].astype(_BF16)
    u_bf = u_exp[l].astype(_BF16)
    vt_bf = v_exp[l].astype(_BF16).reshape(PEER_NEXP // _PEER_EC, _PEER_EC, D).transpose(0, 2, 1)

    def peer(x1):
        B, S, _ = x1.shape
        x1 = x1.reshape(B * S, D)
        xn, cnt, p1n, rank2, p2 = _peer_select(x1, g_ffn[l], wpq_bf, k_sub1[l], k_sub2[l], tb=_PEER_TB)
        y = _peer_dense(xn, u_bf, vt_bf, cnt, p1n, rank2, p2, x1, g_final, tb=_PEER_TB, ec=_PEER_EC)
        return y.reshape(B, S, D)

    hs = (XA_HEADS, XA_DH)
    return (peer(x1p), peer(x1s), cp[None], np_[None], mp[None],
            mk.reshape(1, Bp, N_MEM, *hs), mv.reshape(1, Bp, N_MEM, *hs),
            cs[None], ns[None], ms[None], vn[None])
```

```python
import functools

import jax
import jax.numpy as jnp
from jax import lax
from jax.experimental import pallas as pl
from jax.experimental.pallas import tpu as pltpu

D_MODEL = 1024
EPS = 1e-6
CHUNK = 64
N_MEM = 256
ML_HEADS = 4
ML_DQK = 128
ML_DV = D_MODEL // ML_HEADS
SGU_CHUNK = 128
SGU_GROUPS = 4
SGU_GDIM = D_MODEL // SGU_GROUPS
XA_HEADS = 4
XA_DH = D_MODEL // XA_HEADS
PEER_HEADS = 8
PEER_NKEYS = 128
PEER_DQ = 256
PEER_TOPK = 16
PEER_NEXP = PEER_NKEYS * PEER_NKEYS

_SEG_A = ML_HEADS * (2 * ML_DQK + ML_DV)
_SEG_B = 2 * D_MODEL
_SEG_C = 3 * D_MODEL
_SEG_D = 2 * D_MODEL
_SEG0 = (0, _SEG_A, _SEG_A + _SEG_B, _SEG_A + _SEG_B + _SEG_C)
_P_COLS = _SEG_A + _SEG_B + _SEG_C + _SEG_D
_K0 = ML_HEADS * ML_DQK
_V0 = 2 * ML_HEADS * ML_DQK

_VMEM_LIMIT = 56 * 1024 * 1024

_MIXER_TC = 256
_PEER_TB = 512
_PEER_EC = 2048

_BF16 = jnp.bfloat16
_F32 = jnp.float32
_NEG_INF = float("-inf")


def _rms(xf, g):
    return xf * lax.rsqrt(jnp.mean(xf * xf, axis=-1, keepdims=True) + EPS) * g


def _dot(a, b):
    return jnp.dot(a, b, preferred_element_type=_F32)


def _dot_nt(a, b):
    return lax.dot_general(a, b, (((1,), (1,)), ((), ())), preferred_element_type=_F32)


def _bmm(a, b, ca, cb):
    return lax.dot_general(a, b, (((ca,), (cb,)), ((0,), (0,))), preferred_element_type=_F32)


def _mem_kv_kernel(mem_ref, g_ref, wk_ref, wv_ref, k_ref, v_ref):
    mn = _rms(mem_ref[0], g_ref[...]).astype(_BF16)
    k_ref[0] = _dot(mn, wk_ref[...])
    v_ref[0] = _dot(mn, wv_ref[...])


def _mem_kv(mem, g_mem, w_mk, w_mv):
    B = mem.shape[0]
    full = lambda b: (0, 0)
    return pl.pallas_call(
        _mem_kv_kernel,
        out_shape=(jax.ShapeDtypeStruct((B, N_MEM, D_MODEL), _F32),) * 2,
        grid=(B,),
        in_specs=[
            pl.BlockSpec((1, N_MEM, D_MODEL), lambda b: (b, 0, 0)),
            pl.BlockSpec((1, D_MODEL), full),
            pl.BlockSpec((D_MODEL, D_MODEL), full),
            pl.BlockSpec((D_MODEL, D_MODEL), full),
        ],
        out_specs=(pl.BlockSpec((1, N_MEM, D_MODEL), lambda b: (b, 0, 0)),) * 2,
        compiler_params=pltpu.CompilerParams(
            dimension_semantics=("arbitrary",), vmem_limit_bytes=_VMEM_LIMIT),
        name="mem_kv",
    )(mem, g_mem.reshape(1, D_MODEL), w_mk.astype(_BF16), w_mv.astype(_BF16))


def _mixer_kernel(x_ref, mk_ref, mv_ref, c0_ref, n0_ref, m0_ref, gmix_ref, win_ref, bin_ref,
                  wif_ref, wift_ref, bif_ref, bift_ref, gmlh_ref, gsgu_ref, ws_ref, bst_ref,
                  wout_ref,
                  x1_ref, c_out_ref, n_out_ref, m_out_ref, vn_ref,
                  pa_scr, pb_scr, pc_scr, pd_scr, mrg_scr, c_scr, n_scr, m_scr, *, tc, sgu_len):
    ci = pl.program_id(1)
    nsub = tc // CHUNK

    @pl.when(ci == 0)
    def _():
        c_scr[...] = c0_ref[0]
        n_scr[...] = n0_ref[0]
        m_scr[...] = m0_ref[0]

    x = x_ref[0]
    xn = _rms(x, gmix_ref[...]).astype(_BF16)

    def project(seg, out_scr):
        cols = slice(_SEG0[seg], _SEG0[seg] + out_scr.shape[1])
        out_scr[...] = _dot(xn, win_ref[:, cols]) + bin_ref[:, cols]

    project(0, pa_scr)
    gif = _dot(xn, wif_ref[...]) + bif_ref[...]
    gift = _dot_nt(wift_ref[...], xn) + bift_ref[...]

    row = lax.broadcasted_iota(jnp.int32, (CHUNK, CHUNK), 0)
    col = lax.broadcasted_iota(jnp.int32, (CHUNK, CHUNK), 1)
    tril = row >= col
    tril_f = tril.astype(_F32)
    triu_f = (row <= col).astype(_F32)
    stack = lambda f: jnp.stack([f(h) for h in range(ML_HEADS)])
    for j in range(nsub):
        r0 = j * CHUNK
        rows = slice(r0, r0 + CHUNK)
        ig_c = gif[rows, 0:ML_HEADS]
        lf_c = jax.nn.log_sigmoid(gif[rows, ML_HEADS:2 * ML_HEADS])
        ig_r = gift[0:ML_HEADS, rows]
        lf_r = jax.nn.log_sigmoid(gift[ML_HEADS:2 * ML_HEADS, rows])
        b_c = jnp.dot(tril_f, lf_c, precision=lax.Precision.HIGHEST,
                      preferred_element_type=_F32)
        b_r = jnp.dot(lf_r, triu_f, precision=lax.Precision.HIGHEST,
                      preferred_element_type=_F32)
        q = stack(lambda h: pa_scr[rows, h * ML_DQK:(h + 1) * ML_DQK]).astype(_BF16)
        k = stack(lambda h: pa_scr[rows, _K0 + h * ML_DQK:_K0 + (h + 1) * ML_DQK]) * (ML_DQK ** -0.5)
        v = stack(lambda h: pa_scr[rows, _V0 + h * ML_DV:_V0 + (h + 1) * ML_DV]).astype(_BF16)
        c_st = c_scr[...]
        n_st = stack(lambda h: n_scr[h:h + 1, :])
        m_st = stack(lambda h: m_scr[h:h + 1, 0:1])
        bc = stack(lambda h: b_c[:, h:h + 1])
        br = stack(lambda h: b_r[h:h + 1, :])
        ig_row = stack(lambda h: ig_r[h:h + 1, :])
        ig_col = stack(lambda h: ig_c[:, h:h + 1])
        logd = jnp.where(tril[None], bc - br + ig_row, _NEG_INF)
        log_prev = bc + m_st
        m_t = jnp.maximum(log_prev, jnp.max(logd, axis=2, keepdims=True))
        a = _bmm(q, k.astype(_BF16), 2, 2) * jnp.exp(logd - m_t)
        wp = jnp.exp(log_prev - m_t)
        num = _bmm(a.astype(_BF16), v, 2, 1) + wp * _bmm(q, c_st.astype(_BF16), 2, 1)
        qn = jnp.sum(q.astype(_F32) * n_st, axis=2, keepdims=True)
        den = jnp.sum(a, axis=2, keepdims=True) + wp * qn
        hh = num / jnp.maximum(jnp.abs(den), jnp.exp(-m_t))
        hn = hh * lax.rsqrt(jnp.mean(hh * hh, axis=2, keepdims=True) + EPS)
        for h in range(ML_HEADS):
            mrg_scr[rows, h * ML_DV:(h + 1) * ML_DV] = hn[h]
        b_last = bc[:, CHUNK - 1:CHUNK, :]
        log_in = b_last - bc + ig_col
        m_new = jnp.maximum(b_last + m_st, jnp.max(log_in, axis=1, keepdims=True))
        wi = jnp.exp(log_in - m_new)
        wc = jnp.exp(b_last + m_st - m_new)
        kw = wi * k
        c_scr[...] = wc * c_st + _bmm(kw.astype(_BF16), v, 1, 1)
        n_new = wc * n_st + jnp.sum(kw, axis=1, keepdims=True)
        for h in range(ML_HEADS):
            n_scr[h:h + 1, :] = n_new[h]
            m_scr[h:h + 1, :] = jnp.broadcast_to(m_new[h], (1, 128))

    project(1, pb_scr)
    h_a = jax.nn.sigmoid(pb_scr[:, 0:D_MODEL]) * mrg_scr[...] * gmlh_ref[...]
    mrg_scr[...] = jax.nn.sigmoid(pb_scr[:, D_MODEL:2 * D_MODEL]) * h_a

    project(2, pc_scr)
    sv = jax.nn.gelu(pc_scr[:, D_MODEL:2 * D_MODEL])
    vn = _rms(sv, gsgu_ref[...])
    if vn_ref is not None:
        vn_ref[0] = vn
    vnb = vn.astype(_BF16)
    rs = lax.broadcasted_iota(jnp.int32, (sgu_len, sgu_len), 0)
    cs = lax.broadcasted_iota(jnp.int32, (sgu_len, sgu_len), 1)
    for g in range(SGU_GROUPS):
        wsg = jnp.where(rs >= cs, ws_ref[g, 0:sgu_len, 0:sgu_len], 0.0).astype(_BF16)
        bsg = bst_ref[0:sgu_len, g:g + 1]
        gl = slice(g * SGU_GDIM, (g + 1) * SGU_GDIM)
        for c in range(tc // sgu_len):
            rows = slice(c * sgu_len, (c + 1) * sgu_len)
            mix = _dot(wsg, vnb[rows, gl]) + bsg
            u = jax.nn.gelu(pc_scr[rows, g * SGU_GDIM:(g + 1) * SGU_GDIM])
            g1 = jax.nn.sigmoid(pc_scr[rows, 2 * D_MODEL + g * SGU_GDIM:2 * D_MODEL + (g + 1) * SGU_GDIM])
            mrg_scr[rows, gl] += g1 * (u * mix)

    project(3, pd_scr)
    for h in range(XA_HEADS):
        hl = slice(h * XA_DH, (h + 1) * XA_DH)
        xq = pd_scr[:, h * XA_DH:(h + 1) * XA_DH].astype(_BF16)
        sc = _dot_nt(xq, mk_ref[0, :, hl].astype(_BF16)) * (XA_DH ** -0.5)
        sc = sc - jnp.max(sc, axis=1, keepdims=True)
        e = jnp.exp(sc)
        att = e / jnp.sum(e, axis=1, keepdims=True)
        h_c = _dot(att.astype(_BF16), mv_ref[0, :, hl].astype(_BF16))
        g2 = jax.nn.sigmoid(pd_scr[:, D_MODEL + h * XA_DH:D_MODEL + (h + 1) * XA_DH])
        mrg_scr[:, hl] += g2 * h_c

    x1_ref[0] = x + _dot(mrg_scr[...].astype(_BF16), wout_ref[...])

    @pl.when(ci == pl.num_programs(1) - 1)
    def _():
        c_out_ref[0] = c_scr[...]
        n_out_ref[0] = n_scr[...]
        m_out_ref[0] = m_scr[...]


def _mixer(x, mem_k, mem_v, c0, n0, m0, wts, *, tc, want_vn):
    B, S, D = x.shape
    sgu_len = min(S, SGU_CHUNK)
    assert S % tc == 0 and tc % CHUNK == 0 and tc % sgu_len == 0
    nchunks = S // tc
    m0p = jnp.broadcast_to(m0[:, :, None], (B, ML_HEADS, 128))
    m0p = jnp.concatenate([m0p, jnp.zeros((B, 8 - ML_HEADS, 128), _F32)], axis=1)

    def body(*refs):
        ins, rest = refs[:18], refs[18:]
        if want_vn:
            outs, scr = rest[:5], rest[5:]
        else:
            outs, scr = rest[:4] + (None,), rest[4:]
        _mixer_kernel(*ins, *outs, *scr, tc=tc, sgu_len=sgu_len)

    const2 = lambda b, c: (0, 0)
    const3 = lambda b, c: (0, 0, 0)
    per_b3 = lambda b, c: (b, 0, 0)
    per_b4 = lambda b, c: (b, 0, 0, 0)
    once = dict(pipeline_mode=pl.Buffered(1))
    in_specs = [
        pl.BlockSpec((1, tc, D), lambda b, c: (b, c, 0)),
        pl.BlockSpec((1, N_MEM, D), per_b3),
        pl.BlockSpec((1, N_MEM, D), per_b3),
        pl.BlockSpec((1, ML_HEADS, ML_DQK, ML_DV), per_b4),
        pl.BlockSpec((1, ML_HEADS, ML_DQK), per_b3),
        pl.BlockSpec((1, 8, 128), per_b3),
        pl.BlockSpec((1, D), const2, **once),
        pl.BlockSpec((D, _P_COLS), const2, **once),
        pl.BlockSpec((1, _P_COLS), const2, **once),
        pl.BlockSpec((D, 8), const2, **once),
        pl.BlockSpec((8, D), const2, **once),
        pl.BlockSpec((1, 8), const2, **once),
        pl.BlockSpec((8, 1), const2, **once),
        pl.BlockSpec((1, D), const2, **once),
        pl.BlockSpec((1, D), const2, **once),
        pl.BlockSpec((SGU_GROUPS, SGU_CHUNK, SGU_CHUNK), const3, **once),
        pl.BlockSpec((SGU_CHUNK, SGU_GROUPS), const2, **once),
        pl.BlockSpec((D, D), const2, **once),
    ]
    out_shape = [
        jax.ShapeDtypeStruct((B, S, D), _F32),
        jax.ShapeDtypeStruct((B, ML_HEADS, ML_DQK, ML_DV), _F32),
        jax.ShapeDtypeStruct((B, ML_HEADS, ML_DQK), _F32),
        jax.ShapeDtypeStruct((B, 8, 128), _F32),
    ]
    out_specs = [
        pl.BlockSpec((1, tc, D), lambda b, c: (b, c, 0)),
        pl.BlockSpec((1, ML_HEADS, ML_DQK, ML_DV), per_b4),
        pl.BlockSpec((1, ML_HEADS, ML_DQK), per_b3),
        pl.BlockSpec((1, 8, 128), per_b3),
    ]
    if want_vn:
        out_shape.append(jax.ShapeDtypeStruct((B, S, D), _F32))
        out_specs.append(pl.BlockSpec((1, tc, D), lambda b, c: (b, c, 0)))
    outs = pl.pallas_call(
        body,
        out_shape=tuple(out_shape),
        grid=(B, nchunks),
        in_specs=in_specs,
        out_specs=tuple(out_specs),
        scratch_shapes=[
            pltpu.VMEM((tc, _SEG_A), _F32),
            pltpu.VMEM((tc, _SEG_B), _F32),
            pltpu.VMEM((tc, _SEG_C), _F32),
            pltpu.VMEM((tc, _SEG_D), _F32),
            pltpu.VMEM((tc, D), _F32),
            pltpu.VMEM((ML_HEADS, ML_DQK, ML_DV), _F32),
            pltpu.VMEM((ML_HEADS, ML_DQK), _F32),
            pltpu.VMEM((8, 128), _F32),
        ],
        compiler_params=pltpu.CompilerParams(
            dimension_semantics=("arbitrary", "arbitrary"), vmem_limit_bytes=_VMEM_LIMIT),
        name="mixer",
    )(x, mem_k, mem_v, c0, n0, m0p, *wts)
    x1, c1, n1, m1p = outs[:4]
    vn = outs[4] if want_vn else None
    return x1, c1, n1, m1p[:, :ML_HEADS, 0], vn


def _batcher_pairs(n):
    pairs = []
    p = 1
    while p < n:
        k = p
        while k >= 1:
            for j in range(k % p, n - k, 2 * k):
                for i in range(min(k, n - j - k)):
                    if (i + j) // (2 * p) == (i + j + k) // (2 * p):
                        pairs.append((i + j, i + j + k))
            k //= 2
        p *= 2
    return pairs


_SORT16 = _batcher_pairs(PEER_NKEYS // 8)
_SEL_UNROLL = 2


def _sorted_columns(c):
    cols = [c[8 * k:8 * k + 8, :] for k in range(PEER_NKEYS // 8)]
    for i, j in _SORT16:
        cols[i], cols[j] = jnp.maximum(cols[i], cols[j]), jnp.minimum(cols[i], cols[j])
    return cols


def _pop_top(cols, out_scr):
    for r in range(PEER_TOPK):
        mx = jnp.max(cols[0], axis=0, keepdims=True)
        out_scr[r:r + 1, :] = mx
        left = PEER_TOPK - 1 - r
        if left:
            eq = cols[0] == mx
            for k in range(left):
                cols[k] = jnp.where(eq, cols[k + 1], cols[k])


def _bf16_pair(x, exact=False):
    u = pltpu.bitcast(x, jnp.uint32)
    if not exact:
        u = u + jnp.uint32(0x7FFF) + ((u >> 16) & jnp.uint32(1))
    hi = u & jnp.uint32(0xFFFF0000)
    return hi | (hi >> 16)


def _peer_select_kernel(x1_ref, gffn_ref, wpq_ref, k1_ref, k2_ref,
                        xn_ref, cnt_ref, p1n_ref, rank_ref, p2_ref,
                        q_scr, c1_scr, c2_scr, v1_scr, v2_scr, n_scr, *, tb):
    xn = _rms(x1_ref[...], gffn_ref[...]).astype(_BF16)
    xn_ref[...] = xn
    q_scr[...] = _dot(xn, wpq_ref[...])
    k1 = k1_ref[...].astype(_BF16)
    k2 = k2_ref[...].astype(_BF16)
    half = PEER_DQ // 2
    K = PEER_TOPK

    def lane_group(h, ls):
        c1 = c1_scr[:, ls]
        c2 = c2_scr[:, ls]
        _pop_top(_sorted_columns(c1), v1_scr)
        _pop_top(_sorted_columns(c2), v2_scr)
        v1 = v1_scr[...]
        lists = [v1_scr[0:8, :] + v2_scr[b:b + 1, :] for b in range(8)]
        ea = v1_scr[8:16, :] + v2_scr[0:1, :]
        eb = v1_scr[0:1, :] + v2_scr[8:16, :]
        cands = lists + [ea, eb]
        tau = None
        for r in range(K):
            tau = jnp.max(jnp.maximum(jnp.maximum(lists[0], ea), eb), axis=0, keepdims=True)
            left = K - 1 - r
            if left:
                eq = lists[0] == tau
                for k in range(min(left, 7)):
                    lists[k] = jnp.where(eq, lists[k + 1], lists[k])
                if left >= 8:
                    lists[7] = jnp.where(eq, _NEG_INF, lists[7])
                ea = jnp.where(ea == tau, _NEG_INF, ea)
                eb = jnp.where(eb == tau, _NEG_INF, eb)
        z = sum(jnp.sum(jnp.where(c >= tau, jnp.exp(c), 0.0), axis=0, keepdims=True) for c in cands)
        n = jnp.zeros(v1.shape, _F32)
        for b in range(K):
            n = jnp.where(v1 + v2_scr[b:b + 1, :] >= tau, float(b + 1), n)
        n_scr[...] = n
        cnt = jnp.zeros(c1.shape, _F32)
        rank2 = jnp.full(c2.shape, float(K), _F32)
        for a in range(K - 1, -1, -1):
            cnt = jnp.where(c1 >= v1_scr[a:a + 1, :], n_scr[a:a + 1, :], cnt)
            rank2 = jnp.where(c2 >= v2_scr[a:a + 1, :], float(a), rank2)
        cnt_ref[h, :, ls] = _bf16_pair(cnt, exact=True)
        p1n_ref[h, :, ls] = _bf16_pair(jnp.exp(c1) / z)
        rank_ref[h, :, ls] = rank2.astype(_BF16)
        p2_ref[h, :, ls] = jnp.exp(c2).astype(_BF16)

    def head(h, carry):
        q0 = pl.multiple_of(h * PEER_DQ, PEER_DQ)
        s1 = _dot_nt(k1, q_scr[:, pl.ds(q0, half)].astype(_BF16))
        s2 = _dot_nt(k2, q_scr[:, pl.ds(q0 + half, half)].astype(_BF16))
        c1_scr[...] = s1 - jnp.max(s1, axis=0, keepdims=True)
        c2_scr[...] = s2 - jnp.max(s2, axis=0, keepdims=True)

        def lane_groups(i, carry):
            for u in range(_SEL_UNROLL):
                lane_group(h, pl.ds(pl.multiple_of((i * _SEL_UNROLL + u) * 128, 128), 128))
            return carry

        lax.fori_loop(0, tb // (128 * _SEL_UNROLL), lane_groups, 0)
        return carry

    lax.fori_loop(0, PEER_HEADS, head, 0)


def _peer_select(x1, g_ffn, wpq_bf, k_sub1, k_sub2, *, tb):
    T, D = x1.shape
    assert T % tb == 0 and tb % (128 * _SEL_UNROLL) == 0
    sel = lambda dt: jax.ShapeDtypeStruct((PEER_HEADS, PEER_NKEYS, T), dt)
    sel_spec = pl.BlockSpec((PEER_HEADS, PEER_NKEYS, tb), lambda i: (0, 0, i))
    const2 = lambda i: (0, 0)
    return pl.pallas_call(
        functools.partial(_peer_select_kernel, tb=tb),
        out_shape=(jax.ShapeDtypeStruct((T, D), _BF16), sel(jnp.uint32), sel(jnp.uint32), sel(_BF16), sel(_BF16)),
        grid=(T // tb,),
        in_specs=[
            pl.BlockSpec((tb, D), lambda i: (i, 0)),
            pl.BlockSpec((1, D), const2),
            pl.BlockSpec((D, PEER_HEADS * PEER_DQ), const2),
            pl.BlockSpec((PEER_NKEYS, PEER_DQ // 2), const2),
            pl.BlockSpec((PEER_NKEYS, PEER_DQ // 2), const2),
        ],
        out_specs=(pl.BlockSpec((tb, D), lambda i: (i, 0)),) + (sel_spec,) * 4,
        scratch_shapes=[
            pltpu.VMEM((tb, PEER_HEADS * PEER_DQ), _F32),
            pltpu.VMEM((PEER_NKEYS, tb), _F32),
            pltpu.VMEM((PEER_NKEYS, tb), _F32),
            pltpu.VMEM((PEER_TOPK, 128), _F32),
            pltpu.VMEM((PEER_TOPK, 128), _F32),
            pltpu.VMEM((PEER_TOPK, 128), _F32),
        ],
        compiler_params=pltpu.CompilerParams(
            dimension_semantics=("arbitrary",), vmem_limit_bytes=_VMEM_LIMIT),
        name="peer_select",
    )(x1, g_ffn.reshape(1, D), wpq_bf, k_sub1, k_sub2)


def _peer_dense_kernel(xn_ref, u_ref, vtp_ref, vtl_ref, cnt_ref, p1n_ref, rank_ref, p2_ref, x1_ref,
                       gfin_ref, y_ref, acc_scr, st_scr, h_scr, *, tb, rows):
    j = pl.program_id(1)
    cur = lax.rem(j, 2)

    tile = (PEER_NKEYS, 128)
    words = (PEER_NKEYS // 2, 128)
    rb = 2

    def chunk_step(mix_previous):
        for b in range(rows // rb):
            bs = slice(b * rb * PEER_NKEYS, (b + 1) * rb * PEER_NKEYS)
            st_scr[bs, :] = _dot_nt(u_ref[bs, :], xn_ref[...])
            for r in range(b * rb, (b + 1) * rb):
                es = slice(r * PEER_NKEYS, (r + 1) * PEER_NKEYS)
                for lg in range(tb // 128):
                    ls = slice(lg * 128, (lg + 1) * 128)
                    gate = jnp.zeros(tile, _BF16)
                    for h in range(PEER_HEADS):
                        cnt = pltpu.bitcast(jnp.broadcast_to(cnt_ref[h, r:r + 1, ls], words), _BF16)
                        pn = pltpu.bitcast(jnp.broadcast_to(p1n_ref[h, r:r + 1, ls], words), _BF16)
                        sel = jnp.minimum(jnp.maximum(cnt - rank_ref[h, :, ls], 0), pn)
                        gate = gate + sel * p2_ref[h, :, ls]
                    act = jax.nn.gelu(st_scr[es, ls].astype(_BF16))
                    h_scr[cur, es, ls] = gate * act
        if mix_previous:
            acc_scr[...] += _dot(vtp_ref[0], h_scr[1 - cur])

    @pl.when(j == 0)
    def _():
        acc_scr[...] = jnp.zeros_like(acc_scr)
        chunk_step(False)

    @pl.when(j > 0)
    def _():
        chunk_step(True)

    @pl.when(j == pl.num_programs(1) - 1)
    def _():
        acc = acc_scr[...] + _dot(vtl_ref[0], h_scr[cur])
        x2 = x1_ref[...] + acc.T
        y_ref[...] = _rms(x2, gfin_ref[...])


def _peer_dense(xn, u_bf, vt_bf, cnt, p1n, rank2, p2, x1, g_final, *, tb, ec):
    T, D = x1.shape
    rows = ec // PEER_NKEYS
    nchunks = PEER_NEXP // ec
    assert T % tb == 0 and PEER_NEXP % ec == 0 and rows % 8 == 0
    sel_all = pl.BlockSpec((PEER_HEADS, PEER_NKEYS, tb), lambda i, j: (0, 0, i))
    sel_rows = pl.BlockSpec((PEER_HEADS, rows, tb), lambda i, j: (0, j, i))
    tok = pl.BlockSpec((tb, D), lambda i, j: (i, 0))
    return pl.pallas_call(
        functools.partial(_peer_dense_kernel, tb=tb, rows=rows),
        out_shape=jax.ShapeDtypeStruct((T, D), _F32),
        grid=(T // tb, nchunks),
        in_specs=[
            tok,
            pl.BlockSpec((ec, D), lambda i, j: (j, 0)),
            pl.BlockSpec((1, D, ec), lambda i, j: (jnp.maximum(j - 1, 0), 0, 0)),
            pl.BlockSpec((1, D, ec), lambda i, j: (nchunks - 1, 0, 0), pipeline_mode=pl.Buffered(1)),
            sel_rows, sel_rows, sel_all, sel_all,
            tok,
            pl.BlockSpec((1, D), lambda i, j: (0, 0)),
        ],
        out_specs=tok,
        scratch_shapes=[
            pltpu.VMEM((D, tb), _F32),
            pltpu.VMEM((ec, tb), _F32),
            pltpu.VMEM((2, ec, tb), _BF16),
        ],
        compiler_params=pltpu.CompilerParams(
            dimension_semantics=("arbitrary", "arbitrary"), vmem_limit_bytes=_VMEM_LIMIT),
        name="peer_dense",
    )(xn, u_bf, vt_bf, vt_bf, cnt, p1n, rank2, p2, x1, g_final.reshape(1, D))


def _mixer_weights(g_mix, w_in, b_in, g_mlh, g_sgu, w_s, b_s, w_out):
    nq = ML_HEADS * ML_DQK
    nv = ML_HEADS * ML_DV
    o_q, o_k, o_v = 0, nq, 2 * nq
    o_ig = o_v + nv
    o_fg = o_ig + ML_HEADS
    o_og = o_fg + ML_HEADS
    o_su = o_og + nv
    o_sv = o_su + D_MODEL
    o_xq = o_sv + D_MODEL
    o_gt = o_xq + D_MODEL
    gate = lambda b: slice(o_gt + b * D_MODEL, o_gt + (b + 1) * D_MODEL)
    main = lambda a: jnp.concatenate(
        [a[..., o_q:o_ig],
         a[..., o_og:o_su], a[..., gate(0)],
         a[..., o_su:o_xq], a[..., gate(1)],
         a[..., o_xq:o_gt], a[..., gate(2)]], axis=-1)
    w_if = w_in[:, o_ig:o_og]
    b_if = b_in[o_ig:o_og]
    return (
        g_mix.reshape(1, D_MODEL),
        main(w_in).astype(_BF16),
        main(b_in).reshape(1, _P_COLS),
        w_if.astype(_BF16),
        w_if.T.astype(_BF16),
        b_if.reshape(1, 8),
        b_if.reshape(8, 1),
        g_mlh.reshape(1, D_MODEL),
        g_sgu.reshape(1, D_MODEL),
        w_s,
        b_s.T,
        w_out.astype(_BF16),
    )


def kernel(x_prompt, x_sample, mem_prompt, cache_mem_k, cache_mem_v, state_mlstm_C, state_mlstm_n,
           state_mlstm_m, g_mix, w_in, b_in, g_mlh, g_sgu, w_s, b_s, g_mem, w_mk, w_mv, w_out, g_ffn,
           w_pq, k_sub1, k_sub2, u_exp, v_exp, g_final):
    depth = g_mix.shape[0]
    assert depth == 1
    l = 0
    Bp, Sp, D = x_prompt.shape
    Bs, Ss, _ = x_sample.shape

    wts = _mixer_weights(g_mix[l], w_in[l], b_in[l], g_mlh[l], g_sgu[l], w_s[l], b_s[l], w_out[l])

    mk, mv = _mem_kv(mem_prompt, g_mem[l], w_mk[l], w_mv[l])
    zc = jnp.zeros((Bp, ML_HEADS, ML_DQK, ML_DV), _F32)
    zn = jnp.zeros((Bp, ML_HEADS, ML_DQK), _F32)
    zm = jnp.zeros((Bp, ML_HEADS), _F32)
    x1p, cp, np_, mp, _ = _mixer(x_prompt, mk, mv, zc, zn, zm, wts,
                                 tc=min(_MIXER_TC, Sp), want_vn=False)
    x1s, cs, ns, ms, vn = _mixer(
        x_sample, cache_mem_k[l].reshape(Bs, N_MEM, D), cache_mem_v[l].reshape(Bs, N_MEM, D),
        state_mlstm_C[l], state_mlstm_n[l], state_mlstm_m[l], wts,
        tc=min(_MIXER_TC, Ss), want_vn=True)

    wpq_bf = w_pq[l].astype(_BF16)
    u_bf = u_exp[l].astype(_BF16)
    vt_bf = v_exp[l].astype(_BF16).reshape(PEER_NEXP // _PEER_EC, _PEER_EC, D).transpose(0, 2, 1)

    def peer(x1):
        B, S, _ = x1.shape
        x1 = x1.reshape(B * S, D)
        xn, cnt, p1n, rank2, p2 = _peer_select(x1, g_ffn[l], wpq_bf, k_sub1[l], k_sub2[l], tb=_PEER_TB)
        y = _peer_dense(xn, u_bf, vt_bf, cnt, p1n, rank2, p2, x1, g_final, tb=_PEER_TB, ec=_PEER_EC)
        return y.reshape(B, S, D)

    hs = (XA_HEADS, XA_DH)
    return (peer(x1p), peer(x1s), cp[None], np_[None], mp[None],
            mk.reshape(1, Bp, N_MEM, *hs), mv.reshape(1, Bp, N_MEM, *hs),
            cs[None], ns[None], ms[None], vn[None])
```
